```python
import math
import jax
import jax.numpy as jnp
from jax import lax
import numpy as np

D_MODEL = 1024
BATCH = 4
SEQ = 4096
DEPTH = 2
DEC_BATCH = 32
DEC_SEQ = 8
PAST_LEN = 16384
PAGE_SIZE = 128

HEAD_DIM = 64
D_MIX = D_MODEL
D_ATT = D_MIX // 2
D_RWKV = D_MIX - D_ATT
H_ATT = D_ATT // HEAD_DIM
H_RWKV = D_RWKV // HEAD_DIM
MOBA_BLOCK = 256
MOBA_TOPK = 3
Q_CHUNK = 128
N_BUCKETS = 32
MAX_DISTANCE = 128
LORA_W = 64
LORA_A = 64
LORA_G = 128
ATT_IN = 3 * D_ATT
RW_IN = 3 * D_RWKV + LORA_W + LORA_A + LORA_G
IN_COLS = ATT_IN + RW_IN
D_FF = 2816
N_EXPERTS = 8
TOP_K = 2
N_DENSE = (DEPTH + 1) // 2
N_MOE = DEPTH // 2
ALPHA = (2 * DEPTH) ** 0.25
BETA = (8 * DEPTH) ** -0.25
LN_EPS = 1e-5
LNX_EPS = 64e-5
NEG_INF = -1e30

kernel_name = 'hymba_moba_rwkv7_deepnorm_adaln_decoder_step'


def layer_norm(x, g, b, eps):
    xf = x.astype(jnp.float32)
    mu = jnp.mean(xf, axis=-1, keepdims=True)
    var = jnp.mean(jnp.square(xf - mu), axis=-1, keepdims=True)
    return (xf - mu) * lax.rsqrt(var + eps) * g.astype(jnp.float32) + b.astype(jnp.float32)


def rel_bucket(dist):
    max_exact = N_BUCKETS // 2
    d = jnp.maximum(dist, 0)
    log_ratio = jnp.log(jnp.maximum(d, 1).astype(jnp.float32) / max_exact) / math.log(MAX_DISTANCE / max_exact)
    large = jnp.minimum(max_exact + (log_ratio * (N_BUCKETS - max_exact)).astype(jnp.int32), N_BUCKETS - 1)
    return jnp.where(d < max_exact, d, large)


def attend(s_own, v_own, s_sel, v_sel):
    if s_sel is None:
        p = jax.nn.softmax(s_own, axis=-1)
        return jnp.einsum('...qk,...kd->...qd', p, v_own.astype(jnp.float32))
    n_flat = s_sel.shape[-2] * s_sel.shape[-1]
    s = jnp.concatenate([s_sel.reshape(s_sel.shape[:-2] + (n_flat,)), s_own], axis=-1)
    p = jax.nn.softmax(s, axis=-1)
    p_sel = p[..., :n_flat].reshape(s_sel.shape)
    p_own = p[..., n_flat:]
    return (jnp.einsum('...qk,...kd->...qd', p_own, v_own.astype(jnp.float32))
            + jnp.einsum('...qnk,...qnkd->...qd', p_sel, v_sel.astype(jnp.float32)))


def moba_prompt(q, k, v, rel_bias):
    n_seq, n_h, s_len, hd = q.shape
    n_blk = -(-s_len // MOBA_BLOCK)
    pad = n_blk * MOBA_BLOCK - s_len
    kb = jnp.pad(k, ((0, 0), (0, 0), (0, pad), (0, 0))).reshape(n_seq, n_h, n_blk, MOBA_BLOCK, hd)
    vb = jnp.pad(v, ((0, 0), (0, 0), (0, pad), (0, 0))).reshape(n_seq, n_h, n_blk, MOBA_BLOCK, hd)
    k_mean = jnp.mean(kb.astype(jnp.float32), axis=3)
    n_sel = min(MOBA_TOPK, n_blk - 1)
    n_qc = s_len // Q_CHUNK
    h_idx = jnp.arange(n_h)
    scale = HEAD_DIM ** -0.5
    blk_ar = jnp.arange(MOBA_BLOCK)

    def one_chunk(i):
        bi = i // n_qc
        q0 = (i % n_qc) * Q_CHUNK
        own = q0 // MOBA_BLOCK
        t = q0 + jnp.arange(Q_CHUNK)
        qc = lax.dynamic_slice_in_dim(lax.dynamic_index_in_dim(q, bi, 0, keepdims=False), q0, Q_CHUNK, axis=1).astype(jnp.float32)
        kb_b = lax.dynamic_index_in_dim(kb, bi, 0, keepdims=False)
        vb_b = lax.dynamic_index_in_dim(vb, bi, 0, keepdims=False)
        k_own = lax.dynamic_index_in_dim(kb_b, own, 1, keepdims=False)
        v_own = lax.dynamic_index_in_dim(vb_b, own, 1, keepdims=False)
        dist_own = t[:, None] - (own * MOBA_BLOCK + blk_ar)[None, :]
        s_own = (jnp.einsum('hqd,hkd->hqk', qc, k_own.astype(jnp.float32)) * scale
                 + jnp.transpose(rel_bias[rel_bucket(dist_own)], (2, 0, 1)).astype(jnp.float32))
        s_own = jnp.where(dist_own[None] >= 0, s_own, NEG_INF)
        if n_sel <= 0:
            return attend(s_own, v_own, None, None)
        km_b = lax.dynamic_index_in_dim(k_mean, bi, 0, keepdims=False)
        gate = jnp.einsum('hqd,hnd->hqn', qc, km_b)
        gate = jnp.where(jnp.arange(n_blk)[None, None, :] < own, gate, NEG_INF)
        _, sel = lax.top_k(gate, n_sel)
        k_sel = kb_b[h_idx[:, None, None], sel]
        v_sel = vb_b[h_idx[:, None, None], sel]
        dist = t[None, :, None, None] - (sel[..., None] * MOBA_BLOCK + blk_ar)
        s_sel = (jnp.einsum('hqd,hqnkd->hqnk', qc, k_sel.astype(jnp.float32)) * scale
                 + rel_bias[rel_bucket(dist), h_idx[:, None, None, None]].astype(jnp.float32))
        slot_ok = (jnp.arange(n_sel) < own)[None, None, :, None]
        s_sel = jnp.where(slot_ok, s_sel, NEG_INF)
        return attend(s_own, v_own, s_sel, v_sel)

    out = lax.map(one_chunk, jnp.arange(n_seq * n_qc))
    out = out.reshape(n_seq, n_qc, n_h, Q_CHUNK, hd).transpose(0, 1, 3, 2, 4)
    return out.reshape(n_seq, s_len, n_h * hd).astype(q.dtype)


def moba_sample(q, k_new, v_new, k_pool, v_pool, page_table, rel_bias):
    n_seq, n_h, t_len, hd = q.shape
    n_pages = page_table.shape[1]
    past = n_pages * PAGE_SIZE
    ppb = MOBA_BLOCK // PAGE_SIZE
    n_full = past // MOBA_BLOCK
    own_pages = n_pages - n_full * ppb
    own_rows = own_pages * PAGE_SIZE
    scale = HEAD_DIM ** -0.5
    qf = q.astype(jnp.float32)
    t = past + jnp.arange(t_len)
    hh = jnp.arange(n_h)[None, :, None, None, None]
    blk_ar = jnp.arange(MOBA_BLOCK)
    if own_pages > 0:
        pt_own = page_table[:, n_full * ppb:]
        k_past = k_pool[pt_own].transpose(0, 2, 1, 3, 4).reshape(n_seq, n_h, own_rows, hd)
        v_past = v_pool[pt_own].transpose(0, 2, 1, 3, 4).reshape(n_seq, n_h, own_rows, hd)
        k_own = jnp.concatenate([k_past.astype(k_new.dtype), k_new], axis=2)
        v_own = jnp.concatenate([v_past.astype(v_new.dtype), v_new], axis=2)
    else:
        k_own, v_own = k_new, v_new
    dist_own = t[:, None] - (past - own_rows + jnp.arange(own_rows + t_len))[None, :]
    s_own = (jnp.einsum('bhqd,bhkd->bhqk', qf, k_own.astype(jnp.float32)) * scale
             + jnp.transpose(rel_bias[rel_bucket(dist_own)], (2, 0, 1))[None].astype(jnp.float32))
    s_own = jnp.where(dist_own[None, None] >= 0, s_own, NEG_INF)
    n_sel = min(MOBA_TOPK, n_full)
    if n_sel <= 0:
        out = attend(s_own, v_own, None, None)
    else:
        k_rows = k_pool[page_table[:, :n_full * ppb]].astype(jnp.float32)
        k_mean = jnp.mean(k_rows.reshape(n_seq, n_full, ppb, n_h, PAGE_SIZE, hd), axis=(2, 4))
        gate = jnp.einsum('bhqd,bnhd->bhqn', qf, k_mean)
        _, sel = lax.top_k(gate, n_sel)
        logical = sel[..., None] * ppb + jnp.arange(ppb)
        phys = page_table[jnp.arange(n_seq)[:, None, None, None, None], logical]
        k_sel = k_pool[phys, hh].reshape(n_seq, n_h, t_len, n_sel, MOBA_BLOCK, hd)
        v_sel = v_pool[phys, hh].reshape(n_seq, n_h, t_len, n_sel, MOBA_BLOCK, hd)
        dist = t[None, None, :, None, None] - (sel[..., None] * MOBA_BLOCK + blk_ar)
        s_sel = (jnp.einsum('bhqd,bhqnkd->bhqnk', qf, k_sel.astype(jnp.float32)) * scale
                 + rel_bias[rel_bucket(dist), hh].astype(jnp.float32))
        out = attend(s_own, v_own, s_sel, v_sel)
    return out.transpose(0, 2, 1, 3).reshape(n_seq, t_len, n_h * hd).astype(q.dtype)


def rwkv7_mix(p, shift0, s0, mu, w0, w2, a0, a2, g2, k_k, k_a, r_k, lnx_g, lnx_b):
    n, L, _ = p.shape
    p_prev = jnp.concatenate([shift0[:, None, :].astype(p.dtype), p[:, :-1]], axis=1)
    xm = (p + (p_prev - p) * mu).astype(jnp.float32)
    o = 3 * D_RWKV
    r = xm[..., :D_RWKV]
    k = xm[..., D_RWKV:2 * D_RWKV]
    v = xm[..., 2 * D_RWKV:o]
    xw = xm[..., o:o + LORA_W]
    xa = xm[..., o + LORA_W:o + LORA_W + LORA_A]
    xg = xm[..., o + LORA_W + LORA_A:]
    w_log = -jax.nn.softplus(-(w0 + jnp.tanh(xw) @ w2)) - 0.5
    decay = jnp.exp(-jnp.exp(w_log))
    a = jax.nn.sigmoid(a0 + xa @ a2)
    g = jax.nn.sigmoid(xg) @ g2
    heads = lambda z: z.reshape(n, L, H_RWKV, HEAD_DIM)
    kk = heads(k * k_k)
    kk = kk / jnp.maximum(jnp.sqrt(jnp.sum(jnp.square(kk), axis=-1, keepdims=True)), 1e-12)
    k = k * (1 + (a - 1) * k_a)
    rh, wh, kh, vh, ah = heads(r), heads(decay), heads(k), heads(v), heads(a)
    bh = kk * ah

    def step(s, inp):
        r_t, w_t, k_t, v_t, kk_t, b_t = inp
        sa = jnp.einsum('nhij,nhj->nhi', s, -kk_t)
        s = s * w_t[:, :, None, :] + sa[..., None] * b_t[:, :, None, :] + v_t[..., None] * k_t[:, :, None, :]
        return s, jnp.einsum('nhij,nhj->nhi', s, r_t)

    xs = tuple(jnp.moveaxis(z, 1, 0) for z in (rh, wh, kh, vh, kk, bh))
    s_fin, ys = lax.scan(step, s0.astype(jnp.float32), xs)
    y = jnp.moveaxis(ys, 0, 1)
    ym = jnp.mean(y, axis=-1, keepdims=True)
    yv = jnp.mean(jnp.square(y - ym), axis=-1, keepdims=True)
    y = ((y - ym) * lax.rsqrt(yv + LNX_EPS)).reshape(n, L, D_RWKV) * lnx_g + lnx_b
    bonus = (jnp.sum(rh * kh * r_k, axis=-1, keepdims=True) * vh).reshape(n, L, D_RWKV)
    return ((y + bonus) * g).astype(p.dtype), s_fin, p[:, -1]


def swiglu(u, wg, wu, wd):
    return (jax.nn.silu(u @ wg) * (u @ wu)) @ wd


def moe_swiglu(u, w_router, wg, wu, wd):
    logits = (u @ w_router).astype(jnp.float32)
    top_v, top_i = lax.top_k(logits, TOP_K)
    top_w = jax.nn.softmax(top_v, axis=-1)
    gates = jnp.sum(jax.nn.one_hot(top_i, N_EXPERTS, dtype=jnp.float32) * top_w[..., None], axis=-2)
    out = jnp.zeros(u.shape, jnp.float32)
    for e in range(N_EXPERTS):
        out = out + gates[..., e:e + 1] * swiglu(u, wg[e], wu[e], wd[e]).astype(jnp.float32)
    return out.astype(u.dtype)


def to_pages(z):
    b, h, s, d = z.shape
    return z.reshape(b, h, s // PAGE_SIZE, PAGE_SIZE, d).transpose(0, 2, 1, 3, 4)


def setup_inputs(seed: int = 0) -> dict:
    key = jax.random.key(seed)
    ks = list(jax.random.split(key, 40))

    def nrm(i, shape, s):
        return jax.random.normal(ks[i], shape, jnp.float32) * s

    n_pages = PAST_LEN // PAGE_SIZE
    n_pool = (DEC_BATCH * n_pages * 5) // 4
    perm = jax.random.permutation(ks[0], n_pool)
    page_table = perm[:DEC_BATCH * n_pages].reshape(DEC_BATCH, n_pages).astype(jnp.int32)
    col_scale = (jnp.ones((IN_COLS,), jnp.float32)
                 .at[2 * D_ATT:3 * D_ATT].set(BETA)
                 .at[ATT_IN + 2 * D_RWKV:ATT_IN + 3 * D_RWKV].set(BETA))
    d_in = D_MODEL ** -0.5
    return {
        'x_prompt': nrm(1, (BATCH, SEQ, D_MODEL), 1.0),
        'x_sample': nrm(2, (DEC_BATCH, DEC_SEQ, D_MODEL), 1.0),
        'cache_k': nrm(3, (DEPTH, n_pool, H_ATT, PAGE_SIZE, HEAD_DIM), 1.0),
        'cache_v': nrm(4, (DEPTH, n_pool, H_ATT, PAGE_SIZE, HEAD_DIM), 1.0),
        'state_wkv': nrm(5, (DEPTH, DEC_BATCH, H_RWKV, HEAD_DIM, HEAD_DIM), 0.3),
        'state_shift': nrm(6, (DEPTH, DEC_BATCH, RW_IN), 1.0),
        'page_table': page_table,
        'c_prompt': nrm(7, (BATCH, D_MODEL), 1.0),
        'c_sample': nrm(8, (DEC_BATCH, D_MODEL), 1.0),
        'rel_bias': nrm(9, (N_BUCKETS, H_ATT), 0.5),
        'w_ada': nrm(10, (DEPTH, D_MODEL, 6 * D_MODEL), 0.2 * d_in),
        'b_ada': nrm(11, (DEPTH, 6 * D_MODEL), 0.01),
        'w_in': nrm(12, (DEPTH, D_MODEL, IN_COLS), d_in) * col_scale,
        'rw_mu': jax.random.uniform(ks[13], (DEPTH, RW_IN), jnp.float32),
        'rw_w0': jax.random.uniform(ks[14], (DEPTH, D_RWKV), jnp.float32, -3.0, 1.0),
        'rw_w2': nrm(15, (DEPTH, LORA_W, D_RWKV), 0.5 * LORA_W ** -0.5),
        'rw_a0': nrm(16, (DEPTH, D_RWKV), 0.5),
        'rw_a2': nrm(17, (DEPTH, LORA_A, D_RWKV), 0.5 * LORA_A ** -0.5),
        'rw_g2': nrm(18, (DEPTH, LORA_G, D_RWKV), LORA_G ** -0.5),
        'rw_k_k': 0.85 + nrm(19, (DEPTH, D_RWKV), 0.05),
        'rw_k_a': 1.0 + nrm(20, (DEPTH, D_RWKV), 0.05),
        'rw_r_k': nrm(21, (DEPTH, H_RWKV, HEAD_DIM), 0.1),
        'rw_lnx_g': 1.0 + nrm(22, (DEPTH, D_RWKV), 0.02),
        'rw_lnx_b': nrm(23, (DEPTH, D_RWKV), 0.02),
        'w_out': nrm(24, (DEPTH, D_MIX, D_MODEL), BETA * D_MIX ** -0.5),
        'ln1_g': 1.0 + nrm(25, (DEPTH, D_MODEL), 0.02),
        'ln1_b': nrm(26, (DEPTH, D_MODEL), 0.02),
        'ln2_g': 1.0 + nrm(27, (DEPTH, D_MODEL), 0.02),
        'ln2_b': nrm(28, (DEPTH, D_MODEL), 0.02),
        'ffn_w_gate': nrm(29, (N_DENSE, D_MODEL, D_FF), BETA * d_in),
        'ffn_w_up': nrm(30, (N_DENSE, D_MODEL, D_FF), BETA * d_in),
        'ffn_w_down': nrm(31, (N_DENSE, D_FF, D_MODEL), BETA * D_FF ** -0.5),
        'moe_w_router': nrm(32, (N_MOE, D_MODEL, N_EXPERTS), d_in),
        'moe_w_gate': nrm(33, (N_MOE, N_EXPERTS, D_MODEL, D_FF), BETA * d_in),
        'moe_w_up': nrm(34, (N_MOE, N_EXPERTS, D_MODEL, D_FF), BETA * d_in),
        'moe_w_down': nrm(35, (N_MOE, N_EXPERTS, D_FF, D_MODEL), BETA * D_FF ** -0.5),
    }


def reference(x_prompt, x_sample, cache_k, cache_v, state_wkv, state_shift, page_table, c_prompt, c_sample,
              rel_bias, w_ada, b_ada, w_in, rw_mu, rw_w0, rw_w2, rw_a0, rw_a2, rw_g2, rw_k_k, rw_k_a, rw_r_k,
              rw_lnx_g, rw_lnx_b, w_out, ln1_g, ln1_b, ln2_g, ln2_b, ffn_w_gate, ffn_w_up, ffn_w_down,
              moe_w_router, moe_w_gate, moe_w_up, moe_w_down):

    def layer(l, x, c, attn, shift0, s0):
        mod = jax.nn.silu(c) @ w_ada[l] + b_ada[l]
        sh1, sc1, gt1, sh2, sc2, gt2 = [m[:, None, :] for m in jnp.split(mod, 6, axis=-1)]
        u = x * (1 + sc1) + sh1
        p = u @ w_in[l]
        n, L, _ = p.shape
        qkv = p[..., :ATT_IN].reshape(n, L, 3, H_ATT, HEAD_DIM).transpose(2, 0, 3, 1, 4)
        q, k, v = qkv[0], qkv[1], qkv[2]
        y_att = attn(q, k, v)
        y_rw, s_new, shift_new = rwkv7_mix(p[..., ATT_IN:], shift0, s0, rw_mu[l], rw_w0[l], rw_w2[l],
                                           rw_a0[l], rw_a2[l], rw_g2[l], rw_k_k[l], rw_k_a[l], rw_r_k[l],
                                           rw_lnx_g[l], rw_lnx_b[l])
        o = jnp.concatenate([y_att, y_rw], axis=-1) @ w_out[l]
        x = layer_norm(ALPHA * x + (1 + gt1) * o, ln1_g[l], ln1_b[l], LN_EPS).astype(x.dtype)
        u = x * (1 + sc2) + sh2
        if l % 2 == 0:
            f = swiglu(u, ffn_w_gate[l // 2], ffn_w_up[l // 2], ffn_w_down[l // 2])
        else:
            f = moe_swiglu(u, moe_w_router[l // 2], moe_w_gate[l // 2], moe_w_up[l // 2], moe_w_down[l // 2])
        x = layer_norm(ALPHA * x + (1 + gt2) * f, ln2_g[l], ln2_b[l], LN_EPS).astype(x.dtype)
        return x, k, v, s_new, shift_new

    n_p = x_prompt.shape[0]
    attn_prompt = lambda q, k, v: moba_prompt(q, k, v, rel_bias)
    xp, xs = x_prompt, x_sample
    kp, vp, ksm, vsm, wp, wsm, hp, hsm = [], [], [], [], [], [], [], []
    for l in range(DEPTH):
        xp, k, v, s_new, sh_new = layer(l, xp, c_prompt, attn_prompt,
                                        jnp.zeros((n_p, RW_IN), x_prompt.dtype),
                                        jnp.zeros((n_p, H_RWKV, HEAD_DIM, HEAD_DIM), jnp.float32))
        kp.append(to_pages(k))
        vp.append(to_pages(v))
        wp.append(s_new)
        hp.append(sh_new)
        attn_sample = lambda q, k, v, l=l: moba_sample(q, k, v, cache_k[l], cache_v[l], page_table, rel_bias)
        xs, k, v, s_new, sh_new = layer(l, xs, c_sample, attn_sample, state_shift[l], state_wkv[l])
        ksm.append(k)
        vsm.append(v)
        wsm.append(s_new)
        hsm.append(sh_new)
    return (xp, xs, jnp.stack(kp), jnp.stack(vp), jnp.stack(ksm), jnp.stack(vsm),
            jnp.stack(wp), jnp.stack(wsm), jnp.stack(hp), jnp.stack(hsm))
```

```python
import functools
import math

import jax
import jax.numpy as jnp
from jax import lax
from jax.experimental import pallas as pl
from jax.experimental.pallas import tpu as pltpu

F32 = jnp.float32
BF16 = jnp.bfloat16
I32 = jnp.int32
HI = lax.Precision.HIGHEST

D_MODEL = 1024
HEAD_DIM = 64
H_ATT = 8
H_RWKV = 8
D_ATT = H_ATT * HEAD_DIM
D_RWKV = H_RWKV * HEAD_DIM
ATT_IN = 3 * D_ATT
LORA_W = 64
LORA_A = 64
LORA_G = 128
RW_IN = 3 * D_RWKV + LORA_W + LORA_A + LORA_G
MOBA_BLOCK = 256
MOBA_TOPK = 3
PAGE_SIZE = 128
N_BUCKETS = 32
MAX_EXACT = N_BUCKETS // 2
MAX_DISTANCE = 128
D_FF = 2816
N_EXPERTS = 8
DEPTH = 2
ALPHA = (2 * DEPTH) ** 0.25
LN_EPS = 1e-5
LNX_EPS = 64e-5
NEG_INF = -1e30
ATT_SCALE = HEAD_DIM ** -0.5

LANES = 128
HEAD_PAIR = LANES // HEAD_DIM
BF16_ROWS = 16
VMEM_LIMIT = 56 * 1024 * 1024


def _cparams(*sem):
    return pltpu.CompilerParams(dimension_semantics=sem, vmem_limit_bytes=VMEM_LIMIT)


def _sigmoid(x):
    return 1.0 / (1.0 + jnp.exp(-x))


def _dot_nt(a, b, precision=None):
    return lax.dot_general(a, b, (((1,), (1,)), ((), ())), precision=precision, preferred_element_type=F32)


def _dot_tn(a, b, precision=None):
    return lax.dot_general(a, b, (((0,), (0,)), ((), ())), precision=precision, preferred_element_type=F32)


def _dot(a, b, precision=None):
    return jnp.dot(a, b, precision=precision, preferred_element_type=F32)


def _ada_kernel(c_ref, w_ref, b_ref, o_ref):
    c = c_ref[...]
    o_ref[0] = _dot(c * _sigmoid(c), w_ref[0], precision=HI) + b_ref[0]


def _ada_call(c_all, w_ada, b_ada):
    n = c_all.shape[0]
    depth, d, cols = w_ada.shape
    tn = 768
    return pl.pallas_call(
        _ada_kernel,
        grid=(depth, cols // tn),
        in_specs=[pl.BlockSpec((n, d), lambda l, j: (0, 0)),
                  pl.BlockSpec((1, d, tn), lambda l, j: (l, 0, j)),
                  pl.BlockSpec((1, 1, tn), lambda l, j: (l, 0, j))],
        out_specs=pl.BlockSpec((1, n, tn), lambda l, j: (l, 0, j)),
        out_shape=jax.ShapeDtypeStruct((depth, n, cols), F32),
        compiler_params=_cparams("parallel", "parallel"),
        name="ada_mod",
    )(c_all, w_ada, b_ada.reshape(depth, 1, cols))


def _bucket_of(dist):
    d = jnp.maximum(dist, 0)
    log_ratio = jnp.log(jnp.maximum(d, 1).astype(F32) / MAX_EXACT) / math.log(MAX_DISTANCE / MAX_EXACT)
    large = jnp.minimum(MAX_EXACT + (log_ratio * (N_BUCKETS - MAX_EXACT)).astype(I32), N_BUCKETS - 1)
    return jnp.where(d < MAX_EXACT, d, large)


def _bias_kernel(rb_ref, *o_refs, offsets):
    for o_ref, off in zip(o_refs, offsets):
        n_h, rows, cols = o_ref.shape
        dist = (lax.broadcasted_iota(I32, (rows, cols), 0) - lax.broadcasted_iota(I32, (rows, cols), 1) + off)
        bucket = _bucket_of(dist)
        for h in range(n_h):
            acc = jnp.zeros((rows, cols), F32)
            for b in range(N_BUCKETS):
                acc = jnp.where(bucket == b, rb_ref[b, h], acc)
            o_ref[h] = jnp.where(dist >= 0, acc, NEG_INF)


def _bias_call(rel_bias, shapes, offsets):
    n_h = rel_bias.shape[1]
    return pl.pallas_call(
        functools.partial(_bias_kernel, offsets=tuple(offsets)),
        in_specs=[pl.BlockSpec(memory_space=pltpu.SMEM)],
        out_specs=[pl.BlockSpec(memory_space=pltpu.VMEM) for _ in shapes],
        out_shape=[jax.ShapeDtypeStruct((n_h,) + s, F32) for s in shapes],
        name="rel_bias_tables",
    )(rel_bias)


def _inproj_kernel(x_ref, sc_ref, sh_ref, w_ref, oa_ref, or_ref, *, chunk):
    u = (x_ref[0] * (1.0 + sc_ref[0]) + sh_ref[0]).astype(BF16)
    att = oa_ref.shape[-1]
    for c0 in range(0, w_ref.shape[1], chunk):
        res = _dot(u, w_ref[:, c0:c0 + chunk])
        if c0 < att:
            oa_ref[0, :, c0:c0 + chunk] = res
        else:
            or_ref[0, :, c0 - att:c0 - att + chunk] = res


def _mod_spec(mod, tm):
    if mod.shape[1] == 1:
        return pl.BlockSpec((1, 1, mod.shape[2]), lambda n, i: (n, 0, 0))
    return pl.BlockSpec((1, tm, mod.shape[2]), lambda n, i: (n, i, 0))


def _inproj_call(x, sc, sh, w_bf, tm):
    n, l, d = x.shape
    cols = w_bf.shape[1]
    return pl.pallas_call(
        functools.partial(_inproj_kernel, chunk=256),
        grid=(n, l // tm),
        in_specs=[pl.BlockSpec((1, tm, d), lambda n, i: (n, i, 0)),
                  _mod_spec(sc, tm), _mod_spec(sh, tm),
                  pl.BlockSpec((d, cols), lambda n, i: (0, 0))],
        out_specs=[pl.BlockSpec((1, tm, ATT_IN), lambda n, i: (n, i, 0)),
                   pl.BlockSpec((1, tm, RW_IN), lambda n, i: (n, i, 0))],
        out_shape=[jax.ShapeDtypeStruct((n, l, ATT_IN), F32),
                   jax.ShapeDtypeStruct((n, l, RW_IN), F32)],
        compiler_params=_cparams("parallel", "parallel"),
        name="in_proj",
    )(x, sc, sh, w_bf)


def _moba_prompt_kernel(rb_ref, q_ref, k_ref, v_ref, town_ref, tprev_ref, o_ref, kb_ref, vb_ref, km_ref,
                        *, n_blk):
    hp = pl.program_id(1)
    i = pl.program_id(2)
    blk = MOBA_BLOCK

    @pl.when(i == 0)
    def _():
        k = k_ref[0]
        kb_ref[...] = k.astype(BF16)
        vb_ref[...] = v_ref[0].astype(BF16)
        km_ref[...] = jnp.sum(k.reshape(n_blk, blk, LANES), axis=1) * (1.0 / blk)

    outs = []
    for h2 in range(HEAD_PAIR):
        lanes = slice(h2 * HEAD_DIM, (h2 + 1) * HEAD_DIM)
        q = q_ref[0][:, lanes]
        gate = _dot_nt(q, km_ref[:, lanes], precision=HI)
        bidx = lax.broadcasted_iota(I32, gate.shape, 1)
        past = bidx < i
        g = jnp.where(past, gate, NEG_INF)
        rank = jnp.zeros(gate.shape, I32)
        for jp in range(n_blk):
            gj = g[:, jp:jp + 1]
            rank = rank + ((gj > g) | ((gj == g) & (jp < bidx))).astype(I32)
        sel = jnp.where((rank < MOBA_TOPK) & past, 1.0, 0.0)

        qb = (q * ATT_SCALE).astype(BF16)
        own0 = pl.multiple_of(i * blk, blk)
        s = _dot_nt(qb, kb_ref[pl.ds(own0, blk), lanes]) + town_ref[h2]
        m0 = jnp.max(s, axis=1, keepdims=True)
        p = jnp.exp(s - m0)
        l0 = jnp.sum(p, axis=1, keepdims=True)
        acc0 = _dot(p.astype(BF16), vb_ref[pl.ds(own0, blk), lanes])
        b_far = rb_ref[N_BUCKETS - 1, hp * HEAD_PAIR + h2]

        def body(j, carry):
            m, l, acc = carry
            r0 = pl.multiple_of(j * blk, blk)
            s = _dot_nt(qb, kb_ref[pl.ds(r0, blk), lanes])
            s = s + jnp.where(j == i - 1, tprev_ref[h2], b_far)
            picked = jnp.sum(jnp.where(bidx == j, sel, 0.0), axis=1, keepdims=True) > 0.5
            s = jnp.where(picked, s, NEG_INF)
            m_new = jnp.maximum(m, jnp.max(s, axis=1, keepdims=True))
            a = jnp.exp(m - m_new)
            p = jnp.exp(s - m_new)
            l = a * l + jnp.sum(p, axis=1, keepdims=True)
            acc = a * acc + _dot(p.astype(BF16), vb_ref[pl.ds(r0, blk), lanes])
            return m_new, l, acc

        m, l, acc = lax.fori_loop(0, i, body, (m0, l0, acc0))
        outs.append(acc / l)
    o_ref[0] = jnp.concatenate(outs, axis=1)


def _moba_prompt_call(p_att, rel_bias, t_own, t_prev):
    n, l, _ = p_att.shape
    n_blk = l // MOBA_BLOCK
    n_hp = H_ATT // HEAD_PAIR
    tbl = pl.BlockSpec((HEAD_PAIR, MOBA_BLOCK, MOBA_BLOCK), lambda n, h, i: (h, 0, 0))
    return pl.pallas_call(
        functools.partial(_moba_prompt_kernel, n_blk=n_blk),
        grid=(n, n_hp, n_blk),
        in_specs=[pl.BlockSpec(memory_space=pltpu.SMEM),
                  pl.BlockSpec((1, MOBA_BLOCK, LANES), lambda n, h, i: (n, i, h)),
                  pl.BlockSpec((1, l, LANES), lambda n, h, i: (n, 0, n_hp + h)),
                  pl.BlockSpec((1, l, LANES), lambda n, h, i: (n, 0, 2 * n_hp + h)),
                  tbl, tbl],
        out_specs=pl.BlockSpec((1, MOBA_BLOCK, LANES), lambda n, h, i: (n, i, h)),
        out_shape=jax.ShapeDtypeStruct((n, l, D_ATT), F32),
        scratch_shapes=[pltpu.VMEM((l, LANES), BF16), pltpu.VMEM((l, LANES), BF16),
                        pltpu.VMEM((n_blk, LANES), F32)],
        compiler_params=_cparams("parallel", "parallel", "arbitrary"),
        name="moba_prompt",
    )(rel_bias, p_att, p_att, p_att, t_own, t_prev)


def _softplus(x):
    return jnp.maximum(x, 0.0) + jnp.log(1.0 + jnp.exp(-jnp.abs(x)))


def _rwkv_pre_kernel(p_ref, prev_ref, sh0_ref, mu_ref, w0_ref, w2_ref, a0_ref, a2_ref, g2_ref, kkw_ref, kaw_ref,
                     r_o, lw_o, k_o, v_o, kk_o, b_o, g_o):
    i = pl.program_id(1)
    p = p_ref[0]
    prev_row = jnp.where(i == 0, sh0_ref[0], prev_ref[0][7:8])
    row = lax.broadcasted_iota(I32, p.shape, 0)
    p_prev = jnp.where(row == 0, prev_row, pltpu.roll(p, 1, 0))
    xm = p + (p_prev - p) * mu_ref[...]
    o = 3 * D_RWKV
    r = xm[:, :D_RWKV]
    k = xm[:, D_RWKV:2 * D_RWKV]
    v = xm[:, 2 * D_RWKV:o]
    xw = xm[:, o:o + LORA_W]
    xa = xm[:, o + LORA_W:o + LORA_W + LORA_A]
    xg = xm[:, o + LORA_W + LORA_A:]
    w_log = -_softplus(-(w0_ref[...] + _dot(jnp.tanh(xw), w2_ref[...], precision=HI))) - 0.5
    a = _sigmoid(a0_ref[...] + _dot(xa, a2_ref[...], precision=HI))
    g = _dot(_sigmoid(xg), g2_ref[...], precision=HI)
    kk = k * kkw_ref[...]
    same_head = (lax.broadcasted_iota(I32, (D_RWKV, D_RWKV), 0) // HEAD_DIM
                 == lax.broadcasted_iota(I32, (D_RWKV, D_RWKV), 1) // HEAD_DIM)
    ss = _dot(kk * kk, jnp.where(same_head, 1.0, 0.0), precision=HI)
    kk = kk / jnp.maximum(jnp.sqrt(ss), 1e-12)
    r_o[0] = r
    lw_o[0] = -jnp.exp(w_log)
    k_o[0] = k * (1.0 + (a - 1.0) * kaw_ref[...])
    v_o[0] = v
    kk_o[0] = kk
    b_o[0] = kk * a
    g_o[0] = g


def _rwkv_pre_call(p_rw, shift0, mu, w0, w2, a0, a2, g2, k_k, k_a, tc):
    n, l, _ = p_rw.shape
    row = lambda z: z.reshape(1, -1)
    full = lambda z: pl.BlockSpec(z.shape, lambda n, i: (0,) * z.ndim)
    params = [row(mu), row(w0), w2, row(a0), a2, g2, row(k_k), row(k_a)]
    out_spec = pl.BlockSpec((1, tc, D_RWKV), lambda n, i: (n, i, 0))
    return pl.pallas_call(
        _rwkv_pre_kernel,
        grid=(n, l // tc),
        in_specs=[pl.BlockSpec((1, tc, RW_IN), lambda n, i: (n, i, 0)),
                  pl.BlockSpec((1, 8, RW_IN), lambda n, i: (n, jnp.maximum(i * (tc // 8) - 1, 0), 0)),
                  pl.BlockSpec((1, 1, RW_IN), lambda n, i: (n, 0, 0))] + [full(z) for z in params],
        out_specs=[out_spec] * 7,
        out_shape=[jax.ShapeDtypeStruct((n, l, D_RWKV), F32)] * 7,
        compiler_params=_cparams("parallel", "parallel"),
        name="rwkv_pre",
    )(p_rw, p_rw, shift0.reshape(n, 1, RW_IN), *params)


def _chunk_cumsum(x):
    rows = x.shape[0]
    row = lax.broadcasted_iota(I32, x.shape, 0)
    s = 1
    while s < rows:
        x = x + jnp.where(row >= s, pltpu.roll(x, s, 0), 0.0)
        s *= 2
    return x


def _rwkv_scan_kernel(r_ref, lw_ref, k_ref, v_ref, kk_ref, b_ref, g_ref, s0_ref, rk_ref, lg_ref, lb_ref,
                      y_ref, sf_ref, st_ref, yraw_ref, *, chunk, mxu):
    i = pl.program_id(2)
    C = chunk
    tl = r_ref.shape[1]
    n_dbl = max(C.bit_length() - 2, 0)

    @pl.when(i == 0)
    def _():
        st_ref[...] = s0_ref[0, 0]

    lane = lax.broadcasted_iota(I32, (1, LANES), 1)
    head0 = lane < HEAD_DIM
    ti = lax.broadcasted_iota(I32, (C, C), 0)
    si = lax.broadcasted_iota(I32, (C, C), 1)
    strict = si < ti
    incl = si <= ti
    eye_c = jnp.where(si == ti, 1.0, 0.0)
    rj = lax.broadcasted_iota(I32, (LANES, LANES), 0)
    cj = lax.broadcasted_iota(I32, (LANES, LANES), 1)
    same_head = (rj // HEAD_DIM) == (cj // HEAD_DIM)
    eye_l = rj == cj
    mm = lambda a, b: _dot(a.astype(mxu), b.astype(mxu))
    mm_nt = lambda a, b: _dot_nt(a.astype(mxu), b.astype(mxu))
    mm_tn = lambda a, b: _dot_tn(a.astype(mxu), b.astype(mxu))

    def chunk_body(c, carry):
        rows = pl.ds(pl.multiple_of(c * C, C), C)
        lw = lw_ref[0, rows, :]
        r = r_ref[0, rows, :]
        k = k_ref[0, rows, :]
        v = v_ref[0, rows, :]
        kk = kk_ref[0, rows, :]
        b = b_ref[0, rows, :]
        cum = _chunk_cumsum(lw)
        c_end = cum[C - 1:C, :]
        e_neg = jnp.exp(-cum)
        e_end = jnp.exp(c_end - cum)
        al = -kk * jnp.exp(cum - lw)
        bt = b * e_neg
        kt = k * e_neg
        rt = r * jnp.exp(cum)
        bt2 = b * e_end
        kt2 = k * e_end
        lhs = jnp.concatenate([al, rt], axis=0)
        parts = []
        for h in range(HEAD_PAIR):
            mine = head0 if h == 0 else jnp.logical_not(head0)
            lhs_h = jnp.where(mine, lhs, 0.0)
            ab = mm_nt(lhs_h, bt)
            ak = mm_nt(lhs_h, kt)
            m_ab = jnp.where(strict, ab[:C], 0.0)
            m_ak = jnp.where(strict, ak[:C], 0.0)
            b_br = jnp.where(incl, ab[C:], 0.0)
            b_kr = jnp.where(incl, ak[C:], 0.0)
            t_inv = eye_c + m_ab
            pk = m_ab
            for _ in range(n_dbl):
                pk = mm(pk, pk)
                t_inv = t_inv + mm(pk, t_inv)
            aw = mm(t_inv, jnp.concatenate([al, mm(m_ak, v)], axis=1))
            baw = mm(b_br, aw)
            parts.append((aw[:, :LANES], aw[:, LANES:], rt + baw[:, :LANES], baw[:, LANES:] + mm(b_kr, v)))
        a2, w, r2, y0 = [jnp.where(head0, p0, p1) for p0, p1 in zip(*parts)]
        tc_t = jnp.where(same_head, mm_tn(bt2, a2), 0.0)
        gc_t = jnp.where(same_head, mm_tn(jnp.concatenate([bt2, kt2], axis=0),
                                          jnp.concatenate([w, v], axis=0)), 0.0)
        st = st_ref[...]
        yraw_ref[rows, :] = mm(r2, st) + y0
        dec_col = jnp.sum(jnp.where(eye_l, jnp.exp(c_end), 0.0), axis=1, keepdims=True)
        st_ref[...] = dec_col * st + mm(tc_t, st) + gc_t
        return carry

    lax.fori_loop(0, tl // C, chunk_body, 0)
    sf_ref[0, 0] = st_ref[...]

    def head_sum(z):
        s0 = jnp.sum(jnp.where(head0, z, 0.0), axis=1, keepdims=True)
        s1 = jnp.sum(jnp.where(head0, 0.0, z), axis=1, keepdims=True)
        return jnp.where(head0, s0, s1)

    y = yraw_ref[...]
    d = y - head_sum(y) * (1.0 / HEAD_DIM)
    var = head_sum(d * d) * (1.0 / HEAD_DIM)
    yn = d * lax.rsqrt(var + LNX_EPS) * lg_ref[...] + lb_ref[...]
    bonus = head_sum(r_ref[0] * k_ref[0] * rk_ref[...]) * v_ref[0]
    y_ref[0] = (yn + bonus) * g_ref[0]


def _rwkv_scan_call(ops, st0, r_k, lnx_g, lnx_b, tl, chunk):
    n, l, _ = ops[0].shape
    n_hp = H_RWKV // HEAD_PAIR
    mxu = BF16 if chunk >= 16 else F32
    tok = pl.BlockSpec((1, tl, LANES), lambda n, h, i: (n, i, h))
    par = pl.BlockSpec((1, LANES), lambda n, h, i: (0, h))
    st_spec = pl.BlockSpec((1, 1, LANES, LANES), lambda n, h, i: (n, h, 0, 0))
    return pl.pallas_call(
        functools.partial(_rwkv_scan_kernel, chunk=chunk, mxu=mxu),
        grid=(n, n_hp, l // tl),
        in_specs=[tok] * 7 + [st_spec, par, par, par],
        out_specs=[tok, st_spec],
        out_shape=[jax.ShapeDtypeStruct((n, l, D_RWKV), F32),
                   jax.ShapeDtypeStruct((n, n_hp, LANES, LANES), F32)],
        scratch_shapes=[pltpu.VMEM((LANES, LANES), F32), pltpu.VMEM((tl, LANES), F32)],
        compiler_params=_cparams("parallel", "parallel", "arbitrary"),
        name="rwkv_scan",
    )(*ops, st0, r_k.reshape(1, D_RWKV), lnx_g.reshape(1, D_RWKV), lnx_b.reshape(1, D_RWKV))


def _state_to_pairs(s):
    n, h = s.shape[:2]
    st = jnp.swapaxes(s, -1, -2).reshape(n, h // HEAD_PAIR, HEAD_PAIR, HEAD_DIM, HEAD_DIM)
    z = jnp.zeros_like(st[:, :, 0])
    top = jnp.concatenate([st[:, :, 0], z], axis=-1)
    bot = jnp.concatenate([z, st[:, :, 1]], axis=-1)
    return jnp.concatenate([top, bot], axis=-2)


def _pairs_to_state(sp):
    n, n_hp = sp.shape[:2]
    d0 = sp[:, :, :HEAD_DIM, :HEAD_DIM]
    d1 = sp[:, :, HEAD_DIM:, HEAD_DIM:]
    st = jnp.stack([d0, d1], axis=2).reshape(n, n_hp * HEAD_PAIR, HEAD_DIM, HEAD_DIM)
    return jnp.swapaxes(st, -1, -2)


def _rwkv_mix(p_rw, shift0, s0, prm, tc, tl, chunk):
    ops = _rwkv_pre_call(p_rw, shift0, prm["mu"], prm["w0"], prm["w2"], prm["a0"], prm["a2"], prm["g2"],
                         prm["k_k"], prm["k_a"], tc)
    y, sp = _rwkv_scan_call(ops, _state_to_pairs(s0), prm["r_k"], prm["lnx_g"], prm["lnx_b"], tl, chunk)
    return y, _pairs_to_state(sp), p_rw[:, -1]


def _layer_norm(z, g, b):
    mu = jnp.mean(z, axis=-1, keepdims=True)
    d = z - mu
    var = jnp.mean(d * d, axis=-1, keepdims=True)
    return d * lax.rsqrt(var + LN_EPS) * g + b


def _outproj_kernel(ya_ref, yr_ref, x_ref, gt_ref, w_ref, g_ref, b_ref, o_ref):
    o = (_dot(ya_ref[0].astype(BF16), w_ref[:D_ATT, :]) + _dot(yr_ref[0].astype(BF16), w_ref[D_ATT:, :]))
    o_ref[0] = _layer_norm(ALPHA * x_ref[0] + (1.0 + gt_ref[0]) * o, g_ref[...], b_ref[...])


def _outproj_call(y_att, y_rw, x, gt, w_bf, ln_g, ln_b, tm):
    n, l, d = x.shape
    row = pl.BlockSpec((1, d), lambda n, i: (0, 0))
    return pl.pallas_call(
        _outproj_kernel,
        grid=(n, l // tm),
        in_specs=[pl.BlockSpec((1, tm, D_ATT), lambda n, i: (n, i, 0)),
                  pl.BlockSpec((1, tm, D_RWKV), lambda n, i: (n, i, 0)),
                  pl.BlockSpec((1, tm, d), lambda n, i: (n, i, 0)),
                  _mod_spec(gt, tm),
                  pl.BlockSpec(w_bf.shape, lambda n, i: (0, 0)), row, row],
        out_specs=pl.BlockSpec((1, tm, d), lambda n, i: (n, i, 0)),
        out_shape=jax.ShapeDtypeStruct((n, l, d), F32),
        compiler_params=_cparams("parallel", "parallel"),
        name="out_proj_ln",
    )(y_att, y_rw, x, gt, w_bf, ln_g.reshape(1, d), ln_b.reshape(1, d))


def _ffn_kernel(x_ref, sc_ref, sh_ref, gt_ref, wr_ref, wg_ref, wu_ref, wd_ref, g_ref, b_ref, o_ref,
                u_ref, acc_ref, gate_ref, *, n_experts):
    e = pl.program_id(2)
    f = pl.program_id(3)
    last = (e == pl.num_programs(2) - 1) & (f == pl.num_programs(3) - 1)

    @pl.when((e == 0) & (f == 0))
    def _():
        u = x_ref[0] * (1.0 + sc_ref[0]) + sh_ref[0]
        u_ref[...] = u.astype(BF16)
        acc_ref[...] = jnp.zeros_like(acc_ref)
        if n_experts > 1:
            logits = _dot(u, wr_ref[...], precision=HI)
            lane = lax.broadcasted_iota(I32, logits.shape, 1)
            logits = jnp.where(lane < n_experts, logits, NEG_INF)
            v1 = jnp.max(logits, axis=1, keepdims=True)
            i1 = jnp.min(jnp.where(logits == v1, lane, LANES), axis=1, keepdims=True)
            rest = jnp.where(lane == i1, NEG_INF, logits)
            v2 = jnp.max(rest, axis=1, keepdims=True)
            i2 = jnp.min(jnp.where(rest == v2, lane, LANES), axis=1, keepdims=True)
            w2 = 1.0 / (1.0 + jnp.exp(v1 - v2))
            gate_ref[...] = jnp.where(lane == i1, 1.0 - w2, 0.0) + jnp.where(lane == i2, w2, 0.0)

    u = u_ref[...]
    hg = _dot(u, wg_ref[0])
    hu = _dot(u, wu_ref[0])
    h = hg * _sigmoid(hg) * hu
    if n_experts > 1:
        lane = lax.broadcasted_iota(I32, gate_ref.shape, 1)
        h = h * jnp.sum(jnp.where(lane == e, gate_ref[...], 0.0), axis=1, keepdims=True)
    acc_ref[...] += _dot(h.astype(BF16), wd_ref[0])

    @pl.when(last)
    def _():
        o_ref[0] = _layer_norm(ALPHA * x_ref[0] + (1.0 + gt_ref[0]) * acc_ref[...], g_ref[...], b_ref[...])


def _ffn_call(x, sc, sh, gt, w_router, wg_bf, wu_bf, wd_bf, ln_g, ln_b, tm, tf):
    n, l, d = x.shape
    n_e, _, d_ff = wg_bf.shape
    row = pl.BlockSpec((1, d), lambda n, i, e, f: (0, 0))
    mod = lambda m: (pl.BlockSpec((1, 1, d), lambda n, i, e, f: (n, 0, 0)) if m.shape[1] == 1
                     else pl.BlockSpec((1, tm, d), lambda n, i, e, f: (n, i, 0)))
    return pl.pallas_call(
        functools.partial(_ffn_kernel, n_experts=n_e),
        grid=(n, l // tm, n_e, d_ff // tf),
        in_specs=[pl.BlockSpec((1, tm, d), lambda n, i, e, f: (n, i, 0)),
                  mod(sc), mod(sh), mod(gt),
                  pl.BlockSpec(w_router.shape, lambda n, i, e, f: (0, 0)),
                  pl.BlockSpec((1, d, tf), lambda n, i, e, f: (e, 0, f)),
                  pl.BlockSpec((1, d, tf), lambda n, i, e, f: (e, 0, f)),
                  pl.BlockSpec((1, tf, d), lambda n, i, e, f: (e, f, 0)),
                  row, row],
        out_specs=pl.BlockSpec((1, tm, d), lambda n, i, e, f: (n, i, 0)),
        out_shape=jax.ShapeDtypeStruct((n, l, d), F32),
        scratch_shapes=[pltpu.VMEM((tm, d), BF16), pltpu.VMEM((tm, d), F32), pltpu.VMEM((tm, LANES), F32)],
        compiler_params=_cparams("parallel", "parallel", "arbitrary", "arbitrary"),
        name="ffn_ln" if n_e == 1 else "moe_ln",
    )(x, sc, sh, gt, w_router, wg_bf, wu_bf, wd_bf, ln_g.reshape(1, d), ln_b.reshape(1, d))


PAGES_PER_BLOCK = MOBA_BLOCK // PAGE_SIZE
KMEAN_PAGES = 8


def _kmean_kernel(pt_ref, *refs):
    o_ref = refs[-1]
    for blk in range(KMEAN_PAGES // PAGES_PER_BLOCK):
        tot = sum(jnp.sum(refs[blk * PAGES_PER_BLOCK + r][...], axis=1) for r in range(PAGES_PER_BLOCK))
        o_ref[0, blk] = tot * (1.0 / MOBA_BLOCK)


def _kmean_call(cache_k, layer, page_table, n_full):
    n_seq = page_table.shape[0]
    _, _, n_h, ps, hd = cache_k.shape
    page = lambda r: pl.BlockSpec((None, None, n_h, ps, hd),
                                  lambda b, g, pt: (layer, pt[b, g * KMEAN_PAGES + r], 0, 0, 0))
    blocks_per_step = KMEAN_PAGES // PAGES_PER_BLOCK
    return pl.pallas_call(
        _kmean_kernel,
        grid_spec=pltpu.PrefetchScalarGridSpec(
            num_scalar_prefetch=1,
            grid=(n_seq, n_full // blocks_per_step),
            in_specs=[page(r) for r in range(KMEAN_PAGES)],
            out_specs=pl.BlockSpec((1, blocks_per_step, n_h, hd), lambda b, g, pt: (b, g, 0, 0))),
        out_shape=jax.ShapeDtypeStruct((n_seq, n_full, n_h, hd), F32),
        compiler_params=_cparams("parallel", "arbitrary"),
        name="moba_kmean",
    )(page_table, *([cache_k] * KMEAN_PAGES))


def _top3_kernel(q_ref, km_ref, o_ref):
    n_h = q_ref.shape[1]
    for h in range(n_h):
        g = _dot_nt(q_ref[0, h], km_ref[0, h], precision=HI)
        lane = lax.broadcasted_iota(I32, g.shape, 1)
        out_lane = lax.broadcasted_iota(I32, (g.shape[0], LANES), 1)
        out = jnp.zeros((g.shape[0], LANES), I32)
        for s in range(MOBA_TOPK):
            mx = jnp.max(g, axis=1, keepdims=True)
            idx = jnp.min(jnp.where(g == mx, lane, g.shape[1]), axis=1, keepdims=True)
            out = jnp.where(out_lane == s, idx, out)
            g = jnp.where(lane == idx, -jnp.inf, g)
        o_ref[0, h] = out


def _top3_call(q, k_mean_h):
    n_seq, n_h, t, hd = q.shape
    n_full = k_mean_h.shape[2]
    return pl.pallas_call(
        _top3_kernel,
        grid=(n_seq,),
        in_specs=[pl.BlockSpec((1, n_h, t, hd), lambda b: (b, 0, 0, 0)),
                  pl.BlockSpec((1, n_h, n_full, hd), lambda b: (b, 0, 0, 0))],
        out_specs=pl.BlockSpec((1, n_h, t, LANES), lambda b: (b, 0, 0, 0)),
        out_shape=jax.ShapeDtypeStruct((n_seq, n_h, t, LANES), I32),
        compiler_params=_cparams("parallel"),
        name="moba_top3",
    )(q, k_mean_h)


def _moba_sample_kernel(pt_ref, sel_ref, rb_ref, q_ref, kn_ref, vn_ref, town_ref, tprev_ref, *refs,
                        n_h, n_t, n_full):
    n_pg = MOBA_TOPK * PAGES_PER_BLOCK
    k_refs = refs[:n_pg]
    v_refs = refs[n_pg:2 * n_pg]
    o_ref = refs[2 * n_pg]
    b = pl.program_id(0)
    h = pl.program_id(1)
    t = pl.program_id(2)
    q = (q_ref[0, 0] * ATT_SCALE).astype(BF16)
    b_far = rb_ref[N_BUCKETS - 1, h]
    base = ((b * n_h + h) * n_t + t) * MOBA_TOPK
    scores = [_dot_nt(q, kn_ref[0, 0].astype(BF16)) + town_ref[0]]
    for s in range(MOBA_TOPK):
        last = sel_ref[base + s] == n_full - 1
        for r in range(PAGES_PER_BLOCK):
            sc = _dot_nt(q, k_refs[s * PAGES_PER_BLOCK + r][...].astype(BF16))
            bias = jnp.where(last, tprev_ref[0, :, r * PAGE_SIZE:(r + 1) * PAGE_SIZE], b_far)
            scores.append(sc + bias)
    m = functools.reduce(jnp.maximum, [jnp.max(sc, axis=1, keepdims=True) for sc in scores])
    probs = [jnp.exp(sc - m) for sc in scores]
    denom = functools.reduce(lambda a, c: a + c, [jnp.sum(p, axis=1, keepdims=True) for p in probs])
    vals = [vn_ref[0, 0]] + [v_refs[j][...] for j in range(n_pg)]
    acc = functools.reduce(lambda a, c: a + c,
                           [_dot(p.astype(BF16), v.astype(BF16)) for p, v in zip(probs, vals)])

    @pl.when(t == 0)
    def _():
        o_ref[...] = jnp.zeros_like(o_ref)

    mine = lax.broadcasted_iota(I32, acc.shape, 0) == t
    o_ref[0, 0] = jnp.where(mine, acc / denom, o_ref[0, 0])


def _moba_sample_call(q, k_new, v_new, cache_k, cache_v, layer, page_table, sel_flat, rel_bias, ts_own, ts_prev, n_full):
    n_seq, n_h, n_t, hd = q.shape
    ps = cache_k.shape[3]
    tok = pl.BlockSpec((1, 1, n_t, hd), lambda b, h, t, pt, sel: (b, h, 0, 0))

    def page(s, r):
        def imap(b, h, t, pt, sel):
            blk = sel[((b * n_h + h) * n_t + t) * MOBA_TOPK + s]
            return (layer, pt[b, blk * PAGES_PER_BLOCK + r], h, 0, 0)
        return pl.BlockSpec((None, None, None, ps, hd), imap)

    pages = [page(s, r) for s in range(MOBA_TOPK) for r in range(PAGES_PER_BLOCK)]
    return pl.pallas_call(
        functools.partial(_moba_sample_kernel, n_h=n_h, n_t=n_t, n_full=n_full),
        grid_spec=pltpu.PrefetchScalarGridSpec(
            num_scalar_prefetch=2,
            grid=(n_seq, n_h, n_t),
            in_specs=[pl.BlockSpec(memory_space=pltpu.SMEM), tok, tok, tok,
                      pl.BlockSpec((1, n_t, n_t), lambda b, h, t, pt, sel: (h, 0, 0)),
                      pl.BlockSpec((1, n_t, MOBA_BLOCK), lambda b, h, t, pt, sel: (h, 0, 0))] + pages + pages,
            out_specs=tok),
        out_shape=jax.ShapeDtypeStruct((n_seq, n_h, n_t, hd), F32),
        compiler_params=_cparams("parallel", "parallel", "arbitrary"),
        name="moba_sample",
    )(page_table, sel_flat, rel_bias, q, k_new, v_new, ts_own, ts_prev,
      *([cache_k] * len(pages)), *([cache_v] * len(pages)))


def _moba_sample(p_att, cache_k, cache_v, layer, page_table, rel_bias, ts_own, ts_prev):
    n_seq, n_t, _ = p_att.shape
    n_full = page_table.shape[1] // PAGES_PER_BLOCK
    qkv = p_att.reshape(n_seq, n_t, 3, H_ATT, HEAD_DIM).transpose(2, 0, 3, 1, 4)
    q, k_new, v_new = qkv[0], qkv[1], qkv[2]
    k_mean = _kmean_call(cache_k, layer, page_table, n_full)
    picks = _top3_call(q, k_mean.transpose(0, 2, 1, 3))
    sel_flat = picks[..., :MOBA_TOPK].reshape(-1)
    out = _moba_sample_call(q, k_new, v_new, cache_k, cache_v, layer, page_table, sel_flat, rel_bias,
                            ts_own, ts_prev, n_full)
    return out.transpose(0, 2, 1, 3).reshape(n_seq, n_t, D_ATT), k_new, v_new


PROMPT_TM = 512
PROMPT_RW_TC = 256
PROMPT_RW_TL = 512
PROMPT_RW_CHUNK = 32
FF_TILE = 256


def _to_pages(z):
    b, s, _ = z.shape
    return z.reshape(b, s // PAGE_SIZE, PAGE_SIZE, H_ATT, HEAD_DIM).transpose(0, 1, 3, 2, 4)


def kernel(x_prompt, x_sample, cache_k, cache_v, state_wkv, state_shift, page_table, c_prompt, c_sample, rel_bias, w_ada, b_ada, w_in, rw_mu, rw_w0, rw_w2, rw_a0, rw_a2, rw_g2, rw_k_k, rw_k_a, rw_r_k, rw_lnx_g, rw_lnx_b, w_out, ln1_g, ln1_b, ln2_g, ln2_b, ffn_w_gate, ffn_w_up, ffn_w_down, moe_w_router, moe_w_gate, moe_w_up, moe_w_down):
    n_p, seq, d = x_prompt.shape
    n_s, t_s, _ = x_sample.shape
    depth = w_in.shape[0]
    assert page_table.shape[1] % PAGES_PER_BLOCK == 0 and seq % MOBA_BLOCK == 0
    rows_s = n_s * t_s

    n_c = n_p + n_s
    pad_c = -n_c % 8
    c_all = jnp.pad(jnp.concatenate([c_prompt, c_sample], axis=0), ((0, pad_c), (0, 0)))
    mods = _ada_call(c_all, w_ada, b_ada)

    t_own, t_prev, ts_own, ts_prev = _bias_call(
        rel_bias, [(MOBA_BLOCK, MOBA_BLOCK), (MOBA_BLOCK, MOBA_BLOCK), (t_s, t_s), (t_s, MOBA_BLOCK)],
        [0, MOBA_BLOCK, 0, MOBA_BLOCK])

    xp = x_prompt
    xs = x_sample.reshape(1, rows_s, d)
    zero_shift = jnp.zeros((n_p, RW_IN), F32)
    zero_state = jnp.zeros((n_p, H_RWKV, HEAD_DIM, HEAD_DIM), F32)
    outs = {k: [] for k in ("kp", "vp", "ks", "vs", "wp", "ws", "hp", "hs")}
    for l in range(depth):
        mod_p = [m[:, None, :] for m in jnp.split(mods[l, :n_p], 6, axis=-1)]
        mod_s = [jnp.repeat(m, t_s, axis=0)[None] for m in jnp.split(mods[l, n_p:n_c], 6, axis=-1)]
        prm = dict(mu=rw_mu[l], w0=rw_w0[l], w2=rw_w2[l], a0=rw_a0[l], a2=rw_a2[l], g2=rw_g2[l], k_k=rw_k_k[l],
                   k_a=rw_k_a[l], r_k=rw_r_k[l], lnx_g=rw_lnx_g[l], lnx_b=rw_lnx_b[l])
        w_in_bf = w_in[l].astype(BF16)
        w_out_bf = w_out[l].astype(BF16)
        if l % 2 == 0:
            w_router = jnp.zeros((d, LANES), F32)
            wg, wu, wd = (w[l // 2][None].astype(BF16) for w in (ffn_w_gate, ffn_w_up, ffn_w_down))
        else:
            w_router = jnp.pad(moe_w_router[l // 2], ((0, 0), (0, LANES - N_EXPERTS)))
            wg, wu, wd = (w[l // 2].astype(BF16) for w in (moe_w_gate, moe_w_up, moe_w_down))

        pa, pr = _inproj_call(xp, mod_p[1], mod_p[0], w_in_bf, PROMPT_TM)
        y_att = _moba_prompt_call(pa, rel_bias, t_own, t_prev)
        y_rw, s_new, sh_new = _rwkv_mix(pr, zero_shift, zero_state, prm, PROMPT_RW_TC, PROMPT_RW_TL, PROMPT_RW_CHUNK)
        xp = _outproj_call(y_att, y_rw, xp, mod_p[2], w_out_bf, ln1_g[l], ln1_b[l], PROMPT_TM)
        xp = _ffn_call(xp, mod_p[4], mod_p[3], mod_p[5], w_router, wg, wu, wd, ln2_g[l], ln2_b[l], PROMPT_TM, FF_TILE)
        outs["kp"].append(_to_pages(pa[..., D_ATT:2 * D_ATT]))
        outs["vp"].append(_to_pages(pa[..., 2 * D_ATT:]))
        outs["wp"].append(s_new)
        outs["hp"].append(sh_new)

        pa, pr = _inproj_call(xs, mod_s[1], mod_s[0], w_in_bf, rows_s)
        y_att, k_new, v_new = _moba_sample(pa.reshape(n_s, t_s, ATT_IN), cache_k, cache_v, l, page_table, rel_bias,
                                           ts_own, ts_prev)
        y_rw, s_new, sh_new = _rwkv_mix(pr.reshape(n_s, t_s, RW_IN), state_shift[l], state_wkv[l], prm, t_s, t_s, t_s)
        xs = _outproj_call(y_att.reshape(1, rows_s, D_ATT), y_rw.reshape(1, rows_s, D_RWKV), xs, mod_s[2], w_out_bf,
                           ln1_g[l], ln1_b[l], rows_s)
        xs = _ffn_call(xs, mod_s[4], mod_s[3], mod_s[5], w_router, wg, wu, wd, ln2_g[l], ln2_b[l], rows_s, FF_TILE)
        outs["ks"].append(k_new)
        outs["vs"].append(v_new)
        outs["ws"].append(s_new)
        outs["hs"].append(sh_new)

    st = lambda k: jnp.stack(outs[k])
    return (xp, xs.reshape(n_s, t_s, d), st("kp"), st("vp"), st("ks"), st("vs"), st("wp"), st("ws"), st("hp"), st("hs"))
```

```python
import functools
import math

import jax
import jax.numpy as jnp
from jax import lax
from jax.experimental import pallas as pl
from jax.experimental.pallas import tpu as pltpu

F32 = jnp.float32
BF16 = jnp.bfloat16
I32 = jnp.int32
HI = lax.Precision.HIGHEST

D_MODEL = 1024
HEAD_DIM = 64
H_ATT = 8
H_RWKV = 8
D_ATT = H_ATT * HEAD_DIM
D_RWKV = H_RWKV * HEAD_DIM
ATT_IN = 3 * D_ATT
LORA_W = 64
LORA_A = 64
LORA_G = 128
RW_IN = 3 * D_RWKV + LORA_W + LORA_A + LORA_G
MOBA_BLOCK = 256
MOBA_TOPK = 3
PAGE_SIZE = 128
N_BUCKETS = 32
MAX_EXACT = N_BUCKETS // 2
MAX_DISTANCE = 128
D_FF = 2816
N_EXPERTS = 8
DEPTH = 2
ALPHA = (2 * DEPTH) ** 0.25
LN_EPS = 1e-5
LNX_EPS = 64e-5
NEG_INF = -1e30
ATT_SCALE = HEAD_DIM ** -0.5

LANES = 128
HEAD_PAIR = LANES // HEAD_DIM
BF16_ROWS = 16
VMEM_LIMIT = 56 * 1024 * 1024


def _cparams(*sem):
    return pltpu.CompilerParams(dimension_semantics=sem, vmem_limit_bytes=VMEM_LIMIT)


def _sigmoid(x):
    return 1.0 / (1.0 + jnp.exp(-x))


def _dot_nt(a, b, precision=None):
    return lax.dot_general(a, b, (((1,), (1,)), ((), ())), precision=precision, preferred_element_type=F32)


def _dot_tn(a, b, precision=None):
    return lax.dot_general(a, b, (((0,), (0,)), ((), ())), precision=precision, preferred_element_type=F32)


def _dot(a, b, precision=None):
    return jnp.dot(a, b, precision=precision, preferred_element_type=F32)


def _ada_kernel(c_ref, w_ref, b_ref, o_ref):
    c = c_ref[...]
    o_ref[0] = _dot(c * _sigmoid(c), w_ref[0], precision=HI) + b_ref[0]


def _ada_call(c_all, w_ada, b_ada):
    n = c_all.shape[0]
    depth, d, cols = w_ada.shape
    tn = 768
    return pl.pallas_call(
        _ada_kernel,
        grid=(depth, cols // tn),
        in_specs=[pl.BlockSpec((n, d), lambda l, j: (0, 0)),
                  pl.BlockSpec((1, d, tn), lambda l, j: (l, 0, j)),
                  pl.BlockSpec((1, 1, tn), lambda l, j: (l, 0, j))],
        out_specs=pl.BlockSpec((1, n, tn), lambda l, j: (l, 0, j)),
        out_shape=jax.ShapeDtypeStruct((depth, n, cols), F32),
        compiler_params=_cparams("parallel", "parallel"),
        name="ada_mod",
    )(c_all, w_ada, b_ada.reshape(depth, 1, cols))


def _bucket_of(dist):
    d = jnp.maximum(dist, 0)
    log_ratio = jnp.log(jnp.maximum(d, 1).astype(F32) / MAX_EXACT) / math.log(MAX_DISTANCE / MAX_EXACT)
    large = jnp.minimum(MAX_EXACT + (log_ratio * (N_BUCKETS - MAX_EXACT)).astype(I32), N_BUCKETS - 1)
    return jnp.where(d < MAX_EXACT, d, large)


def _bias_kernel(rb_ref, *o_refs, offsets, key_major):
    for o_ref, off, km in zip(o_refs, offsets, key_major):
        n_h, rows, cols = o_ref.shape
        dist = (lax.broadcasted_iota(I32, (rows, cols), 0) - lax.broadcasted_iota(I32, (rows, cols), 1))
        dist = (-dist if km else dist) + off
        bucket = _bucket_of(dist)
        for h in range(n_h):
            acc = jnp.zeros((rows, cols), F32)
            for b in range(N_BUCKETS):
                acc = jnp.where(bucket == b, rb_ref[b, h], acc)
            o_ref[h] = jnp.where(dist >= 0, acc, NEG_INF)


def _bias_call(rel_bias, shapes, offsets, key_major=None):
    n_h = rel_bias.shape[1]
    key_major = tuple(key_major) if key_major is not None else (False,) * len(shapes)
    return pl.pallas_call(
        functools.partial(_bias_kernel, offsets=tuple(offsets), key_major=key_major),
        in_specs=[pl.BlockSpec(memory_space=pltpu.SMEM)],
        out_specs=[pl.BlockSpec(memory_space=pltpu.VMEM) for _ in shapes],
        out_shape=[jax.ShapeDtypeStruct((n_h,) + s, F32) for s in shapes],
        name="rel_bias_tables",
    )(rel_bias)


def _inproj_kernel(x_ref, sc_ref, sh_ref, w_ref, oa_ref, or_ref, *, chunk):
    u = (x_ref[0] * (1.0 + sc_ref[0]) + sh_ref[0]).astype(BF16)
    att = oa_ref.shape[-1]
    for c0 in range(0, w_ref.shape[1], chunk):
        res = _dot(u, w_ref[:, c0:c0 + chunk])
        if c0 < att:
            oa_ref[0, :, c0:c0 + chunk] = res
        else:
            or_ref[0, :, c0 - att:c0 - att + chunk] = res


def _mod_spec(mod, tm):
    if mod.shape[1] == 1:
        return pl.BlockSpec((1, 1, mod.shape[2]), lambda n, i: (n, 0, 0))
    return pl.BlockSpec((1, tm, mod.shape[2]), lambda n, i: (n, i, 0))


def _inproj_call(x, sc, sh, w_bf, tm):
    n, l, d = x.shape
    cols = w_bf.shape[1]
    return pl.pallas_call(
        functools.partial(_inproj_kernel, chunk=256),
        grid=(n, l // tm),
        in_specs=[pl.BlockSpec((1, tm, d), lambda n, i: (n, i, 0)),
                  _mod_spec(sc, tm), _mod_spec(sh, tm),
                  pl.BlockSpec((d, cols), lambda n, i: (0, 0))],
        out_specs=[pl.BlockSpec((1, tm, ATT_IN), lambda n, i: (n, i, 0)),
                   pl.BlockSpec((1, tm, RW_IN), lambda n, i: (n, i, 0))],
        out_shape=[jax.ShapeDtypeStruct((n, l, ATT_IN), F32),
                   jax.ShapeDtypeStruct((n, l, RW_IN), F32)],
        compiler_params=_cparams("parallel", "parallel"),
        name="in_proj",
    )(x, sc, sh, w_bf)


def _moba_prompt_kernel(rb_ref, q_ref, k_ref, v_ref, town_ref, tprev_ref, o_ref, kb_ref, vb_ref, km_ref,
                        *, n_blk):
    hp = pl.program_id(1)
    i = pl.program_id(2)
    blk = MOBA_BLOCK

    @pl.when(i == 0)
    def _():
        k = k_ref[0]
        kb_ref[...] = k.astype(BF16)
        km_ref[...] = jnp.sum(k.reshape(n_blk, blk, LANES), axis=1) * (1.0 / blk)
        for c in range(n_blk):
            vb_ref[:, c * blk:(c + 1) * blk] = v_ref[0, c * blk:(c + 1) * blk, :].T.astype(BF16)

    qt = q_ref[0].T
    dim_row = lax.broadcasted_iota(I32, qt.shape, 0)
    own0 = pl.multiple_of(i * blk, blk)
    heads = range(HEAD_PAIR)
    vrows = [slice(h2 * HEAD_DIM, (h2 + 1) * HEAD_DIM) for h2 in heads]
    bidx = lax.broadcasted_iota(I32, (n_blk, blk), 0)
    past = bidx < i
    qt_h = [jnp.where((dim_row // HEAD_DIM) == h2, qt, 0.0) for h2 in heads]
    qb = [(z * ATT_SCALE).astype(BF16) for z in qt_h]
    sel = []
    for h2 in heads:
        g = jnp.where(past, _dot(km_ref[...], qt_h[h2], precision=HI), NEG_INF)
        rank = jnp.zeros(g.shape, I32)
        for jp in range(n_blk):
            gj = g[jp:jp + 1, :]
            rank = rank + ((gj > g) | ((gj == g) & (jp < bidx))).astype(I32)
        sel.append(jnp.where((rank < MOBA_TOPK) & past, 1.0, 0.0))
    b_far = [rb_ref[N_BUCKETS - 1, hp * HEAD_PAIR + h2] for h2 in heads]

    def attend(r0, biases, masks, carry):
        kb = kb_ref[pl.ds(r0, blk), :]
        s = [_dot(kb, qb[h2]) + biases[h2] for h2 in heads]
        if masks is not None:
            s = [jnp.where(masks[h2], s[h2], NEG_INF) for h2 in heads]
        out = []
        for h2 in heads:
            s_h = s[h2]
            if carry is None:
                m_new = jnp.max(s_h, axis=0, keepdims=True)
                p = jnp.exp(s_h - m_new)
                l = jnp.sum(p, axis=0, keepdims=True)
                acc = _dot(vb_ref[vrows[h2], pl.ds(r0, blk)], p.astype(BF16))
            else:
                m, l, acc = carry[h2]
                m_new = jnp.maximum(m, jnp.max(s_h, axis=0, keepdims=True))
                a = jnp.exp(m - m_new)
                p = jnp.exp(s_h - m_new)
                l = a * l + jnp.sum(p, axis=0, keepdims=True)
                acc = a * acc + _dot(vb_ref[vrows[h2], pl.ds(r0, blk)], p.astype(BF16))
            out.append((m_new, l, acc))
        return tuple(out)

    def step(j, carry, biases):
        masks = [jnp.sum(jnp.where(bidx == j, sel[h2], 0.0), axis=0, keepdims=True) > 0.5 for h2 in heads]
        return attend(pl.multiple_of(j * blk, blk), biases, masks, carry)

    carry = attend(own0, [town_ref[h2] for h2 in heads], None, None)
    carry = lax.fori_loop(0, jnp.maximum(i - 1, 0), lambda j, c: step(j, c, b_far), carry)
    carry = lax.fori_loop(jnp.maximum(i - 1, 0), i, lambda j, c: step(j, c, [tprev_ref[h2] for h2 in heads]), carry)
    o_ref[0] = jnp.concatenate([acc / l for _, l, acc in carry], axis=0).T


def _moba_prompt_call(p_att, rel_bias, t_own, t_prev):
    n, l, _ = p_att.shape
    n_blk = l // MOBA_BLOCK
    n_hp = H_ATT // HEAD_PAIR
    tbl = pl.BlockSpec((HEAD_PAIR, MOBA_BLOCK, MOBA_BLOCK), lambda n, h, i: (h, 0, 0))
    return pl.pallas_call(
        functools.partial(_moba_prompt_kernel, n_blk=n_blk),
        grid=(n, n_hp, n_blk),
        in_specs=[pl.BlockSpec(memory_space=pltpu.SMEM),
                  pl.BlockSpec((1, MOBA_BLOCK, LANES), lambda n, h, i: (n, i, h)),
                  pl.BlockSpec((1, l, LANES), lambda n, h, i: (n, 0, n_hp + h)),
                  pl.BlockSpec((1, l, LANES), lambda n, h, i: (n, 0, 2 * n_hp + h)),
                  tbl, tbl],
        out_specs=pl.BlockSpec((1, MOBA_BLOCK, LANES), lambda n, h, i: (n, i, h)),
        out_shape=jax.ShapeDtypeStruct((n, l, D_ATT), F32),
        scratch_shapes=[pltpu.VMEM((l, LANES), BF16), pltpu.VMEM((LANES, l), BF16),
                        pltpu.VMEM((n_blk, LANES), F32)],
        compiler_params=_cparams("parallel", "parallel", "arbitrary"),
        name="moba_prompt",
    )(rel_bias, p_att, p_att, p_att, t_own, t_prev)


def _softplus(x):
    return jnp.maximum(x, 0.0) + jnp.log(1.0 + jnp.exp(-jnp.abs(x)))


def _rwkv_pre_kernel(p_ref, prev_ref, sh0_ref, mu_ref, w0_ref, w2_ref, a0_ref, a2_ref, g2_ref, kkw_ref, kaw_ref,
                     r_o, lw_o, k_o, v_o, kk_o, b_o, g_o):
    i = pl.program_id(1)
    p = p_ref[0]
    prev_row = jnp.where(i == 0, sh0_ref[0], prev_ref[0][7:8])
    row = lax.broadcasted_iota(I32, p.shape, 0)
    p_prev = jnp.where(row == 0, prev_row, pltpu.roll(p, 1, 0))
    xm = p + (p_prev - p) * mu_ref[...]
    o = 3 * D_RWKV
    r = xm[:, :D_RWKV]
    k = xm[:, D_RWKV:2 * D_RWKV]
    v = xm[:, 2 * D_RWKV:o]
    xw = xm[:, o:o + LORA_W]
    xa = xm[:, o + LORA_W:o + LORA_W + LORA_A]
    xg = xm[:, o + LORA_W + LORA_A:]
    w_log = -_softplus(-(w0_ref[...] + _dot(jnp.tanh(xw), w2_ref[...], precision=HI))) - 0.5
    a = _sigmoid(a0_ref[...] + _dot(xa, a2_ref[...], precision=HI))
    g = _dot(_sigmoid(xg), g2_ref[...], precision=HI)
    kk = k * kkw_ref[...]
    same_head = (lax.broadcasted_iota(I32, (D_RWKV, D_RWKV), 0) // HEAD_DIM
                 == lax.broadcasted_iota(I32, (D_RWKV, D_RWKV), 1) // HEAD_DIM)
    ss = _dot(kk * kk, jnp.where(same_head, 1.0, 0.0), precision=HI)
    kk = kk / jnp.maximum(jnp.sqrt(ss), 1e-12)
    r_o[0] = r
    lw_o[0] = -jnp.exp(w_log)
    k_o[0] = k * (1.0 + (a - 1.0) * kaw_ref[...])
    v_o[0] = v
    kk_o[0] = kk
    b_o[0] = kk * a
    g_o[0] = g


def _rwkv_pre_call(p_rw, shift0, mu, w0, w2, a0, a2, g2, k_k, k_a, tc):
    n, l, _ = p_rw.shape
    row = lambda z: z.reshape(1, -1)
    full = lambda z: pl.BlockSpec(z.shape, lambda n, i: (0,) * z.ndim)
    params = [row(mu), row(w0), w2, row(a0), a2, g2, row(k_k), row(k_a)]
    out_spec = pl.BlockSpec((1, tc, D_RWKV), lambda n, i: (n, i, 0))
    return pl.pallas_call(
        _rwkv_pre_kernel,
        grid=(n, l // tc),
        in_specs=[pl.BlockSpec((1, tc, RW_IN), lambda n, i: (n, i, 0)),
                  pl.BlockSpec((1, 8, RW_IN), lambda n, i: (n, jnp.maximum(i * (tc // 8) - 1, 0), 0)),
                  pl.BlockSpec((1, 1, RW_IN), lambda n, i: (n, 0, 0))] + [full(z) for z in params],
        out_specs=[out_spec] * 7,
        out_shape=[jax.ShapeDtypeStruct((n, l, D_RWKV), F32)] * 7,
        compiler_params=_cparams("parallel", "parallel"),
        name="rwkv_pre",
    )(p_rw, p_rw, shift0.reshape(n, 1, RW_IN), *params)


def _chunk_cumsum(x):
    rows = x.shape[0]
    row = lax.broadcasted_iota(I32, x.shape, 0)
    s = 1
    while s < rows:
        x = x + jnp.where(row >= s, pltpu.roll(x, s, 0), 0.0)
        s *= 2
    return x


def _rwkv_scan_kernel(r_ref, lw_ref, k_ref, v_ref, kk_ref, b_ref, g_ref, s0_ref, rk_ref, lg_ref, lb_ref,
                      y_ref, sf_ref, st_ref, yraw_ref, *, chunk, unroll, mxu):
    i = pl.program_id(2)
    C = chunk
    tl = r_ref.shape[1]
    n_dbl = max(C.bit_length() - 2, 0)

    @pl.when(i == 0)
    def _():
        st_ref[...] = s0_ref[0, 0]

    lane = lax.broadcasted_iota(I32, (1, LANES), 1)
    head0 = lane < HEAD_DIM
    ti = lax.broadcasted_iota(I32, (C, C), 0)
    si = lax.broadcasted_iota(I32, (C, C), 1)
    strict = si < ti
    incl = si <= ti
    eye_c = jnp.where(si == ti, 1.0, 0.0)
    rj = lax.broadcasted_iota(I32, (LANES, LANES), 0)
    cj = lax.broadcasted_iota(I32, (LANES, LANES), 1)
    same_head = (rj // HEAD_DIM) == (cj // HEAD_DIM)
    eye_l = rj == cj
    mm = lambda a, b: _dot(a.astype(mxu), b.astype(mxu))
    mm_nt = lambda a, b: _dot_nt(a.astype(mxu), b.astype(mxu))
    mm_tn = lambda a, b: _dot_tn(a.astype(mxu), b.astype(mxu))

    def chunk_operands(rows):
        lw = lw_ref[0, rows, :]
        r = r_ref[0, rows, :]
        k = k_ref[0, rows, :]
        v = v_ref[0, rows, :]
        kk = kk_ref[0, rows, :]
        b = b_ref[0, rows, :]
        cum = _chunk_cumsum(lw)
        c_end = cum[C - 1:C, :]
        e_neg = jnp.exp(-cum)
        e_end = jnp.exp(c_end - cum)
        al = -kk * jnp.exp(cum - lw)
        rt = r * jnp.exp(cum)
        return dict(v=v, al=al, rt=rt, bt=b * e_neg, kt=k * e_neg, bt2=b * e_end, kt2=k * e_end,
                    lhs=jnp.concatenate([al, rt], axis=0), c_end=c_end)

    def group_terms(rows_list):
        ops = [chunk_operands(rw) for rw in rows_list]
        ch = [(u, h) for u in range(len(ops)) for h in range(HEAD_PAIR)]
        mine = [head0, jnp.logical_not(head0)]
        lhs = [jnp.where(mine[h], ops[u]["lhs"], 0.0) for u, h in ch]
        ab = [mm_nt(lhs[n], ops[u]["bt"]) for n, (u, h) in enumerate(ch)]
        ak = [mm_nt(lhs[n], ops[u]["kt"]) for n, (u, h) in enumerate(ch)]
        m_ab = [jnp.where(strict, z[:C], 0.0) for z in ab]
        m_ak = [jnp.where(strict, z[:C], 0.0) for z in ak]
        b_br = [jnp.where(incl, z[C:], 0.0) for z in ab]
        b_kr = [jnp.where(incl, z[C:], 0.0) for z in ak]
        t_inv = [eye_c + z for z in m_ab]
        pk = m_ab
        for _ in range(n_dbl):
            pk = [mm(z, z) for z in pk]
            t_inv = [t + mm(z, t) for z, t in zip(pk, t_inv)]
        mv = [mm(m_ak[n], ops[u]["v"]) for n, (u, h) in enumerate(ch)]
        aw = [mm(t_inv[n], jnp.concatenate([ops[u]["al"], mv[n]], axis=1)) for n, (u, h) in enumerate(ch)]
        baw = [mm(b_br[n], aw[n]) for n in range(len(ch))]
        bkv = [mm(b_kr[n], ops[u]["v"]) for n, (u, h) in enumerate(ch)]
        terms = []
        for u, o in enumerate(ops):
            n0, n1 = u * HEAD_PAIR, u * HEAD_PAIR + 1
            pick = lambda f: jnp.where(head0, f(n0), f(n1))
            a2 = pick(lambda n: aw[n][:, :LANES])
            w = pick(lambda n: aw[n][:, LANES:])
            r2 = o["rt"] + pick(lambda n: baw[n][:, :LANES])
            y0 = pick(lambda n: baw[n][:, LANES:] + bkv[n])
            tc_t = jnp.where(same_head, mm_tn(o["bt2"], a2), 0.0)
            gc_t = jnp.where(same_head, mm_tn(jnp.concatenate([o["bt2"], o["kt2"]], axis=0),
                                              jnp.concatenate([w, o["v"]], axis=0)), 0.0)
            dec_col = jnp.sum(jnp.where(eye_l, jnp.exp(o["c_end"]), 0.0), axis=1, keepdims=True)
            terms.append((r2, y0, tc_t, gc_t, dec_col))
        return terms

    def group_body(c, carry):
        rows = [pl.ds(pl.multiple_of((c * unroll + u) * C, C), C) for u in range(unroll)]
        terms = group_terms(rows)
        st = st_ref[...]
        for rw, (r2, y0, tc_t, gc_t, dec_col) in zip(rows, terms):
            yraw_ref[rw, :] = mm(r2, st) + y0
            st = dec_col * st + mm(tc_t, st) + gc_t
        st_ref[...] = st
        return carry

    lax.fori_loop(0, tl // (C * unroll), group_body, 0)
    sf_ref[0, 0] = st_ref[...]

    def head_sum(z):
        s0 = jnp.sum(jnp.where(head0, z, 0.0), axis=1, keepdims=True)
        s1 = jnp.sum(jnp.where(head0, 0.0, z), axis=1, keepdims=True)
        return jnp.where(head0, s0, s1)

    y = yraw_ref[...]
    d = y - head_sum(y) * (1.0 / HEAD_DIM)
    var = head_sum(d * d) * (1.0 / HEAD_DIM)
    yn = d * lax.rsqrt(var + LNX_EPS) * lg_ref[...] + lb_ref[...]
    bonus = head_sum(r_ref[0] * k_ref[0] * rk_ref[...]) * v_ref[0]
    y_ref[0] = (yn + bonus) * g_ref[0]


def _rwkv_scan_call(ops, st0, r_k, lnx_g, lnx_b, tl, chunk, unroll):
    n, l, _ = ops[0].shape
    n_hp = H_RWKV // HEAD_PAIR
    mxu = BF16 if chunk >= 16 else F32
    tok = pl.BlockSpec((1, tl, LANES), lambda n, h, i: (n, i, h))
    par = pl.BlockSpec((1, LANES), lambda n, h, i: (0, h))
    st_spec = pl.BlockSpec((1, 1, LANES, LANES), lambda n, h, i: (n, h, 0, 0))
    return pl.pallas_call(
        functools.partial(_rwkv_scan_kernel, chunk=chunk, unroll=unroll, mxu=mxu),
        grid=(n, n_hp, l // tl),
        in_specs=[tok] * 7 + [st_spec, par, par, par],
        out_specs=[tok, st_spec],
        out_shape=[jax.ShapeDtypeStruct((n, l, D_RWKV), F32),
                   jax.ShapeDtypeStruct((n, n_hp, LANES, LANES), F32)],
        scratch_shapes=[pltpu.VMEM((LANES, LANES), F32), pltpu.VMEM((tl, LANES), F32)],
        compiler_params=_cparams("parallel", "parallel", "arbitrary"),
        name="rwkv_scan",
    )(*ops, st0, r_k.reshape(1, D_RWKV), lnx_g.reshape(1, D_RWKV), lnx_b.reshape(1, D_RWKV))


def _state_to_pairs(s):
    n, h = s.shape[:2]
    st = jnp.swapaxes(s, -1, -2).reshape(n, h // HEAD_PAIR, HEAD_PAIR, HEAD_DIM, HEAD_DIM)
    z = jnp.zeros_like(st[:, :, 0])
    top = jnp.concatenate([st[:, :, 0], z], axis=-1)
    bot = jnp.concatenate([z, st[:, :, 1]], axis=-1)
    return jnp.concatenate([top, bot], axis=-2)


def _pairs_to_state(sp):
    n, n_hp = sp.shape[:2]
    d0 = sp[:, :, :HEAD_DIM, :HEAD_DIM]
    d1 = sp[:, :, HEAD_DIM:, HEAD_DIM:]
    st = jnp.stack([d0, d1], axis=2).reshape(n, n_hp * HEAD_PAIR, HEAD_DIM, HEAD_DIM)
    return jnp.swapaxes(st, -1, -2)


def _rwkv_mix(p_rw, shift0, s0, prm, tc, tl, chunk, unroll=1):
    ops = _rwkv_pre_call(p_rw, shift0, prm["mu"], prm["w0"], prm["w2"], prm["a0"], prm["a2"], prm["g2"],
                         prm["k_k"], prm["k_a"], tc)
    y, sp = _rwkv_scan_call(ops, _state_to_pairs(s0), prm["r_k"], prm["lnx_g"], prm["lnx_b"], tl, chunk, unroll)
    return y, _pairs_to_state(sp), p_rw[:, -1]


def _layer_norm(z, g, b):
    mu = jnp.mean(z, axis=-1, keepdims=True)
    d = z - mu
    var = jnp.mean(d * d, axis=-1, keepdims=True)
    return d * lax.rsqrt(var + LN_EPS) * g + b


def _outproj_kernel(ya_ref, yr_ref, x_ref, gt_ref, w_ref, g_ref, b_ref, o_ref):
    o = (_dot(ya_ref[0].astype(BF16), w_ref[:D_ATT, :]) + _dot(yr_ref[0].astype(BF16), w_ref[D_ATT:, :]))
    o_ref[0] = _layer_norm(ALPHA * x_ref[0] + (1.0 + gt_ref[0]) * o, g_ref[...], b_ref[...])


def _outproj_call(y_att, y_rw, x, gt, w_bf, ln_g, ln_b, tm):
    n, l, d = x.shape
    row = pl.BlockSpec((1, d), lambda n, i: (0, 0))
    return pl.pallas_call(
        _outproj_kernel,
        grid=(n, l // tm),
        in_specs=[pl.BlockSpec((1, tm, D_ATT), lambda n, i: (n, i, 0)),
                  pl.BlockSpec((1, tm, D_RWKV), lambda n, i: (n, i, 0)),
                  pl.BlockSpec((1, tm, d), lambda n, i: (n, i, 0)),
                  _mod_spec(gt, tm),
                  pl.BlockSpec(w_bf.shape, lambda n, i: (0, 0)), row, row],
        out_specs=pl.BlockSpec((1, tm, d), lambda n, i: (n, i, 0)),
        out_shape=jax.ShapeDtypeStruct((n, l, d), F32),
        compiler_params=_cparams("parallel", "parallel"),
        name="out_proj_ln",
    )(y_att, y_rw, x, gt, w_bf, ln_g.reshape(1, d), ln_b.reshape(1, d))


def _ffn_kernel(x_ref, sc_ref, sh_ref, gt_ref, wr_ref, wg_ref, wu_ref, wd_ref, g_ref, b_ref, o_ref,
                u_ref, acc_ref, gate_ref, *, n_experts):
    e = pl.program_id(2)
    f = pl.program_id(3)
    last = (e == pl.num_programs(2) - 1) & (f == pl.num_programs(3) - 1)

    @pl.when((e == 0) & (f == 0))
    def _():
        u = x_ref[0] * (1.0 + sc_ref[0]) + sh_ref[0]
        u_ref[...] = u.astype(BF16)
        acc_ref[...] = jnp.zeros_like(acc_ref)
        if n_experts > 1:
            logits = _dot(u, wr_ref[...], precision=HI)
            lane = lax.broadcasted_iota(I32, logits.shape, 1)
            logits = jnp.where(lane < n_experts, logits, NEG_INF)
            v1 = jnp.max(logits, axis=1, keepdims=True)
            i1 = jnp.min(jnp.where(logits == v1, lane, LANES), axis=1, keepdims=True)
            rest = jnp.where(lane == i1, NEG_INF, logits)
            v2 = jnp.max(rest, axis=1, keepdims=True)
            i2 = jnp.min(jnp.where(rest == v2, lane, LANES), axis=1, keepdims=True)
            w2 = 1.0 / (1.0 + jnp.exp(v1 - v2))
            gate_ref[...] = jnp.where(lane == i1, 1.0 - w2, 0.0) + jnp.where(lane == i2, w2, 0.0)

    u = u_ref[...]
    hg = _dot(u, wg_ref[0])
    hu = _dot(u, wu_ref[0])
    h = hg * _sigmoid(hg) * hu
    if n_experts > 1:
        lane = lax.broadcasted_iota(I32, gate_ref.shape, 1)
        h = h * jnp.sum(jnp.where(lane == e, gate_ref[...], 0.0), axis=1, keepdims=True)
    acc_ref[...] += _dot(h.astype(BF16), wd_ref[0])

    @pl.when(last)
    def _():
        o_ref[0] = _layer_norm(ALPHA * x_ref[0] + (1.0 + gt_ref[0]) * acc_ref[...], g_ref[...], b_ref[...])


def _ffn_call(x, sc, sh, gt, w_router, wg_bf, wu_bf, wd_bf, ln_g, ln_b, tm, tf):
    n, l, d = x.shape
    n_e, _, d_ff = wg_bf.shape
    row = pl.BlockSpec((1, d), lambda n, i, e, f: (0, 0))
    mod = lambda m: (pl.BlockSpec((1, 1, d), lambda n, i, e, f: (n, 0, 0)) if m.shape[1] == 1
                     else pl.BlockSpec((1, tm, d), lambda n, i, e, f: (n, i, 0)))
    return pl.pallas_call(
        functools.partial(_ffn_kernel, n_experts=n_e),
        grid=(n, l // tm, n_e, d_ff // tf),
        in_specs=[pl.BlockSpec((1, tm, d), lambda n, i, e, f: (n, i, 0)),
                  mod(sc), mod(sh), mod(gt),
                  pl.BlockSpec(w_router.shape, lambda n, i, e, f: (0, 0)),
                  pl.BlockSpec((1, d, tf), lambda n, i, e, f: (e, 0, f)),
                  pl.BlockSpec((1, d, tf), lambda n, i, e, f: (e, 0, f)),
                  pl.BlockSpec((1, tf, d), lambda n, i, e, f: (e, f, 0)),
                  row, row],
        out_specs=pl.BlockSpec((1, tm, d), lambda n, i, e, f: (n, i, 0)),
        out_shape=jax.ShapeDtypeStruct((n, l, d), F32),
        scratch_shapes=[pltpu.VMEM((tm, d), BF16), pltpu.VMEM((tm, d), F32), pltpu.VMEM((tm, LANES), F32)],
        compiler_params=_cparams("parallel", "parallel", "arbitrary", "arbitrary"),
        name="ffn_ln" if n_e == 1 else "moe_ln",
    )(x, sc, sh, gt, w_router, wg_bf, wu_bf, wd_bf, ln_g.reshape(1, d), ln_b.reshape(1, d))


PAGES_PER_BLOCK = MOBA_BLOCK // PAGE_SIZE
KMEAN_PAGES = 8


def _kmean_kernel(pt_ref, *refs):
    o_ref = refs[-1]
    g = pl.program_id(1)

    @pl.when(g == 0)
    def _():
        o_ref[...] = jnp.zeros_like(o_ref)

    lane = lax.broadcasted_iota(I32, o_ref.shape[1:], 1)
    out = o_ref[0]
    for blk in range(KMEAN_PAGES // PAGES_PER_BLOCK):
        tot = functools.reduce(lambda a, c: a + c,
                               [refs[blk * PAGES_PER_BLOCK + r][...] for r in range(PAGES_PER_BLOCK)])
        col = jnp.sum(tot, axis=-1, keepdims=True).reshape(-1, 1) * (1.0 / MOBA_BLOCK)
        out = jnp.where(lane == g * (KMEAN_PAGES // PAGES_PER_BLOCK) + blk, col, out)
    o_ref[0] = out


def _kmean_call(cache_kt, layer, page_table, n_full):
    n_seq = page_table.shape[0]
    _, _, n_h, hd, ps = cache_kt.shape
    page = lambda r: pl.BlockSpec((None, None, n_h, hd, ps),
                                  lambda b, g, pt: (layer, pt[b, g * KMEAN_PAGES + r], 0, 0, 0))
    blocks_per_step = KMEAN_PAGES // PAGES_PER_BLOCK
    return pl.pallas_call(
        _kmean_kernel,
        grid_spec=pltpu.PrefetchScalarGridSpec(
            num_scalar_prefetch=1,
            grid=(n_seq, n_full // blocks_per_step),
            in_specs=[page(r) for r in range(KMEAN_PAGES)],
            out_specs=pl.BlockSpec((1, n_h * hd, n_full), lambda b, g, pt: (b, 0, 0))),
        out_shape=jax.ShapeDtypeStruct((n_seq, n_h * hd, n_full), F32),
        compiler_params=_cparams("parallel", "arbitrary"),
        name="moba_kmean",
    )(page_table, *([cache_kt] * KMEAN_PAGES))


def _top3_kernel(q_ref, km_ref, o_ref):
    n_h = q_ref.shape[1]
    for h in range(n_h):
        g = _dot(q_ref[0, h], km_ref[0, h], precision=HI)
        lane = lax.broadcasted_iota(I32, g.shape, 1)
        out_lane = lax.broadcasted_iota(I32, (g.shape[0], LANES), 1)
        out = jnp.zeros((g.shape[0], LANES), I32)
        for s in range(MOBA_TOPK):
            mx = jnp.max(g, axis=1, keepdims=True)
            idx = jnp.min(jnp.where(g == mx, lane, g.shape[1]), axis=1, keepdims=True)
            out = jnp.where(out_lane == s, idx, out)
            g = jnp.where(lane == idx, -jnp.inf, g)
        o_ref[0, h] = out


def _top3_call(q, k_mean_t):
    n_seq, n_h, t, hd = q.shape
    n_full = k_mean_t.shape[3]
    return pl.pallas_call(
        _top3_kernel,
        grid=(n_seq,),
        in_specs=[pl.BlockSpec((1, n_h, t, hd), lambda b: (b, 0, 0, 0)),
                  pl.BlockSpec((1, n_h, hd, n_full), lambda b: (b, 0, 0, 0))],
        out_specs=pl.BlockSpec((1, n_h, t, LANES), lambda b: (b, 0, 0, 0)),
        out_shape=jax.ShapeDtypeStruct((n_seq, n_h, t, LANES), I32),
        compiler_params=_cparams("parallel"),
        name="moba_top3",
    )(q, k_mean_t)


def _moba_sample_kernel(pt_ref, sel_ref, rb_ref, qt_ref, knt_ref, vnt_ref, town_ref, tprev_ref, *refs,
                        n_h, n_t, n_full):
    n_pg = MOBA_TOPK * PAGES_PER_BLOCK
    k_refs = refs[:n_t * n_pg]
    v_refs = refs[n_t * n_pg:2 * n_t * n_pg]
    o_ref = refs[2 * n_t * n_pg]
    b = pl.program_id(0)
    h = pl.program_id(1)
    qt = qt_ref[0, 0] * ATT_SCALE
    knt = knt_ref[0, 0]
    vnt = vnt_ref[0, 0]
    b_far = rb_ref[N_BUCKETS - 1, h]
    out_lane = lax.broadcasted_iota(I32, (HEAD_DIM, n_t), 1)
    out = jnp.zeros((HEAD_DIM, n_t), F32)
    for t in range(n_t):
        q_col = qt[:, t:t + 1]
        base = ((b * n_h + h) * n_t + t) * MOBA_TOPK
        scores = [jnp.sum(knt * q_col, axis=0, keepdims=True) + town_ref[0, t:t + 1, :]]
        for s in range(MOBA_TOPK):
            last = sel_ref[base + s] == n_full - 1
            for r in range(PAGES_PER_BLOCK):
                sc = jnp.sum(k_refs[(t * MOBA_TOPK + s) * PAGES_PER_BLOCK + r][...] * q_col, axis=0, keepdims=True)
                bias = jnp.where(last, tprev_ref[0, t:t + 1, r * PAGE_SIZE:(r + 1) * PAGE_SIZE], b_far)
                scores.append(sc + bias)
        m = functools.reduce(jnp.maximum, [jnp.max(sc, axis=1, keepdims=True) for sc in scores])
        probs = [jnp.exp(sc - m) for sc in scores]
        denom = functools.reduce(lambda a, c: a + c, [jnp.sum(p, axis=1, keepdims=True) for p in probs])
        acc = functools.reduce(lambda a, c: a + c,
                               [probs[1 + j] * v_refs[t * n_pg + j][...] for j in range(n_pg)])
        col = (jnp.sum(acc, axis=1, keepdims=True) + jnp.sum(probs[0] * vnt, axis=1, keepdims=True)) / denom
        out = jnp.where(out_lane == t, col, out)
    o_ref[0, 0] = out


def _moba_sample_call(qt, knt, vnt, cache_kt, cache_vt, layer, page_table, sel_flat, rel_bias, ts_own, ts_prev, n_full):
    n_seq, n_h, hd, n_t = qt.shape
    ps = cache_kt.shape[4]
    tok = pl.BlockSpec((1, 1, hd, n_t), lambda b, h, pt, sel: (b, h, 0, 0))

    def page(t, s, r):
        def imap(b, h, pt, sel):
            blk = sel[((b * n_h + h) * n_t + t) * MOBA_TOPK + s]
            return (layer, pt[b, blk * PAGES_PER_BLOCK + r], h, 0, 0)
        return pl.BlockSpec((None, None, None, hd, ps), imap)

    pages = [page(t, s, r) for t in range(n_t) for s in range(MOBA_TOPK) for r in range(PAGES_PER_BLOCK)]
    return pl.pallas_call(
        functools.partial(_moba_sample_kernel, n_h=n_h, n_t=n_t, n_full=n_full),
        grid_spec=pltpu.PrefetchScalarGridSpec(
            num_scalar_prefetch=2,
            grid=(n_seq, n_h),
            in_specs=[pl.BlockSpec(memory_space=pltpu.SMEM), tok, tok, tok,
                      pl.BlockSpec((1, n_t, n_t), lambda b, h, pt, sel: (h, 0, 0)),
                      pl.BlockSpec((1, n_t, MOBA_BLOCK), lambda b, h, pt, sel: (h, 0, 0))] + pages + pages,
            out_specs=tok),
        out_shape=jax.ShapeDtypeStruct((n_seq, n_h, hd, n_t), F32),
        compiler_params=_cparams("parallel", "arbitrary"),
        name="moba_sample",
    )(page_table, sel_flat, rel_bias, qt, knt, vnt, ts_own, ts_prev,
      *([cache_kt] * len(pages)), *([cache_vt] * len(pages)))


def _moba_sample(p_att, cache_kt, cache_vt, layer, page_table, rel_bias, ts_own, ts_prev):
    n_seq, n_t, _ = p_att.shape
    n_full = page_table.shape[1] // PAGES_PER_BLOCK
    qkv = p_att.reshape(n_seq, n_t, 3, H_ATT, HEAD_DIM).transpose(2, 0, 3, 1, 4)
    q, k_new, v_new = qkv[0], qkv[1], qkv[2]
    k_mean_t = _kmean_call(cache_kt, layer, page_table, n_full).reshape(n_seq, H_ATT, HEAD_DIM, n_full)
    picks = _top3_call(q, k_mean_t)
    sel_flat = picks[..., :MOBA_TOPK].reshape(-1)
    tr = lambda z: jnp.swapaxes(z, 2, 3)
    out_t = _moba_sample_call(tr(q), tr(k_new), tr(v_new), cache_kt, cache_vt, layer, page_table, sel_flat, rel_bias,
                              ts_own, ts_prev, n_full)
    return out_t.transpose(0, 3, 1, 2).reshape(n_seq, n_t, D_ATT), k_new, v_new


PROMPT_TM = 512
PROMPT_RW_TC = 256
PROMPT_RW_TL = 512
PROMPT_RW_CHUNK = 32
PROMPT_RW_UNROLL = 4
FF_TILE = 256


def _to_pages(z):
    b, s, _ = z.shape
    return z.reshape(b, s // PAGE_SIZE, PAGE_SIZE, H_ATT, HEAD_DIM).transpose(0, 1, 3, 2, 4)


def kernel(x_prompt, x_sample, cache_k, cache_v, state_wkv, state_shift, page_table, c_prompt, c_sample, rel_bias, w_ada, b_ada, w_in, rw_mu, rw_w0, rw_w2, rw_a0, rw_a2, rw_g2, rw_k_k, rw_k_a, rw_r_k, rw_lnx_g, rw_lnx_b, w_out, ln1_g, ln1_b, ln2_g, ln2_b, ffn_w_gate, ffn_w_up, ffn_w_down, moe_w_router, moe_w_gate, moe_w_up, moe_w_down):
    n_p, seq, d = x_prompt.shape
    n_s, t_s, _ = x_sample.shape
    depth = w_in.shape[0]
    assert page_table.shape[1] % PAGES_PER_BLOCK == 0 and seq % MOBA_BLOCK == 0
    rows_s = n_s * t_s

    n_c = n_p + n_s
    pad_c = -n_c % 8
    c_all = jnp.pad(jnp.concatenate([c_prompt, c_sample], axis=0), ((0, pad_c), (0, 0)))
    mods = _ada_call(c_all, w_ada, b_ada)

    t_own, t_prev, ts_own, ts_prev = _bias_call(
        rel_bias, [(MOBA_BLOCK, MOBA_BLOCK), (MOBA_BLOCK, MOBA_BLOCK), (t_s, t_s), (t_s, MOBA_BLOCK)],
        [0, MOBA_BLOCK, 0, MOBA_BLOCK], key_major=[True, True, False, False])

    cache_kt = jnp.swapaxes(cache_k, 3, 4)
    cache_vt = jnp.swapaxes(cache_v, 3, 4)

    xp = x_prompt
    xs = x_sample.reshape(1, rows_s, d)
    zero_shift = jnp.zeros((n_p, RW_IN), F32)
    zero_state = jnp.zeros((n_p, H_RWKV, HEAD_DIM, HEAD_DIM), F32)
    outs = {k: [] for k in ("kp", "vp", "ks", "vs", "wp", "ws", "hp", "hs")}
    for l in range(depth):
        mod_p = [m[:, None, :] for m in jnp.split(mods[l, :n_p], 6, axis=-1)]
        mod_s = [jnp.repeat(m, t_s, axis=0)[None] for m in jnp.split(mods[l, n_p:n_c], 6, axis=-1)]
        prm = dict(mu=rw_mu[l], w0=rw_w0[l], w2=rw_w2[l], a0=rw_a0[l], a2=rw_a2[l], g2=rw_g2[l], k_k=rw_k_k[l],
                   k_a=rw_k_a[l], r_k=rw_r_k[l], lnx_g=rw_lnx_g[l], lnx_b=rw_lnx_b[l])
        w_in_bf = w_in[l].astype(BF16)
        w_out_bf = w_out[l].astype(BF16)
        if l % 2 == 0:
            w_router = jnp.zeros((d, LANES), F32)
            wg, wu, wd = (w[l // 2][None].astype(BF16) for w in (ffn_w_gate, ffn_w_up, ffn_w_down))
        else:
            w_router = jnp.pad(moe_w_router[l // 2], ((0, 0), (0, LANES - N_EXPERTS)))
            wg, wu, wd = (w[l // 2].astype(BF16) for w in (moe_w_gate, moe_w_up, moe_w_down))

        pa, pr = _inproj_call(xp, mod_p[1], mod_p[0], w_in_bf, PROMPT_TM)
        y_att = _moba_prompt_call(pa, rel_bias, t_own, t_prev)
        y_rw, s_new, sh_new = _rwkv_mix(pr, zero_shift, zero_state, prm, PROMPT_RW_TC, PROMPT_RW_TL, PROMPT_RW_CHUNK,
                                        PROMPT_RW_UNROLL)
        xp = _outproj_call(y_att, y_rw, xp, mod_p[2], w_out_bf, ln1_g[l], ln1_b[l], PROMPT_TM)
        xp = _ffn_call(xp, mod_p[4], mod_p[3], mod_p[5], w_router, wg, wu, wd, ln2_g[l], ln2_b[l], PROMPT_TM, FF_TILE)
        outs["kp"].append(_to_pages(pa[..., D_ATT:2 * D_ATT]))
        outs["vp"].append(_to_pages(pa[..., 2 * D_ATT:]))
        outs["wp"].append(s_new)
        outs["hp"].append(sh_new)

        pa, pr = _inproj_call(xs, mod_s[1], mod_s[0], w_in_bf, rows_s)
        y_att, k_new, v_new = _moba_sample(pa.reshape(n_s, t_s, ATT_IN), cache_kt, cache_vt, l, page_table, rel_bias,
                                           ts_own, ts_prev)
        y_rw, s_new, sh_new = _rwkv_mix(pr.reshape(n_s, t_s, RW_IN), state_shift[l], state_wkv[l], prm, t_s, t_s, t_s)
        xs = _outproj_call(y_att.reshape(1, rows_s, D_ATT), y_rw.reshape(1, rows_s, D_RWKV), xs, mod_s[2], w_out_bf,
                           ln1_g[l], ln1_b[l], rows_s)
        xs = _ffn_call(xs, mod_s[4], mod_s[3], mod_s[5], w_router, wg, wu, wd, ln2_g[l], ln2_b[l], rows_s, FF_TILE)
        outs["ks"].append(k_new)
        outs["vs"].append(v_new)
        outs["ws"].append(s_new)
        outs["hs"].append(sh_new)

    st = lambda k: jnp.stack(outs[k])
    return (xp, xs.reshape(n_s, t_s, d), st("kp"), st("vp"), st("ks"), st("vs"), st("wp"), st("ws"), st("hp"), st("hs"))
```

```python
import functools
import math

import jax
import jax.numpy as jnp
from jax import lax
from jax.experimental import pallas as pl
from jax.experimental.pallas import tpu as pltpu

F32 = jnp.float32
BF16 = jnp.bfloat16
I32 = jnp.int32
HI = lax.Precision.HIGHEST

D_MODEL = 1024
HEAD_DIM = 64
H_ATT = 8
H_RWKV = 8
D_ATT = H_ATT * HEAD_DIM
D_RWKV = H_RWKV * HEAD_DIM
ATT_IN = 3 * D_ATT
LORA_W = 64
LORA_A = 64
LORA_G = 128
RW_IN = 3 * D_RWKV + LORA_W + LORA_A + LORA_G
MOBA_BLOCK = 256
MOBA_TOPK = 3
PAGE_SIZE = 128
N_BUCKETS = 32
MAX_EXACT = N_BUCKETS // 2
MAX_DISTANCE = 128
D_FF = 2816
N_EXPERTS = 8
DEPTH = 2
ALPHA = (2 * DEPTH) ** 0.25
LN_EPS = 1e-5
LNX_EPS = 64e-5
NEG_INF = -1e30
ATT_SCALE = HEAD_DIM ** -0.5

LANES = 128
HEAD_PAIR = LANES // HEAD_DIM
BF16_ROWS = 16
VMEM_LIMIT = 56 * 1024 * 1024


def _cparams(*sem):
    return pltpu.CompilerParams(dimension_semantics=sem, vmem_limit_bytes=VMEM_LIMIT)


def _sigmoid(x):
    return 1.0 / (1.0 + jnp.exp(-x))


def _dot_nt(a, b, precision=None):
    return lax.dot_general(a, b, (((1,), (1,)), ((), ())), precision=precision, preferred_element_type=F32)


def _dot_tn(a, b, precision=None):
    return lax.dot_general(a, b, (((0,), (0,)), ((), ())), precision=precision, preferred_element_type=F32)


def _dot(a, b, precision=None):
    return jnp.dot(a, b, precision=precision, preferred_element_type=F32)


def _ada_kernel(c_ref, w_ref, b_ref, o_ref):
    c = c_ref[...]
    o_ref[0] = _dot(c * _sigmoid(c), w_ref[0], precision=HI) + b_ref[0]


def _ada_call(c_all, w_ada, b_ada):
    n = c_all.shape[0]
    depth, d, cols = w_ada.shape
    tn = 768
    return pl.pallas_call(
        _ada_kernel,
        grid=(depth, cols // tn),
        in_specs=[pl.BlockSpec((n, d), lambda l, j: (0, 0)),
                  pl.BlockSpec((1, d, tn), lambda l, j: (l, 0, j)),
                  pl.BlockSpec((1, 1, tn), lambda l, j: (l, 0, j))],
        out_specs=pl.BlockSpec((1, n, tn), lambda l, j: (l, 0, j)),
        out_shape=jax.ShapeDtypeStruct((depth, n, cols), F32),
        compiler_params=_cparams("parallel", "parallel"),
        name="ada_mod",
    )(c_all, w_ada, b_ada.reshape(depth, 1, cols))


def _bucket_of(dist):
    d = jnp.maximum(dist, 0)
    log_ratio = jnp.log(jnp.maximum(d, 1).astype(F32) / MAX_EXACT) / math.log(MAX_DISTANCE / MAX_EXACT)
    large = jnp.minimum(MAX_EXACT + (log_ratio * (N_BUCKETS - MAX_EXACT)).astype(I32), N_BUCKETS - 1)
    return jnp.where(d < MAX_EXACT, d, large)


def _bias_kernel(rb_ref, *o_refs, offsets, key_major):
    for o_ref, off, km in zip(o_refs, offsets, key_major):
        n_h, rows, cols = o_ref.shape
        dist = (lax.broadcasted_iota(I32, (rows, cols), 0) - lax.broadcasted_iota(I32, (rows, cols), 1))
        dist = (-dist if km else dist) + off
        bucket = _bucket_of(dist)
        for h in range(n_h):
            acc = jnp.zeros((rows, cols), F32)
            for b in range(N_BUCKETS):
                acc = jnp.where(bucket == b, rb_ref[b, h], acc)
            o_ref[h] = jnp.where(dist >= 0, acc, NEG_INF)


def _bias_call(rel_bias, shapes, offsets, key_major=None):
    n_h = rel_bias.shape[1]
    key_major = tuple(key_major) if key_major is not None else (False,) * len(shapes)
    return pl.pallas_call(
        functools.partial(_bias_kernel, offsets=tuple(offsets), key_major=key_major),
        in_specs=[pl.BlockSpec(memory_space=pltpu.SMEM)],
        out_specs=[pl.BlockSpec(memory_space=pltpu.VMEM) for _ in shapes],
        out_shape=[jax.ShapeDtypeStruct((n_h,) + s, F32) for s in shapes],
        name="rel_bias_tables",
    )(rel_bias)


def _inproj_kernel(x_ref, sc_ref, sh_ref, w_ref, oa_ref, or_ref, *, chunk):
    u = (x_ref[0] * (1.0 + sc_ref[0]) + sh_ref[0]).astype(BF16)
    att = oa_ref.shape[-1]
    for c0 in range(0, w_ref.shape[1], chunk):
        res = _dot(u, w_ref[:, c0:c0 + chunk])
        if c0 < att:
            oa_ref[0, :, c0:c0 + chunk] = res
        else:
            or_ref[0, :, c0 - att:c0 - att + chunk] = res


def _mod_spec(mod, tm):
    if mod.shape[1] == 1:
        return pl.BlockSpec((1, 1, mod.shape[2]), lambda n, i: (n, 0, 0))
    return pl.BlockSpec((1, tm, mod.shape[2]), lambda n, i: (n, i, 0))


def _inproj_call(x, sc, sh, w_bf, tm):
    n, l, d = x.shape
    cols = w_bf.shape[1]
    return pl.pallas_call(
        functools.partial(_inproj_kernel, chunk=256),
        grid=(n, l // tm),
        in_specs=[pl.BlockSpec((1, tm, d), lambda n, i: (n, i, 0)),
                  _mod_spec(sc, tm), _mod_spec(sh, tm),
                  pl.BlockSpec((d, cols), lambda n, i: (0, 0))],
        out_specs=[pl.BlockSpec((1, tm, ATT_IN), lambda n, i: (n, i, 0)),
                   pl.BlockSpec((1, tm, RW_IN), lambda n, i: (n, i, 0))],
        out_shape=[jax.ShapeDtypeStruct((n, l, ATT_IN), F32),
                   jax.ShapeDtypeStruct((n, l, RW_IN), F32)],
        compiler_params=_cparams("parallel", "parallel"),
        name="in_proj",
    )(x, sc, sh, w_bf)


def _moba_prompt_kernel(rb_ref, q_ref, k_ref, v_ref, town_ref, tprev_ref, o_ref, kb_ref, vb_ref, km_ref,
                        *, n_blk):
    hp = pl.program_id(1)
    i = pl.program_id(2)
    blk = MOBA_BLOCK

    @pl.when(i == 0)
    def _():
        k = k_ref[0]
        kb_ref[...] = k.astype(BF16)
        km_ref[...] = jnp.sum(k.reshape(n_blk, blk, LANES), axis=1) * (1.0 / blk)
        for c in range(n_blk):
            vb_ref[:, c * blk:(c + 1) * blk] = v_ref[0, c * blk:(c + 1) * blk, :].T.astype(BF16)

    qt = q_ref[0].T
    dim_row = lax.broadcasted_iota(I32, qt.shape, 0)
    own0 = pl.multiple_of(i * blk, blk)
    heads = range(HEAD_PAIR)
    vrows = [slice(h2 * HEAD_DIM, (h2 + 1) * HEAD_DIM) for h2 in heads]
    bidx = lax.broadcasted_iota(I32, (n_blk, blk), 0)
    past = bidx < i
    qt_h = [jnp.where((dim_row // HEAD_DIM) == h2, qt, 0.0) for h2 in heads]
    qb = [(z * ATT_SCALE).astype(BF16) for z in qt_h]
    sel = []
    for h2 in heads:
        g = jnp.where(past, _dot(km_ref[...], qt_h[h2], precision=HI), NEG_INF)
        rank = jnp.zeros(g.shape, I32)
        for jp in range(n_blk):
            gj = g[jp:jp + 1, :]
            rank = rank + ((gj > g) | ((gj == g) & (jp < bidx))).astype(I32)
        sel.append(jnp.where((rank < MOBA_TOPK) & past, 1.0, 0.0))
    b_far = [rb_ref[N_BUCKETS - 1, hp * HEAD_PAIR + h2] for h2 in heads]

    def attend(blocks, carry):
        s = []
        for r0, biases, masks in blocks:
            kb = kb_ref[pl.ds(r0, blk), :]
            s_b = [_dot(kb, qb[h2]) + biases[h2] for h2 in heads]
            if masks is not None:
                s_b = [jnp.where(masks[h2], s_b[h2], NEG_INF) for h2 in heads]
            s.append(s_b)
        out = []
        for h2 in heads:
            m_new = functools.reduce(jnp.maximum, [jnp.max(s_b[h2], axis=0, keepdims=True) for s_b in s])
            if carry is not None:
                m, l, acc = carry[h2]
                m_new = jnp.maximum(m, m_new)
                a = jnp.exp(m - m_new)
                l, acc = a * l, a * acc
            else:
                l, acc = 0.0, 0.0
            for (r0, _, _), s_b in zip(blocks, s):
                p = jnp.exp(s_b[h2] - m_new)
                l = l + jnp.sum(p, axis=0, keepdims=True)
                acc = acc + _dot(vb_ref[vrows[h2], pl.ds(r0, blk)], p.astype(BF16))
            out.append((m_new, l, acc))
        return tuple(out)

    def far_block(j):
        masks = [jnp.sum(jnp.where(bidx == j, sel[h2], 0.0), axis=0, keepdims=True) > 0.5 for h2 in heads]
        return pl.multiple_of(j * blk, blk), b_far, masks

    def near_block(j):
        r0, _, masks = far_block(j)
        return r0, [tprev_ref[h2] for h2 in heads], masks

    carry = attend([(own0, [town_ref[h2] for h2 in heads], None)], None)
    n_far = jnp.maximum(i - 1, 0)
    carry = lax.fori_loop(0, n_far // 2, lambda j, c: attend([far_block(2 * j), far_block(2 * j + 1)], c), carry)
    carry = lax.fori_loop(2 * (n_far // 2), n_far, lambda j, c: attend([far_block(j)], c), carry)
    carry = lax.fori_loop(n_far, i, lambda j, c: attend([near_block(j)], c), carry)
    o_ref[0] = jnp.concatenate([acc / l for _, l, acc in carry], axis=0).T


def _moba_prompt_call(p_att, rel_bias, t_own, t_prev):
    n, l, _ = p_att.shape
    n_blk = l // MOBA_BLOCK
    n_hp = H_ATT // HEAD_PAIR
    tbl = pl.BlockSpec((HEAD_PAIR, MOBA_BLOCK, MOBA_BLOCK), lambda n, h, i: (h, 0, 0))
    return pl.pallas_call(
        functools.partial(_moba_prompt_kernel, n_blk=n_blk),
        grid=(n, n_hp, n_blk),
        in_specs=[pl.BlockSpec(memory_space=pltpu.SMEM),
                  pl.BlockSpec((1, MOBA_BLOCK, LANES), lambda n, h, i: (n, i, h)),
                  pl.BlockSpec((1, l, LANES), lambda n, h, i: (n, 0, n_hp + h)),
                  pl.BlockSpec((1, l, LANES), lambda n, h, i: (n, 0, 2 * n_hp + h)),
                  tbl, tbl],
        out_specs=pl.BlockSpec((1, MOBA_BLOCK, LANES), lambda n, h, i: (n, i, h)),
        out_shape=jax.ShapeDtypeStruct((n, l, D_ATT), F32),
        scratch_shapes=[pltpu.VMEM((l, LANES), BF16), pltpu.VMEM((LANES, l), BF16),
                        pltpu.VMEM((n_blk, LANES), F32)],
        compiler_params=_cparams("parallel", "parallel", "arbitrary"),
        name="moba_prompt",
    )(rel_bias, p_att, p_att, p_att, t_own, t_prev)


def _softplus(x):
    return jnp.maximum(x, 0.0) + jnp.log(1.0 + jnp.exp(-jnp.abs(x)))


def _rwkv_pre_kernel(p_ref, prev_ref, sh0_ref, mu_ref, w0_ref, w2_ref, a0_ref, a2_ref, g2_ref, kkw_ref, kaw_ref,
                     r_o, lw_o, k_o, v_o, kk_o, b_o, g_o):
    i = pl.program_id(1)
    p = p_ref[0]
    prev_row = jnp.where(i == 0, sh0_ref[0], prev_ref[0][7:8])
    row = lax.broadcasted_iota(I32, p.shape, 0)
    p_prev = jnp.where(row == 0, prev_row, pltpu.roll(p, 1, 0))
    xm = p + (p_prev - p) * mu_ref[...]
    o = 3 * D_RWKV
    r = xm[:, :D_RWKV]
    k = xm[:, D_RWKV:2 * D_RWKV]
    v = xm[:, 2 * D_RWKV:o]
    xw = xm[:, o:o + LORA_W]
    xa = xm[:, o + LORA_W:o + LORA_W + LORA_A]
    xg = xm[:, o + LORA_W + LORA_A:]
    w_log = -_softplus(-(w0_ref[...] + _dot(jnp.tanh(xw), w2_ref[...], precision=HI))) - 0.5
    a = _sigmoid(a0_ref[...] + _dot(xa, a2_ref[...], precision=HI))
    g = _dot(_sigmoid(xg), g2_ref[...], precision=HI)
    kk = k * kkw_ref[...]
    same_head = (lax.broadcasted_iota(I32, (D_RWKV, D_RWKV), 0) // HEAD_DIM
                 == lax.broadcasted_iota(I32, (D_RWKV, D_RWKV), 1) // HEAD_DIM)
    ss = _dot(kk * kk, jnp.where(same_head, 1.0, 0.0), precision=HI)
    kk = kk / jnp.maximum(jnp.sqrt(ss), 1e-12)
    r_o[0] = r
    lw_o[0] = -jnp.exp(w_log)
    k_o[0] = k * (1.0 + (a - 1.0) * kaw_ref[...])
    v_o[0] = v
    kk_o[0] = kk
    b_o[0] = kk * a
    g_o[0] = g


def _rwkv_pre_call(p_rw, shift0, mu, w0, w2, a0, a2, g2, k_k, k_a, tc):
    n, l, _ = p_rw.shape
    row = lambda z: z.reshape(1, -1)
    full = lambda z: pl.BlockSpec(z.shape, lambda n, i: (0,) * z.ndim)
    params = [row(mu), row(w0), w2, row(a0), a2, g2, row(k_k), row(k_a)]
    out_spec = pl.BlockSpec((1, tc, D_RWKV), lambda n, i: (n, i, 0))
    return pl.pallas_call(
        _rwkv_pre_kernel,
        grid=(n, l // tc),
        in_specs=[pl.BlockSpec((1, tc, RW_IN), lambda n, i: (n, i, 0)),
                  pl.BlockSpec((1, 8, RW_IN), lambda n, i: (n, jnp.maximum(i * (tc // 8) - 1, 0), 0)),
                  pl.BlockSpec((1, 1, RW_IN), lambda n, i: (n, 0, 0))] + [full(z) for z in params],
        out_specs=[out_spec] * 7,
        out_shape=[jax.ShapeDtypeStruct((n, l, D_RWKV), F32)] * 7,
        compiler_params=_cparams("parallel", "parallel"),
        name="rwkv_pre",
    )(p_rw, p_rw, shift0.reshape(n, 1, RW_IN), *params)


def _chunk_cumsum(x):
    rows = x.shape[0]
    row = lax.broadcasted_iota(I32, x.shape, 0)
    s = 1
    while s < rows:
        x = x + jnp.where(row >= s, pltpu.roll(x, s, 0), 0.0)
        s *= 2
    return x


def _rwkv_scan_kernel(r_ref, lw_ref, k_ref, v_ref, kk_ref, b_ref, g_ref, s0_ref, rk_ref, lg_ref, lb_ref,
                      y_ref, sf_ref, st_ref, yraw_ref, *, chunk, unroll, mxu):
    i = pl.program_id(2)
    C = chunk
    tl = r_ref.shape[1]
    n_dbl = max(C.bit_length() - 2, 0)

    @pl.when(i == 0)
    def _():
        st_ref[...] = s0_ref[0, 0]

    lane = lax.broadcasted_iota(I32, (1, LANES), 1)
    head0 = lane < HEAD_DIM
    ti = lax.broadcasted_iota(I32, (C, C), 0)
    si = lax.broadcasted_iota(I32, (C, C), 1)
    strict = si < ti
    incl = si <= ti
    eye_c = jnp.where(si == ti, 1.0, 0.0)
    rj = lax.broadcasted_iota(I32, (LANES, LANES), 0)
    cj = lax.broadcasted_iota(I32, (LANES, LANES), 1)
    same_head = (rj // HEAD_DIM) == (cj // HEAD_DIM)
    eye_l = rj == cj
    mm = lambda a, b: _dot(a.astype(mxu), b.astype(mxu))
    mm_nt = lambda a, b: _dot_nt(a.astype(mxu), b.astype(mxu))
    mm_tn = lambda a, b: _dot_tn(a.astype(mxu), b.astype(mxu))

    def chunk_operands(rows):
        lw = lw_ref[0, rows, :]
        r = r_ref[0, rows, :]
        k = k_ref[0, rows, :]
        v = v_ref[0, rows, :]
        kk = kk_ref[0, rows, :]
        b = b_ref[0, rows, :]
        cum = _chunk_cumsum(lw)
        c_end = cum[C - 1:C, :]
        e_neg = jnp.exp(-cum)
        e_end = jnp.exp(c_end - cum)
        al = -kk * jnp.exp(cum - lw)
        rt = r * jnp.exp(cum)
        return dict(v=v, al=al, rt=rt, bt=b * e_neg, kt=k * e_neg, bt2=b * e_end, kt2=k * e_end,
                    lhs=jnp.concatenate([al, rt], axis=0), c_end=c_end)

    def group_terms(rows_list):
        ops = [chunk_operands(rw) for rw in rows_list]
        ch = [(u, h) for u in range(len(ops)) for h in range(HEAD_PAIR)]
        mine = [head0, jnp.logical_not(head0)]
        lhs = [jnp.where(mine[h], ops[u]["lhs"], 0.0) for u, h in ch]
        ab = [mm_nt(lhs[n], ops[u]["bt"]) for n, (u, h) in enumerate(ch)]
        ak = [mm_nt(lhs[n], ops[u]["kt"]) for n, (u, h) in enumerate(ch)]
        m_ab = [jnp.where(strict, z[:C], 0.0) for z in ab]
        m_ak = [jnp.where(strict, z[:C], 0.0) for z in ak]
        b_br = [jnp.where(incl, z[C:], 0.0) for z in ab]
        b_kr = [jnp.where(incl, z[C:], 0.0) for z in ak]
        t_inv = [eye_c + z for z in m_ab]
        pk = m_ab
        for _ in range(n_dbl):
            pk = [mm(z, z) for z in pk]
            t_inv = [t + mm(z, t) for z, t in zip(pk, t_inv)]
        mv = [mm(m_ak[n], ops[u]["v"]) for n, (u, h) in enumerate(ch)]
        aw = [mm(t_inv[n], jnp.concatenate([ops[u]["al"], mv[n]], axis=1)) for n, (u, h) in enumerate(ch)]
        baw = [mm(b_br[n], aw[n]) for n in range(len(ch))]
        bkv = [mm(b_kr[n], ops[u]["v"]) for n, (u, h) in enumerate(ch)]
        terms = []
        for u, o in enumerate(ops):
            n0, n1 = u * HEAD_PAIR, u * HEAD_PAIR + 1
            pick = lambda f: jnp.where(head0, f(n0), f(n1))
            a2 = pick(lambda n: aw[n][:, :LANES])
            w = pick(lambda n: aw[n][:, LANES:])
            r2 = o["rt"] + pick(lambda n: baw[n][:, :LANES])
            y0 = pick(lambda n: baw[n][:, LANES:] + bkv[n])
            tc_t = jnp.where(same_head, mm_tn(o["bt2"], a2), 0.0)
            gc_t = jnp.where(same_head, mm_tn(jnp.concatenate([o["bt2"], o["kt2"]], axis=0),
                                              jnp.concatenate([w, o["v"]], axis=0)), 0.0)
            dec_col = jnp.sum(jnp.where(eye_l, jnp.exp(o["c_end"]), 0.0), axis=1, keepdims=True)
            terms.append((r2, y0, tc_t, gc_t, dec_col))
        return terms

    def group_body(c, carry):
        rows = [pl.ds(pl.multiple_of((c * unroll + u) * C, C), C) for u in range(unroll)]
        terms = group_terms(rows)
        st = st_ref[...]
        for rw, (r2, y0, tc_t, gc_t, dec_col) in zip(rows, terms):
            yraw_ref[rw, :] = mm(r2, st) + y0
            st = dec_col * st + mm(tc_t, st) + gc_t
        st_ref[...] = st
        return carry

    lax.fori_loop(0, tl // (C * unroll), group_body, 0)
    sf_ref[0, 0] = st_ref[...]

    def head_sum(z):
        s0 = jnp.sum(jnp.where(head0, z, 0.0), axis=1, keepdims=True)
        s1 = jnp.sum(jnp.where(head0, 0.0, z), axis=1, keepdims=True)
        return jnp.where(head0, s0, s1)

    y = yraw_ref[...]
    d = y - head_sum(y) * (1.0 / HEAD_DIM)
    var = head_sum(d * d) * (1.0 / HEAD_DIM)
    yn = d * lax.rsqrt(var + LNX_EPS) * lg_ref[...] + lb_ref[...]
    bonus = head_sum(r_ref[0] * k_ref[0] * rk_ref[...]) * v_ref[0]
    y_ref[0] = (yn + bonus) * g_ref[0]


def _rwkv_scan_call(ops, st0, r_k, lnx_g, lnx_b, tl, chunk, unroll):
    n, l, _ = ops[0].shape
    n_hp = H_RWKV // HEAD_PAIR
    mxu = BF16 if chunk >= 16 else F32
    tok = pl.BlockSpec((1, tl, LANES), lambda n, h, i: (n, i, h))
    par = pl.BlockSpec((1, LANES), lambda n, h, i: (0, h))
    st_spec = pl.BlockSpec((1, 1, LANES, LANES), lambda n, h, i: (n, h, 0, 0))
    return pl.pallas_call(
        functools.partial(_rwkv_scan_kernel, chunk=chunk, unroll=unroll, mxu=mxu),
        grid=(n, n_hp, l // tl),
        in_specs=[tok] * 7 + [st_spec, par, par, par],
        out_specs=[tok, st_spec],
        out_shape=[jax.ShapeDtypeStruct((n, l, D_RWKV), F32),
                   jax.ShapeDtypeStruct((n, n_hp, LANES, LANES), F32)],
        scratch_shapes=[pltpu.VMEM((LANES, LANES), F32), pltpu.VMEM((tl, LANES), F32)],
        compiler_params=_cparams("parallel", "parallel", "arbitrary"),
        name="rwkv_scan",
    )(*ops, st0, r_k.reshape(1, D_RWKV), lnx_g.reshape(1, D_RWKV), lnx_b.reshape(1, D_RWKV))


def _state_to_pairs(s):
    n, h = s.shape[:2]
    st = jnp.swapaxes(s, -1, -2).reshape(n, h // HEAD_PAIR, HEAD_PAIR, HEAD_DIM, HEAD_DIM)
    z = jnp.zeros_like(st[:, :, 0])
    top = jnp.concatenate([st[:, :, 0], z], axis=-1)
    bot = jnp.concatenate([z, st[:, :, 1]], axis=-1)
    return jnp.concatenate([top, bot], axis=-2)


def _pairs_to_state(sp):
    n, n_hp = sp.shape[:2]
    d0 = sp[:, :, :HEAD_DIM, :HEAD_DIM]
    d1 = sp[:, :, HEAD_DIM:, HEAD_DIM:]
    st = jnp.stack([d0, d1], axis=2).reshape(n, n_hp * HEAD_PAIR, HEAD_DIM, HEAD_DIM)
    return jnp.swapaxes(st, -1, -2)


def _rwkv_mix(p_rw, shift0, s0, prm, tc, tl, chunk, unroll=1):
    ops = _rwkv_pre_call(p_rw, shift0, prm["mu"], prm["w0"], prm["w2"], prm["a0"], prm["a2"], prm["g2"],
                         prm["k_k"], prm["k_a"], tc)
    y, sp = _rwkv_scan_call(ops, _state_to_pairs(s0), prm["r_k"], prm["lnx_g"], prm["lnx_b"], tl, chunk, unroll)
    return y, _pairs_to_state(sp), p_rw[:, -1]


def _layer_norm(z, g, b):
    mu = jnp.mean(z, axis=-1, keepdims=True)
    d = z - mu
    var = jnp.mean(d * d, axis=-1, keepdims=True)
    return d * lax.rsqrt(var + LN_EPS) * g + b


def _outproj_kernel(ya_ref, yr_ref, x_ref, gt_ref, w_ref, g_ref, b_ref, o_ref):
    o = (_dot(ya_ref[0].astype(BF16), w_ref[:D_ATT, :]) + _dot(yr_ref[0].astype(BF16), w_ref[D_ATT:, :]))
    o_ref[0] = _layer_norm(ALPHA * x_ref[0] + (1.0 + gt_ref[0]) * o, g_ref[...], b_ref[...])


def _outproj_call(y_att, y_rw, x, gt, w_bf, ln_g, ln_b, tm):
    n, l, d = x.shape
    row = pl.BlockSpec((1, d), lambda n, i: (0, 0))
    return pl.pallas_call(
        _outproj_kernel,
        grid=(n, l // tm),
        in_specs=[pl.BlockSpec((1, tm, D_ATT), lambda n, i: (n, i, 0)),
                  pl.BlockSpec((1, tm, D_RWKV), lambda n, i: (n, i, 0)),
                  pl.BlockSpec((1, tm, d), lambda n, i: (n, i, 0)),
                  _mod_spec(gt, tm),
                  pl.BlockSpec(w_bf.shape, lambda n, i: (0, 0)), row, row],
        out_specs=pl.BlockSpec((1, tm, d), lambda n, i: (n, i, 0)),
        out_shape=jax.ShapeDtypeStruct((n, l, d), F32),
        compiler_params=_cparams("parallel", "parallel"),
        name="out_proj_ln",
    )(y_att, y_rw, x, gt, w_bf, ln_g.reshape(1, d), ln_b.reshape(1, d))


def _ffn_kernel(x_ref, sc_ref, sh_ref, gt_ref, wr_ref, wg_ref, wu_ref, wd_ref, g_ref, b_ref, o_ref,
                u_ref, acc_ref, gate_ref, *, n_experts):
    e = pl.program_id(2)
    f = pl.program_id(3)
    last = (e == pl.num_programs(2) - 1) & (f == pl.num_programs(3) - 1)

    @pl.when((e == 0) & (f == 0))
    def _():
        u = x_ref[0] * (1.0 + sc_ref[0]) + sh_ref[0]
        u_ref[...] = u.astype(BF16)
        acc_ref[...] = jnp.zeros_like(acc_ref)
        if n_experts > 1:
            logits = _dot(u, wr_ref[...], precision=HI)
            lane = lax.broadcasted_iota(I32, logits.shape, 1)
            logits = jnp.where(lane < n_experts, logits, NEG_INF)
            v1 = jnp.max(logits, axis=1, keepdims=True)
            i1 = jnp.min(jnp.where(logits == v1, lane, LANES), axis=1, keepdims=True)
            rest = jnp.where(lane == i1, NEG_INF, logits)
            v2 = jnp.max(rest, axis=1, keepdims=True)
            i2 = jnp.min(jnp.where(rest == v2, lane, LANES), axis=1, keepdims=True)
            w2 = 1.0 / (1.0 + jnp.exp(v1 - v2))
            gate_ref[...] = jnp.where(lane == i1, 1.0 - w2, 0.0) + jnp.where(lane == i2, w2, 0.0)

    u = u_ref[...]
    hg = _dot(u, wg_ref[0])
    hu = _dot(u, wu_ref[0])
    h = hg * _sigmoid(hg) * hu
    if n_experts > 1:
        lane = lax.broadcasted_iota(I32, gate_ref.shape, 1)
        h = h * jnp.sum(jnp.where(lane == e, gate_ref[...], 0.0), axis=1, keepdims=True)
    acc_ref[...] += _dot(h.astype(BF16), wd_ref[0])

    @pl.when(last)
    def _():
        o_ref[0] = _layer_norm(ALPHA * x_ref[0] + (1.0 + gt_ref[0]) * acc_ref[...], g_ref[...], b_ref[...])


def _ffn_call(x, sc, sh, gt, w_router, wg_bf, wu_bf, wd_bf, ln_g, ln_b, tm, tf):
    n, l, d = x.shape
    n_e, _, d_ff = wg_bf.shape
    row = pl.BlockSpec((1, d), lambda n, i, e, f: (0, 0))
    mod = lambda m: (pl.BlockSpec((1, 1, d), lambda n, i, e, f: (n, 0, 0)) if m.shape[1] == 1
                     else pl.BlockSpec((1, tm, d), lambda n, i, e, f: (n, i, 0)))
    return pl.pallas_call(
        functools.partial(_ffn_kernel, n_experts=n_e),
        grid=(n, l // tm, n_e, d_ff // tf),
        in_specs=[pl.BlockSpec((1, tm, d), lambda n, i, e, f: (n, i, 0)),
                  mod(sc), mod(sh), mod(gt),
                  pl.BlockSpec(w_router.shape, lambda n, i, e, f: (0, 0)),
                  pl.BlockSpec((1, d, tf), lambda n, i, e, f: (e, 0, f)),
                  pl.BlockSpec((1, d, tf), lambda n, i, e, f: (e, 0, f)),
                  pl.BlockSpec((1, tf, d), lambda n, i, e, f: (e, f, 0)),
                  row, row],
        out_specs=pl.BlockSpec((1, tm, d), lambda n, i, e, f: (n, i, 0)),
        out_shape=jax.ShapeDtypeStruct((n, l, d), F32),
        scratch_shapes=[pltpu.VMEM((tm, d), BF16), pltpu.VMEM((tm, d), F32), pltpu.VMEM((tm, LANES), F32)],
        compiler_params=_cparams("parallel", "parallel", "arbitrary", "arbitrary"),
        name="ffn_ln" if n_e == 1 else "moe_ln",
    )(x, sc, sh, gt, w_router, wg_bf, wu_bf, wd_bf, ln_g.reshape(1, d), ln_b.reshape(1, d))


PAGES_PER_BLOCK = MOBA_BLOCK // PAGE_SIZE
KMEAN_PAGES = 8


def _kmean_kernel(pt_ref, *refs):
    o_ref = refs[-1]
    g = pl.program_id(1)

    @pl.when(g == 0)
    def _():
        o_ref[...] = jnp.zeros_like(o_ref)

    lane = lax.broadcasted_iota(I32, o_ref.shape[1:], 1)
    out = o_ref[0]
    for blk in range(KMEAN_PAGES // PAGES_PER_BLOCK):
        tot = functools.reduce(lambda a, c: a + c,
                               [refs[blk * PAGES_PER_BLOCK + r][...] for r in range(PAGES_PER_BLOCK)])
        col = jnp.sum(tot, axis=-1, keepdims=True).reshape(-1, 1) * (1.0 / MOBA_BLOCK)
        out = jnp.where(lane == g * (KMEAN_PAGES // PAGES_PER_BLOCK) + blk, col, out)
    o_ref[0] = out


def _kmean_call(cache_kt, layer, page_table, n_full):
    n_seq = page_table.shape[0]
    _, _, n_h, hd, ps = cache_kt.shape
    page = lambda r: pl.BlockSpec((None, None, n_h, hd, ps),
                                  lambda b, g, pt: (layer, pt[b, g * KMEAN_PAGES + r], 0, 0, 0))
    blocks_per_step = KMEAN_PAGES // PAGES_PER_BLOCK
    return pl.pallas_call(
        _kmean_kernel,
        grid_spec=pltpu.PrefetchScalarGridSpec(
            num_scalar_prefetch=1,
            grid=(n_seq, n_full // blocks_per_step),
            in_specs=[page(r) for r in range(KMEAN_PAGES)],
            out_specs=pl.BlockSpec((1, n_h * hd, n_full), lambda b, g, pt: (b, 0, 0))),
        out_shape=jax.ShapeDtypeStruct((n_seq, n_h * hd, n_full), F32),
        compiler_params=_cparams("parallel", "arbitrary"),
        name="moba_kmean",
    )(page_table, *([cache_kt] * KMEAN_PAGES))


def _top3_kernel(q_ref, km_ref, o_ref):
    n_h = q_ref.shape[1]
    for h in range(n_h):
        g = _dot(q_ref[0, h], km_ref[0, h], precision=HI)
        lane = lax.broadcasted_iota(I32, g.shape, 1)
        out_lane = lax.broadcasted_iota(I32, (g.shape[0], LANES), 1)
        out = jnp.zeros((g.shape[0], LANES), I32)
        for s in range(MOBA_TOPK):
            mx = jnp.max(g, axis=1, keepdims=True)
            idx = jnp.min(jnp.where(g == mx, lane, g.shape[1]), axis=1, keepdims=True)
            out = jnp.where(out_lane == s, idx, out)
            g = jnp.where(lane == idx, -jnp.inf, g)
        o_ref[0, h] = out


def _top3_call(q, k_mean_t):
    n_seq, n_h, t, hd = q.shape
    n_full = k_mean_t.shape[3]
    return pl.pallas_call(
        _top3_kernel,
        grid=(n_seq,),
        in_specs=[pl.BlockSpec((1, n_h, t, hd), lambda b: (b, 0, 0, 0)),
                  pl.BlockSpec((1, n_h, hd, n_full), lambda b: (b, 0, 0, 0))],
        out_specs=pl.BlockSpec((1, n_h, t, LANES), lambda b: (b, 0, 0, 0)),
        out_shape=jax.ShapeDtypeStruct((n_seq, n_h, t, LANES), I32),
        compiler_params=_cparams("parallel"),
        name="moba_top3",
    )(q, k_mean_t)


def _moba_sample_kernel(pt_ref, sel_ref, rb_ref, qt_ref, knt_ref, vnt_ref, town_ref, tprev_ref, ck_hbm, cv_hbm,
                        o_ref, kbuf, vbuf, sems, *, layer, n_seq, n_h, n_t, n_full):
    n_pg = MOBA_TOPK * PAGES_PER_BLOCK
    n_copies = n_t * n_pg
    b = pl.program_id(0)
    h = pl.program_id(1)
    step = b * n_h + h
    slot = step % 2

    def page_copies(step_f, slot_f):
        b_f = step_f // n_h
        h_f = step_f % n_h
        copies = []
        for t in range(n_t):
            for s in range(MOBA_TOPK):
                blk_id = sel_ref[(step_f * n_t + t) * MOBA_TOPK + s]
                for r in range(PAGES_PER_BLOCK):
                    pg = pt_ref[b_f, blk_id * PAGES_PER_BLOCK + r]
                    j = (t * MOBA_TOPK + s) * PAGES_PER_BLOCK + r
                    copies.append(pltpu.make_async_copy(ck_hbm.at[layer, pg, h_f], kbuf.at[slot_f, j], sems.at[0, slot_f]))
                    copies.append(pltpu.make_async_copy(cv_hbm.at[layer, pg, h_f], vbuf.at[slot_f, j], sems.at[1, slot_f]))
        return copies

    @pl.when(step == 0)
    def _():
        for c in page_copies(step, slot):
            c.start()

    @pl.when(step + 1 < n_seq * n_h)
    def _():
        for c in page_copies(step + 1, 1 - slot):
            c.start()

    for j in range(n_copies):
        pltpu.make_async_copy(ck_hbm.at[layer, 0, 0], kbuf.at[slot, j], sems.at[0, slot]).wait()
        pltpu.make_async_copy(cv_hbm.at[layer, 0, 0], vbuf.at[slot, j], sems.at[1, slot]).wait()

    k_refs = [kbuf.at[slot, j] for j in range(n_copies)]
    v_refs = [vbuf.at[slot, j] for j in range(n_copies)]
    qt = qt_ref[0, 0] * ATT_SCALE
    knt = knt_ref[0, 0]
    vnt = vnt_ref[0, 0]
    b_far = rb_ref[N_BUCKETS - 1, h]
    out_lane = lax.broadcasted_iota(I32, (HEAD_DIM, n_t), 1)
    out = jnp.zeros((HEAD_DIM, n_t), F32)
    for t in range(n_t):
        q_col = qt[:, t:t + 1]
        base = ((b * n_h + h) * n_t + t) * MOBA_TOPK
        scores = [jnp.sum(knt * q_col, axis=0, keepdims=True) + town_ref[0, t:t + 1, :]]
        for s in range(MOBA_TOPK):
            last = sel_ref[base + s] == n_full - 1
            for r in range(PAGES_PER_BLOCK):
                sc = jnp.sum(k_refs[(t * MOBA_TOPK + s) * PAGES_PER_BLOCK + r][...] * q_col, axis=0, keepdims=True)
                bias = jnp.where(last, tprev_ref[0, t:t + 1, r * PAGE_SIZE:(r + 1) * PAGE_SIZE], b_far)
                scores.append(sc + bias)
        m = functools.reduce(jnp.maximum, [jnp.max(sc, axis=1, keepdims=True) for sc in scores])
        probs = [jnp.exp(sc - m) for sc in scores]
        denom = functools.reduce(lambda a, c: a + c, [jnp.sum(p, axis=1, keepdims=True) for p in probs])
        acc = functools.reduce(lambda a, c: a + c,
                               [probs[1 + j] * v_refs[t * n_pg + j][...] for j in range(n_pg)])
        col = (jnp.sum(acc, axis=1, keepdims=True) + jnp.sum(probs[0] * vnt, axis=1, keepdims=True)) / denom
        out = jnp.where(out_lane == t, col, out)
    o_ref[0, 0] = out


def _moba_sample_call(qt, knt, vnt, cache_kt, cache_vt, layer, page_table, sel_flat, rel_bias, ts_own, ts_prev, n_full):
    n_seq, n_h, hd, n_t = qt.shape
    ps = cache_kt.shape[4]
    tok = pl.BlockSpec((1, 1, hd, n_t), lambda b, h, pt, sel: (b, h, 0, 0))
    n_copies = n_t * MOBA_TOPK * PAGES_PER_BLOCK
    return pl.pallas_call(
        functools.partial(_moba_sample_kernel, layer=layer, n_seq=n_seq, n_h=n_h, n_t=n_t, n_full=n_full),
        grid_spec=pltpu.PrefetchScalarGridSpec(
            num_scalar_prefetch=2,
            grid=(n_seq, n_h),
            in_specs=[pl.BlockSpec(memory_space=pltpu.SMEM), tok, tok, tok,
                      pl.BlockSpec((1, n_t, n_t), lambda b, h, pt, sel: (h, 0, 0)),
                      pl.BlockSpec((1, n_t, MOBA_BLOCK), lambda b, h, pt, sel: (h, 0, 0)),
                      pl.BlockSpec(memory_space=pl.ANY), pl.BlockSpec(memory_space=pl.ANY)],
            out_specs=tok,
            scratch_shapes=[pltpu.VMEM((2, n_copies, hd, ps), F32), pltpu.VMEM((2, n_copies, hd, ps), F32),
                            pltpu.SemaphoreType.DMA((2, 2))]),
        out_shape=jax.ShapeDtypeStruct((n_seq, n_h, hd, n_t), F32),
        compiler_params=_cparams("arbitrary", "arbitrary"),
        name="moba_sample",
    )(page_table, sel_flat, rel_bias, qt, knt, vnt, ts_own, ts_prev, cache_kt, cache_vt)


def _moba_sample(p_att, cache_kt, cache_vt, layer, page_table, rel_bias, ts_own, ts_prev):
    n_seq, n_t, _ = p_att.shape
    n_full = page_table.shape[1] // PAGES_PER_BLOCK
    qkv = p_att.reshape(n_seq, n_t, 3, H_ATT, HEAD_DIM).transpose(2, 0, 3, 1, 4)
    q, k_new, v_new = qkv[0], qkv[1], qkv[2]
    k_mean_t = _kmean_call(cache_kt, layer, page_table, n_full).reshape(n_seq, H_ATT, HEAD_DIM, n_full)
    picks = _top3_call(q, k_mean_t)
    sel_flat = picks[..., :MOBA_TOPK].reshape(-1)
    tr = lambda z: jnp.swapaxes(z, 2, 3)
    out_t = _moba_sample_call(tr(q), tr(k_new), tr(v_new), cache_kt, cache_vt, layer, page_table, sel_flat, rel_bias,
                              ts_own, ts_prev, n_full)
    return out_t.transpose(0, 3, 1, 2).reshape(n_seq, n_t, D_ATT), k_new, v_new


PROMPT_TM = 512
PROMPT_FF_TM = 1024
PROMPT_RW_TC = 256
PROMPT_RW_TL = 512
PROMPT_RW_CHUNK = 32
PROMPT_RW_UNROLL = 8
FF_TILE = 256


def _to_pages(z):
    b, s, _ = z.shape
    return z.reshape(b, s // PAGE_SIZE, PAGE_SIZE, H_ATT, HEAD_DIM).transpose(0, 1, 3, 2, 4)


def kernel(x_prompt, x_sample, cache_k, cache_v, state_wkv, state_shift, page_table, c_prompt, c_sample, rel_bias, w_ada, b_ada, w_in, rw_mu, rw_w0, rw_w2, rw_a0, rw_a2, rw_g2, rw_k_k, rw_k_a, rw_r_k, rw_lnx_g, rw_lnx_b, w_out, ln1_g, ln1_b, ln2_g, ln2_b, ffn_w_gate, ffn_w_up, ffn_w_down, moe_w_router, moe_w_gate, moe_w_up, moe_w_down):
    n_p, seq, d = x_prompt.shape
    n_s, t_s, _ = x_sample.shape
    depth = w_in.shape[0]
    assert page_table.shape[1] % PAGES_PER_BLOCK == 0 and seq % MOBA_BLOCK == 0
    rows_s = n_s * t_s

    n_c = n_p + n_s
    pad_c = -n_c % 8
    c_all = jnp.pad(jnp.concatenate([c_prompt, c_sample], axis=0), ((0, pad_c), (0, 0)))
    mods = _ada_call(c_all, w_ada, b_ada)

    t_own, t_prev, ts_own, ts_prev = _bias_call(
        rel_bias, [(MOBA_BLOCK, MOBA_BLOCK), (MOBA_BLOCK, MOBA_BLOCK), (t_s, t_s), (t_s, MOBA_BLOCK)],
        [0, MOBA_BLOCK, 0, MOBA_BLOCK], key_major=[True, True, False, False])

    cache_kt = jnp.swapaxes(cache_k, 3, 4)
    cache_vt = jnp.swapaxes(cache_v, 3, 4)

    xp = x_prompt
    xs = x_sample.reshape(1, rows_s, d)
    zero_shift = jnp.zeros((n_p, RW_IN), F32)
    zero_state = jnp.zeros((n_p, H_RWKV, HEAD_DIM, HEAD_DIM), F32)
    outs = {k: [] for k in ("kp", "vp", "ks", "vs", "wp", "ws", "hp", "hs")}
    for l in range(depth):
        mod_p = [m[:, None, :] for m in jnp.split(mods[l, :n_p], 6, axis=-1)]
        mod_s = [jnp.repeat(m, t_s, axis=0)[None] for m in jnp.split(mods[l, n_p:n_c], 6, axis=-1)]
        prm = dict(mu=rw_mu[l], w0=rw_w0[l], w2=rw_w2[l], a0=rw_a0[l], a2=rw_a2[l], g2=rw_g2[l], k_k=rw_k_k[l],
                   k_a=rw_k_a[l], r_k=rw_r_k[l], lnx_g=rw_lnx_g[l], lnx_b=rw_lnx_b[l])
        w_in_bf = w_in[l].astype(BF16)
        w_out_bf = w_out[l].astype(BF16)
        if l % 2 == 0:
            w_router = jnp.zeros((d, LANES), F32)
            wg, wu, wd = (w[l // 2][None].astype(BF16) for w in (ffn_w_gate, ffn_w_up, ffn_w_down))
        else:
            w_router = jnp.pad(moe_w_router[l // 2], ((0, 0), (0, LANES - N_EXPERTS)))
            wg, wu, wd = (w[l // 2].astype(BF16) for w in (moe_w_gate, moe_w_up, moe_w_down))

        pa, pr = _inproj_call(xp, mod_p[1], mod_p[0], w_in_bf, PROMPT_TM)
        y_att = _moba_prompt_call(pa, rel_bias, t_own, t_prev)
        y_rw, s_new, sh_new = _rwkv_mix(pr, zero_shift, zero_state, prm, PROMPT_RW_TC, PROMPT_RW_TL, PROMPT_RW_CHUNK,
                                        PROMPT_RW_UNROLL)
        xp = _outproj_call(y_att, y_rw, xp, mod_p[2], w_out_bf, ln1_g[l], ln1_b[l], PROMPT_TM)
        xp = _ffn_call(xp, mod_p[4], mod_p[3], mod_p[5], w_router, wg, wu, wd, ln2_g[l], ln2_b[l], PROMPT_FF_TM,
                       FF_TILE)
        outs["kp"].append(_to_pages(pa[..., D_ATT:2 * D_ATT]))
        outs["vp"].append(_to_pages(pa[..., 2 * D_ATT:]))
        outs["wp"].append(s_new)
        outs["hp"].append(sh_new)

        pa, pr = _inproj_call(xs, mod_s[1], mod_s[0], w_in_bf, rows_s)
        y_att, k_new, v_new = _moba_sample(pa.reshape(n_s, t_s, ATT_IN), cache_kt, cache_vt, l, page_table, rel_bias,
                                           ts_own, ts_prev)
        y_rw, s_new, sh_new = _rwkv_mix(pr.reshape(n_s, t_s, RW_IN), state_shift[l], state_wkv[l], prm, t_s, t_s, t_s)
        xs = _outproj_call(y_att.reshape(1, rows_s, D_ATT), y_rw.reshape(1, rows_s, D_RWKV), xs, mod_s[2], w_out_bf,
                           ln1_g[l], ln1_b[l], rows_s)
        xs = _ffn_call(xs, mod_s[4], mod_s[3], mod_s[5], w_router, wg, wu, wd, ln2_g[l], ln2_b[l], rows_s, FF_TILE)
        outs["ks"].append(k_new)
        outs["vs"].append(v_new)
        outs["ws"].append(s_new)
        outs["hs"].append(sh_new)

    st = lambda k: jnp.stack(outs[k])
    return (xp, xs.reshape(n_s, t_s, d), st("kp"), st("vp"), st("ks"), st("vs"), st("wp"), st("ws"), st("hp"), st("hs"))
```

```python
import functools
import math

import jax
import jax.numpy as jnp
from jax import lax
from jax.experimental import pallas as pl
from jax.experimental.pallas import tpu as pltpu

F32 = jnp.float32
BF16 = jnp.bfloat16
I32 = jnp.int32
HI = lax.Precision.HIGHEST

D_MODEL = 1024
HEAD_DIM = 64
H_ATT = 8
H_RWKV = 8
D_ATT = H_ATT * HEAD_DIM
D_RWKV = H_RWKV * HEAD_DIM
ATT_IN = 3 * D_ATT
LORA_W = 64
LORA_A = 64
LORA_G = 128
RW_IN = 3 * D_RWKV + LORA_W + LORA_A + LORA_G
MOBA_BLOCK = 256
MOBA_TOPK = 3
PAGE_SIZE = 128
N_BUCKETS = 32
MAX_EXACT = N_BUCKETS // 2
MAX_DISTANCE = 128
D_FF = 2816
N_EXPERTS = 8
DEPTH = 2
ALPHA = (2 * DEPTH) ** 0.25
LN_EPS = 1e-5
LNX_EPS = 64e-5
NEG_INF = -1e30
ATT_SCALE = HEAD_DIM ** -0.5

LANES = 128
HEAD_PAIR = LANES // HEAD_DIM
BF16_ROWS = 16
VMEM_LIMIT = 56 * 1024 * 1024


def _cparams(*sem):
    return pltpu.CompilerParams(dimension_semantics=sem, vmem_limit_bytes=VMEM_LIMIT)


def _sigmoid(x):
    return 1.0 / (1.0 + jnp.exp(-x))


def _dot_nt(a, b, precision=None):
    return lax.dot_general(a, b, (((1,), (1,)), ((), ())), precision=precision, preferred_element_type=F32)


def _dot_tn(a, b, precision=None):
    return lax.dot_general(a, b, (((0,), (0,)), ((), ())), precision=precision, preferred_element_type=F32)


def _dot(a, b, precision=None):
    return jnp.dot(a, b, precision=precision, preferred_element_type=F32)


def _ada_kernel(c_ref, w_ref, b_ref, o_ref):
    c = c_ref[...]
    o_ref[0] = _dot(c * _sigmoid(c), w_ref[0], precision=HI) + b_ref[0]


def _ada_call(c_all, w_ada, b_ada):
    n = c_all.shape[0]
    depth, d, cols = w_ada.shape
    tn = 768
    return pl.pallas_call(
        _ada_kernel,
        grid=(depth, cols // tn),
        in_specs=[pl.BlockSpec((n, d), lambda l, j: (0, 0)),
                  pl.BlockSpec((1, d, tn), lambda l, j: (l, 0, j)),
                  pl.BlockSpec((1, 1, tn), lambda l, j: (l, 0, j))],
        out_specs=pl.BlockSpec((1, n, tn), lambda l, j: (l, 0, j)),
        out_shape=jax.ShapeDtypeStruct((depth, n, cols), F32),
        compiler_params=_cparams("parallel", "parallel"),
        name="ada_mod",
    )(c_all, w_ada, b_ada.reshape(depth, 1, cols))


def _bucket_of(dist):
    d = jnp.maximum(dist, 0)
    log_ratio = jnp.log(jnp.maximum(d, 1).astype(F32) / MAX_EXACT) / math.log(MAX_DISTANCE / MAX_EXACT)
    large = jnp.minimum(MAX_EXACT + (log_ratio * (N_BUCKETS - MAX_EXACT)).astype(I32), N_BUCKETS - 1)
    return jnp.where(d < MAX_EXACT, d, large)


def _bias_kernel(rb_ref, *o_refs, offsets, key_major):
    for o_ref, off, km in zip(o_refs, offsets, key_major):
        n_h, rows, cols = o_ref.shape
        dist = (lax.broadcasted_iota(I32, (rows, cols), 0) - lax.broadcasted_iota(I32, (rows, cols), 1))
        dist = (-dist if km else dist) + off
        bucket = _bucket_of(dist)
        for h in range(n_h):
            acc = jnp.zeros((rows, cols), F32)
            for b in range(N_BUCKETS):
                acc = jnp.where(bucket == b, rb_ref[b, h], acc)
            o_ref[h] = jnp.where(dist >= 0, acc, NEG_INF)


def _bias_call(rel_bias, shapes, offsets, key_major=None):
    n_h = rel_bias.shape[1]
    key_major = tuple(key_major) if key_major is not None else (False,) * len(shapes)
    return pl.pallas_call(
        functools.partial(_bias_kernel, offsets=tuple(offsets), key_major=key_major),
        in_specs=[pl.BlockSpec(memory_space=pltpu.SMEM)],
        out_specs=[pl.BlockSpec(memory_space=pltpu.VMEM) for _ in shapes],
        out_shape=[jax.ShapeDtypeStruct((n_h,) + s, F32) for s in shapes],
        name="rel_bias_tables",
    )(rel_bias)


def _inproj_kernel(x_ref, sc_ref, sh_ref, w_ref, oa_ref, or_ref, *, chunk):
    u = (x_ref[0] * (1.0 + sc_ref[0]) + sh_ref[0]).astype(BF16)
    att = oa_ref.shape[-1]
    for c0 in range(0, w_ref.shape[1], chunk):
        res = _dot(u, w_ref[:, c0:c0 + chunk])
        if c0 < att:
            oa_ref[0, :, c0:c0 + chunk] = res
        else:
            or_ref[0, :, c0 - att:c0 - att + chunk] = res


def _mod_spec(mod, tm):
    if mod.shape[1] == 1:
        return pl.BlockSpec((1, 1, mod.shape[2]), lambda n, i: (n, 0, 0))
    return pl.BlockSpec((1, tm, mod.shape[2]), lambda n, i: (n, i, 0))


def _inproj_call(x, sc, sh, w_bf, tm):
    n, l, d = x.shape
    cols = w_bf.shape[1]
    return pl.pallas_call(
        functools.partial(_inproj_kernel, chunk=256),
        grid=(n, l // tm),
        in_specs=[pl.BlockSpec((1, tm, d), lambda n, i: (n, i, 0)),
                  _mod_spec(sc, tm), _mod_spec(sh, tm),
                  pl.BlockSpec((d, cols), lambda n, i: (0, 0))],
        out_specs=[pl.BlockSpec((1, tm, ATT_IN), lambda n, i: (n, i, 0)),
                   pl.BlockSpec((1, tm, RW_IN), lambda n, i: (n, i, 0))],
        out_shape=[jax.ShapeDtypeStruct((n, l, ATT_IN), F32),
                   jax.ShapeDtypeStruct((n, l, RW_IN), F32)],
        compiler_params=_cparams("parallel", "parallel"),
        name="in_proj",
    )(x, sc, sh, w_bf)


def _moba_prompt_kernel(rb_ref, q_ref, k_ref, v_ref, town_ref, tprev_ref, o_ref, kb_ref, vb_ref, km_ref,
                        *, n_blk):
    hp = pl.program_id(1)
    i = pl.program_id(2)
    blk = MOBA_BLOCK

    @pl.when(i == 0)
    def _():
        k = k_ref[0]
        kb_ref[...] = k.astype(BF16)
        km_ref[...] = jnp.sum(k.reshape(n_blk, blk, LANES), axis=1) * (1.0 / blk)
        for c in range(n_blk):
            vt = v_ref[0, c * blk:(c + 1) * blk, :].T.astype(BF16)
            for h2 in range(HEAD_PAIR):
                vb_ref[h2, :HEAD_DIM, c * blk:(c + 1) * blk] = vt[h2 * HEAD_DIM:(h2 + 1) * HEAD_DIM]
                vb_ref[h2, HEAD_DIM:, c * blk:(c + 1) * blk] = jnp.ones((BF16_ROWS, blk), BF16)

    qt = q_ref[0].T
    dim_row = lax.broadcasted_iota(I32, qt.shape, 0)
    own0 = pl.multiple_of(i * blk, blk)
    heads = range(HEAD_PAIR)
    bidx = lax.broadcasted_iota(I32, (n_blk, blk), 0)
    past = bidx < i
    qt_h = [jnp.where((dim_row // HEAD_DIM) == h2, qt, 0.0) for h2 in heads]
    qb = [(z * ATT_SCALE).astype(BF16) for z in qt_h]
    sel = []
    for h2 in heads:
        g = jnp.where(past, _dot(km_ref[...], qt_h[h2], precision=HI), NEG_INF)
        rank = jnp.zeros(g.shape, I32)
        for jp in range(n_blk):
            gj = g[jp:jp + 1, :]
            rank = rank + ((gj > g) | ((gj == g) & (jp < bidx))).astype(I32)
        sel.append(jnp.where((rank < MOBA_TOPK) & past, 1.0, 0.0))
    b_far = [rb_ref[N_BUCKETS - 1, hp * HEAD_PAIR + h2] for h2 in heads]

    def attend(blocks, carry):
        s = []
        for r0, tables, rows in blocks:
            kb = kb_ref[pl.ds(r0, blk), :]
            s_b = [_dot(kb, qb[h2]) for h2 in heads]
            if tables is not None:
                s_b = [s_b[h2] + tables[h2] for h2 in heads]
            if rows is not None:
                s_b = [s_b[h2] + rows[h2] for h2 in heads]
            s.append(s_b)
        out = []
        for h2 in heads:
            m_new = functools.reduce(jnp.maximum, [jnp.max(s_b[h2], axis=0, keepdims=True) for s_b in s])
            if carry is not None:
                m, acc = carry[h2]
                m_new = jnp.maximum(m, m_new)
                acc = jnp.exp(m - m_new) * acc
            else:
                acc = 0.0
            for (r0, _, _), s_b in zip(blocks, s):
                p = jnp.exp(s_b[h2] - m_new).astype(BF16)
                acc = acc + _dot(vb_ref[h2, :, pl.ds(r0, blk)], p)
            out.append((m_new, acc))
        return tuple(out)

    def picked_row(j, h2, value):
        hit = jnp.sum(jnp.where(bidx == j, sel[h2], 0.0), axis=0, keepdims=True) > 0.5
        return jnp.where(hit, value, NEG_INF)

    def far_block(j):
        return pl.multiple_of(j * blk, blk), None, [picked_row(j, h2, b_far[h2]) for h2 in heads]

    def near_block(j):
        return pl.multiple_of(j * blk, blk), [tprev_ref[h2] for h2 in heads], [picked_row(j, h2, 0.0) for h2 in heads]

    carry = attend([(own0, [town_ref[h2] for h2 in heads], None)], None)
    n_far = jnp.maximum(i - 1, 0)
    grp = PROMPT_KEY_GROUP
    carry = lax.fori_loop(0, n_far // grp,
                          lambda j, c: attend([far_block(grp * j + u) for u in range(grp)], c), carry)
    carry = lax.fori_loop(grp * (n_far // grp), n_far, lambda j, c: attend([far_block(j)], c), carry)
    carry = lax.fori_loop(n_far, i, lambda j, c: attend([near_block(j)], c), carry)
    o_ref[0] = jnp.concatenate([acc[:HEAD_DIM] / acc[HEAD_DIM:HEAD_DIM + 1] for _, acc in carry], axis=0).T


def _moba_prompt_call(p_att, rel_bias, t_own, t_prev):
    n, l, _ = p_att.shape
    n_blk = l // MOBA_BLOCK
    n_hp = H_ATT // HEAD_PAIR
    tbl = pl.BlockSpec((HEAD_PAIR, MOBA_BLOCK, MOBA_BLOCK), lambda n, h, i: (h, 0, 0))
    return pl.pallas_call(
        functools.partial(_moba_prompt_kernel, n_blk=n_blk),
        grid=(n, n_hp, n_blk),
        in_specs=[pl.BlockSpec(memory_space=pltpu.SMEM),
                  pl.BlockSpec((1, MOBA_BLOCK, LANES), lambda n, h, i: (n, i, h)),
                  pl.BlockSpec((1, l, LANES), lambda n, h, i: (n, 0, n_hp + h)),
                  pl.BlockSpec((1, l, LANES), lambda n, h, i: (n, 0, 2 * n_hp + h)),
                  tbl, tbl],
        out_specs=pl.BlockSpec((1, MOBA_BLOCK, LANES), lambda n, h, i: (n, i, h)),
        out_shape=jax.ShapeDtypeStruct((n, l, D_ATT), F32),
        scratch_shapes=[pltpu.VMEM((l, LANES), BF16), pltpu.VMEM((HEAD_PAIR, HEAD_DIM + BF16_ROWS, l), BF16),
                        pltpu.VMEM((n_blk, LANES), F32)],
        compiler_params=_cparams("parallel", "parallel", "arbitrary"),
        name="moba_prompt",
    )(rel_bias, p_att, p_att, p_att, t_own, t_prev)


def _softplus(x):
    return jnp.maximum(x, 0.0) + jnp.log(1.0 + jnp.exp(-jnp.abs(x)))


def _rwkv_pre_kernel(p_ref, prev_ref, sh0_ref, mu_ref, w0_ref, w2_ref, a0_ref, a2_ref, g2_ref, kkw_ref, kaw_ref,
                     r_o, lw_o, k_o, v_o, kk_o, b_o, g_o):
    i = pl.program_id(1)
    p = p_ref[0]
    prev_row = jnp.where(i == 0, sh0_ref[0], prev_ref[0][7:8])
    row = lax.broadcasted_iota(I32, p.shape, 0)
    p_prev = jnp.where(row == 0, prev_row, pltpu.roll(p, 1, 0))
    xm = p + (p_prev - p) * mu_ref[...]
    o = 3 * D_RWKV
    r = xm[:, :D_RWKV]
    k = xm[:, D_RWKV:2 * D_RWKV]
    v = xm[:, 2 * D_RWKV:o]
    xw = xm[:, o:o + LORA_W]
    xa = xm[:, o + LORA_W:o + LORA_W + LORA_A]
    xg = xm[:, o + LORA_W + LORA_A:]
    w_log = -_softplus(-(w0_ref[...] + _dot(jnp.tanh(xw), w2_ref[...], precision=HI))) - 0.5
    a = _sigmoid(a0_ref[...] + _dot(xa, a2_ref[...], precision=HI))
    g = _dot(_sigmoid(xg), g2_ref[...], precision=HI)
    kk = k * kkw_ref[...]
    same_head = (lax.broadcasted_iota(I32, (D_RWKV, D_RWKV), 0) // HEAD_DIM
                 == lax.broadcasted_iota(I32, (D_RWKV, D_RWKV), 1) // HEAD_DIM)
    ss = _dot(kk * kk, jnp.where(same_head, 1.0, 0.0), precision=HI)
    kk = kk / jnp.maximum(jnp.sqrt(ss), 1e-12)
    r_o[0] = r
    lw_o[0] = -jnp.exp(w_log)
    k_o[0] = k * (1.0 + (a - 1.0) * kaw_ref[...])
    v_o[0] = v
    kk_o[0] = kk
    b_o[0] = kk * a
    g_o[0] = g


def _rwkv_pre_call(p_rw, shift0, mu, w0, w2, a0, a2, g2, k_k, k_a, tc):
    n, l, _ = p_rw.shape
    row = lambda z: z.reshape(1, -1)
    full = lambda z: pl.BlockSpec(z.shape, lambda n, i: (0,) * z.ndim)
    params = [row(mu), row(w0), w2, row(a0), a2, g2, row(k_k), row(k_a)]
    out_spec = pl.BlockSpec((1, tc, D_RWKV), lambda n, i: (n, i, 0))
    return pl.pallas_call(
        _rwkv_pre_kernel,
        grid=(n, l // tc),
        in_specs=[pl.BlockSpec((1, tc, RW_IN), lambda n, i: (n, i, 0)),
                  pl.BlockSpec((1, 8, RW_IN), lambda n, i: (n, jnp.maximum(i * (tc // 8) - 1, 0), 0)),
                  pl.BlockSpec((1, 1, RW_IN), lambda n, i: (n, 0, 0))] + [full(z) for z in params],
        out_specs=[out_spec] * 7,
        out_shape=[jax.ShapeDtypeStruct((n, l, D_RWKV), F32)] * 7,
        compiler_params=_cparams("parallel", "parallel"),
        name="rwkv_pre",
    )(p_rw, p_rw, shift0.reshape(n, 1, RW_IN), *params)


def _chunk_cumsum(x):
    rows = x.shape[0]
    row = lax.broadcasted_iota(I32, x.shape, 0)
    s = 1
    while s < rows:
        x = x + jnp.where(row >= s, pltpu.roll(x, s, 0), 0.0)
        s *= 2
    return x


def _rwkv_scan_kernel(r_ref, lw_ref, k_ref, v_ref, kk_ref, b_ref, g_ref, s0_ref, rk_ref, lg_ref, lb_ref,
                      y_ref, sf_ref, st_ref, yraw_ref, *, chunk, unroll, mxu):
    i = pl.program_id(2)
    C = chunk
    tl = r_ref.shape[1]
    n_dbl = max(C.bit_length() - 2, 0)

    @pl.when(i == 0)
    def _():
        st_ref[...] = s0_ref[0, 0]

    lane = lax.broadcasted_iota(I32, (1, LANES), 1)
    head0 = lane < HEAD_DIM
    ti = lax.broadcasted_iota(I32, (C, C), 0)
    si = lax.broadcasted_iota(I32, (C, C), 1)
    strict = si < ti
    incl = si <= ti
    eye_c = jnp.where(si == ti, 1.0, 0.0)
    rj = lax.broadcasted_iota(I32, (LANES, LANES), 0)
    cj = lax.broadcasted_iota(I32, (LANES, LANES), 1)
    same_head = (rj // HEAD_DIM) == (cj // HEAD_DIM)
    eye_l = rj == cj
    mm = lambda a, b: _dot(a.astype(mxu), b.astype(mxu))
    mm_nt = lambda a, b: _dot_nt(a.astype(mxu), b.astype(mxu))
    mm_tn = lambda a, b: _dot_tn(a.astype(mxu), b.astype(mxu))

    def chunk_operands(rows):
        lw = lw_ref[0, rows, :]
        r = r_ref[0, rows, :]
        k = k_ref[0, rows, :]
        v = v_ref[0, rows, :]
        kk = kk_ref[0, rows, :]
        b = b_ref[0, rows, :]
        cum = _chunk_cumsum(lw)
        c_end = cum[C - 1:C, :]
        e_neg = jnp.exp(-cum)
        e_end = jnp.exp(c_end - cum)
        al = -kk * jnp.exp(cum - lw)
        rt = r * jnp.exp(cum)
        return dict(v=v, al=al, rt=rt, bt=b * e_neg, kt=k * e_neg, bt2=b * e_end, kt2=k * e_end,
                    lhs=jnp.concatenate([al, rt], axis=0), c_end=c_end)

    def group_terms(rows_list):
        ops = [chunk_operands(rw) for rw in rows_list]
        ch = [(u, h) for u in range(len(ops)) for h in range(HEAD_PAIR)]
        mine = [head0, jnp.logical_not(head0)]
        lhs = [jnp.where(mine[h], ops[u]["lhs"], 0.0) for u, h in ch]
        ab = [mm_nt(lhs[n], ops[u]["bt"]) for n, (u, h) in enumerate(ch)]
        ak = [mm_nt(lhs[n], ops[u]["kt"]) for n, (u, h) in enumerate(ch)]
        m_ab = [jnp.where(strict, z[:C], 0.0) for z in ab]
        m_ak = [jnp.where(strict, z[:C], 0.0) for z in ak]
        b_br = [jnp.where(incl, z[C:], 0.0) for z in ab]
        b_kr = [jnp.where(incl, z[C:], 0.0) for z in ak]
        t_inv = [eye_c + z for z in m_ab]
        pk = m_ab
        for _ in range(n_dbl):
            pk = [mm(z, z) for z in pk]
            t_inv = [t + mm(z, t) for z, t in zip(pk, t_inv)]
        mv = [mm(m_ak[n], ops[u]["v"]) for n, (u, h) in enumerate(ch)]
        aw = [mm(t_inv[n], jnp.concatenate([ops[u]["al"], mv[n]], axis=1)) for n, (u, h) in enumerate(ch)]
        baw = [mm(b_br[n], aw[n]) for n in range(len(ch))]
        bkv = [mm(b_kr[n], ops[u]["v"]) for n, (u, h) in enumerate(ch)]
        terms = []
        for u, o in enumerate(ops):
            n0, n1 = u * HEAD_PAIR, u * HEAD_PAIR + 1
            pick = lambda f: jnp.where(head0, f(n0), f(n1))
            a2 = pick(lambda n: aw[n][:, :LANES])
            w = pick(lambda n: aw[n][:, LANES:])
            r2 = o["rt"] + pick(lambda n: baw[n][:, :LANES])
            y0 = pick(lambda n: baw[n][:, LANES:] + bkv[n])
            tc_t = jnp.where(same_head, mm_tn(o["bt2"], a2), 0.0)
            gc_t = jnp.where(same_head, mm_tn(jnp.concatenate([o["bt2"], o["kt2"]], axis=0),
                                              jnp.concatenate([w, o["v"]], axis=0)), 0.0)
            dec_col = jnp.sum(jnp.where(eye_l, jnp.exp(o["c_end"]), 0.0), axis=1, keepdims=True)
            terms.append((r2, y0, tc_t, gc_t, dec_col))
        return terms

    def group_body(c, carry):
        rows = [pl.ds(pl.multiple_of((c * unroll + u) * C, C), C) for u in range(unroll)]
        terms = group_terms(rows)
        st = st_ref[...]
        for rw, (r2, y0, tc_t, gc_t, dec_col) in zip(rows, terms):
            yraw_ref[rw, :] = mm(r2, st) + y0
            st = dec_col * st + mm(tc_t, st) + gc_t
        st_ref[...] = st
        return carry

    lax.fori_loop(0, tl // (C * unroll), group_body, 0)
    sf_ref[0, 0] = st_ref[...]

    def head_sum(z):
        s0 = jnp.sum(jnp.where(head0, z, 0.0), axis=1, keepdims=True)
        s1 = jnp.sum(jnp.where(head0, 0.0, z), axis=1, keepdims=True)
        return jnp.where(head0, s0, s1)

    y = yraw_ref[...]
    d = y - head_sum(y) * (1.0 / HEAD_DIM)
    var = head_sum(d * d) * (1.0 / HEAD_DIM)
    yn = d * lax.rsqrt(var + LNX_EPS) * lg_ref[...] + lb_ref[...]
    bonus = head_sum(r_ref[0] * k_ref[0] * rk_ref[...]) * v_ref[0]
    y_ref[0] = (yn + bonus) * g_ref[0]


def _rwkv_scan_call(ops, st0, r_k, lnx_g, lnx_b, tl, chunk, unroll):
    n, l, _ = ops[0].shape
    n_hp = H_RWKV // HEAD_PAIR
    mxu = BF16 if chunk >= 16 else F32
    tok = pl.BlockSpec((1, tl, LANES), lambda n, h, i: (n, i, h))
    par = pl.BlockSpec((1, LANES), lambda n, h, i: (0, h))
    st_spec = pl.BlockSpec((1, 1, LANES, LANES), lambda n, h, i: (n, h, 0, 0))
    return pl.pallas_call(
        functools.partial(_rwkv_scan_kernel, chunk=chunk, unroll=unroll, mxu=mxu),
        grid=(n, n_hp, l // tl),
        in_specs=[tok] * 7 + [st_spec, par, par, par],
        out_specs=[tok, st_spec],
        out_shape=[jax.ShapeDtypeStruct((n, l, D_RWKV), F32),
                   jax.ShapeDtypeStruct((n, n_hp, LANES, LANES), F32)],
        scratch_shapes=[pltpu.VMEM((LANES, LANES), F32), pltpu.VMEM((tl, LANES), F32)],
        compiler_params=_cparams("parallel", "parallel", "arbitrary"),
        name="rwkv_scan",
    )(*ops, st0, r_k.reshape(1, D_RWKV), lnx_g.reshape(1, D_RWKV), lnx_b.reshape(1, D_RWKV))


def _state_to_pairs(s):
    n, h = s.shape[:2]
    st = jnp.swapaxes(s, -1, -2).reshape(n, h // HEAD_PAIR, HEAD_PAIR, HEAD_DIM, HEAD_DIM)
    z = jnp.zeros_like(st[:, :, 0])
    top = jnp.concatenate([st[:, :, 0], z], axis=-1)
    bot = jnp.concatenate([z, st[:, :, 1]], axis=-1)
    return jnp.concatenate([top, bot], axis=-2)


def _pairs_to_state(sp):
    n, n_hp = sp.shape[:2]
    d0 = sp[:, :, :HEAD_DIM, :HEAD_DIM]
    d1 = sp[:, :, HEAD_DIM:, HEAD_DIM:]
    st = jnp.stack([d0, d1], axis=2).reshape(n, n_hp * HEAD_PAIR, HEAD_DIM, HEAD_DIM)
    return jnp.swapaxes(st, -1, -2)


def _rwkv_mix(p_rw, shift0, s0, prm, tc, tl, chunk, unroll=1):
    ops = _rwkv_pre_call(p_rw, shift0, prm["mu"], prm["w0"], prm["w2"], prm["a0"], prm["a2"], prm["g2"],
                         prm["k_k"], prm["k_a"], tc)
    y, sp = _rwkv_scan_call(ops, _state_to_pairs(s0), prm["r_k"], prm["lnx_g"], prm["lnx_b"], tl, chunk, unroll)
    return y, _pairs_to_state(sp), p_rw[:, -1]


def _layer_norm(z, g, b):
    mu = jnp.mean(z, axis=-1, keepdims=True)
    d = z - mu
    var = jnp.mean(d * d, axis=-1, keepdims=True)
    return d * lax.rsqrt(var + LN_EPS) * g + b


def _outproj_kernel(ya_ref, yr_ref, x_ref, gt_ref, w_ref, g_ref, b_ref, o_ref):
    o = (_dot(ya_ref[0].astype(BF16), w_ref[:D_ATT, :]) + _dot(yr_ref[0].astype(BF16), w_ref[D_ATT:, :]))
    o_ref[0] = _layer_norm(ALPHA * x_ref[0] + (1.0 + gt_ref[0]) * o, g_ref[...], b_ref[...])


def _outproj_call(y_att, y_rw, x, gt, w_bf, ln_g, ln_b, tm):
    n, l, d = x.shape
    row = pl.BlockSpec((1, d), lambda n, i: (0, 0))
    return pl.pallas_call(
        _outproj_kernel,
        grid=(n, l // tm),
        in_specs=[pl.BlockSpec((1, tm, D_ATT), lambda n, i: (n, i, 0)),
                  pl.BlockSpec((1, tm, D_RWKV), lambda n, i: (n, i, 0)),
                  pl.BlockSpec((1, tm, d), lambda n, i: (n, i, 0)),
                  _mod_spec(gt, tm),
                  pl.BlockSpec(w_bf.shape, lambda n, i: (0, 0)), row, row],
        out_specs=pl.BlockSpec((1, tm, d), lambda n, i: (n, i, 0)),
        out_shape=jax.ShapeDtypeStruct((n, l, d), F32),
        compiler_params=_cparams("parallel", "parallel"),
        name="out_proj_ln",
    )(y_att, y_rw, x, gt, w_bf, ln_g.reshape(1, d), ln_b.reshape(1, d))


def _ffn_kernel(x_ref, sc_ref, sh_ref, gt_ref, wr_ref, wg_ref, wu_ref, wd_ref, g_ref, b_ref, o_ref,
                u_ref, gate_ref, *, n_experts):
    e = pl.program_id(2)
    f = pl.program_id(3)
    last = (e == pl.num_programs(2) - 1) & (f == pl.num_programs(3) - 1)

    @pl.when((e == 0) & (f == 0))
    def _():
        u = x_ref[0] * (1.0 + sc_ref[0]) + sh_ref[0]
        u_ref[...] = u.astype(BF16)
        o_ref[...] = jnp.zeros_like(o_ref)
        if n_experts > 1:
            logits = _dot(u, wr_ref[...], precision=HI)
            lane = lax.broadcasted_iota(I32, logits.shape, 1)
            logits = jnp.where(lane < n_experts, logits, NEG_INF)
            v1 = jnp.max(logits, axis=1, keepdims=True)
            i1 = jnp.min(jnp.where(logits == v1, lane, LANES), axis=1, keepdims=True)
            rest = jnp.where(lane == i1, NEG_INF, logits)
            v2 = jnp.max(rest, axis=1, keepdims=True)
            i2 = jnp.min(jnp.where(rest == v2, lane, LANES), axis=1, keepdims=True)
            w2 = 1.0 / (1.0 + jnp.exp(v1 - v2))
            gate_ref[...] = jnp.where(lane == i1, 1.0 - w2, 0.0) + jnp.where(lane == i2, w2, 0.0)

    u = u_ref[...]
    hg = _dot(u, wg_ref[0])
    hu = _dot(u, wu_ref[0])
    h = hg * _sigmoid(hg) * hu
    if n_experts > 1:
        lane = lax.broadcasted_iota(I32, gate_ref.shape, 1)
        h = h * jnp.sum(jnp.where(lane == e, gate_ref[...], 0.0), axis=1, keepdims=True)
    o_ref[0] += _dot(h.astype(BF16), wd_ref[0])

    @pl.when(last)
    def _():
        o_ref[0] = _layer_norm(ALPHA * x_ref[0] + (1.0 + gt_ref[0]) * o_ref[0], g_ref[...], b_ref[...])


def _ffn_call(x, sc, sh, gt, w_router, wg_bf, wu_bf, wd_bf, ln_g, ln_b, tm, tf):
    n, l, d = x.shape
    n_e, _, d_ff = wg_bf.shape
    row = pl.BlockSpec((1, d), lambda n, i, e, f: (0, 0))
    mod = lambda m: (pl.BlockSpec((1, 1, d), lambda n, i, e, f: (n, 0, 0)) if m.shape[1] == 1
                     else pl.BlockSpec((1, tm, d), lambda n, i, e, f: (n, i, 0)))
    return pl.pallas_call(
        functools.partial(_ffn_kernel, n_experts=n_e),
        grid=(n, l // tm, n_e, d_ff // tf),
        in_specs=[pl.BlockSpec((1, tm, d), lambda n, i, e, f: (n, i, 0), pipeline_mode=pl.Buffered(1)),
                  mod(sc), mod(sh), mod(gt),
                  pl.BlockSpec(w_router.shape, lambda n, i, e, f: (0, 0)),
                  pl.BlockSpec((1, d, tf), lambda n, i, e, f: (e, 0, f)),
                  pl.BlockSpec((1, d, tf), lambda n, i, e, f: (e, 0, f)),
                  pl.BlockSpec((1, tf, d), lambda n, i, e, f: (e, f, 0)),
                  row, row],
        out_specs=pl.BlockSpec((1, tm, d), lambda n, i, e, f: (n, i, 0)),
        out_shape=jax.ShapeDtypeStruct((n, l, d), F32),
        scratch_shapes=[pltpu.VMEM((tm, d), BF16), pltpu.VMEM((tm, LANES), F32)],
        compiler_params=_cparams("parallel", "parallel", "arbitrary", "arbitrary"),
        name="ffn_ln" if n_e == 1 else "moe_ln",
    )(x, sc, sh, gt, w_router, wg_bf, wu_bf, wd_bf, ln_g.reshape(1, d), ln_b.reshape(1, d))


PAGES_PER_BLOCK = MOBA_BLOCK // PAGE_SIZE
KMEAN_PAGES = 32


def _kmean_kernel(pt_ref, *refs):
    o_ref = refs[-1]
    g = pl.program_id(1)

    @pl.when(g == 0)
    def _():
        o_ref[...] = jnp.zeros_like(o_ref)

    lane = lax.broadcasted_iota(I32, o_ref.shape[1:], 1)
    out = o_ref[0]
    for blk in range(KMEAN_PAGES // PAGES_PER_BLOCK):
        tot = functools.reduce(lambda a, c: a + c,
                               [refs[blk * PAGES_PER_BLOCK + r][...] for r in range(PAGES_PER_BLOCK)])
        col = jnp.sum(tot, axis=-1, keepdims=True).reshape(-1, 1) * (1.0 / MOBA_BLOCK)
        out = jnp.where(lane == g * (KMEAN_PAGES // PAGES_PER_BLOCK) + blk, col, out)
    o_ref[0] = out


def _kmean_call(cache_kt, layer, page_table, n_full):
    n_seq = page_table.shape[0]
    _, _, n_h, hd, ps = cache_kt.shape
    page = lambda r: pl.BlockSpec((None, None, n_h, hd, ps),
                                  lambda b, g, pt: (layer, pt[b, g * KMEAN_PAGES + r], 0, 0, 0))
    blocks_per_step = KMEAN_PAGES // PAGES_PER_BLOCK
    return pl.pallas_call(
        _kmean_kernel,
        grid_spec=pltpu.PrefetchScalarGridSpec(
            num_scalar_prefetch=1,
            grid=(n_seq, n_full // blocks_per_step),
            in_specs=[page(r) for r in range(KMEAN_PAGES)],
            out_specs=pl.BlockSpec((1, n_h * hd, n_full), lambda b, g, pt: (b, 0, 0))),
        out_shape=jax.ShapeDtypeStruct((n_seq, n_h * hd, n_full), F32),
        compiler_params=_cparams("parallel", "arbitrary"),
        name="moba_kmean",
    )(page_table, *([cache_kt] * KMEAN_PAGES))


def _top3_kernel(q_ref, km_ref, o_ref):
    n_h = q_ref.shape[1]
    for h in range(n_h):
        g = _dot(q_ref[0, h], km_ref[0, h], precision=HI)
        lane = lax.broadcasted_iota(I32, g.shape, 1)
        out_lane = lax.broadcasted_iota(I32, (g.shape[0], LANES), 1)
        out = jnp.zeros((g.shape[0], LANES), I32)
        for s in range(MOBA_TOPK):
            mx = jnp.max(g, axis=1, keepdims=True)
            idx = jnp.min(jnp.where(g == mx, lane, g.shape[1]), axis=1, keepdims=True)
            out = jnp.where(out_lane == s, idx, out)
            g = jnp.where(lane == idx, -jnp.inf, g)
        o_ref[0, h] = out


def _top3_call(q, k_mean_t):
    n_seq, n_h, t, hd = q.shape
    n_full = k_mean_t.shape[3]
    return pl.pallas_call(
        _top3_kernel,
        grid=(n_seq,),
        in_specs=[pl.BlockSpec((1, n_h, t, hd), lambda b: (b, 0, 0, 0)),
                  pl.BlockSpec((1, n_h, hd, n_full), lambda b: (b, 0, 0, 0))],
        out_specs=pl.BlockSpec((1, n_h, t, LANES), lambda b: (b, 0, 0, 0)),
        out_shape=jax.ShapeDtypeStruct((n_seq, n_h, t, LANES), I32),
        compiler_params=_cparams("parallel"),
        name="moba_top3",
    )(q, k_mean_t)


def _moba_sample_kernel(pt_ref, sel_ref, rb_ref, qt_ref, knt_ref, vnt_ref, town_ref, tprev_ref, ck_hbm, cv_hbm,
                        o_ref, kbuf, vbuf, sems, *, layer, n_seq, n_h, n_t, n_full):
    n_pg = MOBA_TOPK * PAGES_PER_BLOCK
    n_copies = n_t * n_pg
    b = pl.program_id(0)
    h = pl.program_id(1)
    step = b * n_h + h
    slot = step % 2

    def page_copies(step_f, slot_f):
        b_f = step_f // n_h
        h_f = step_f % n_h
        copies = []
        for t in range(n_t):
            for s in range(MOBA_TOPK):
                blk_id = sel_ref[(step_f * n_t + t) * MOBA_TOPK + s]
                for r in range(PAGES_PER_BLOCK):
                    pg = pt_ref[b_f, blk_id * PAGES_PER_BLOCK + r]
                    j = (t * MOBA_TOPK + s) * PAGES_PER_BLOCK + r
                    copies.append(pltpu.make_async_copy(ck_hbm.at[layer, pg, h_f], kbuf.at[slot_f, j], sems.at[0, slot_f]))
                    copies.append(pltpu.make_async_copy(cv_hbm.at[layer, pg, h_f], vbuf.at[slot_f, j], sems.at[1, slot_f]))
        return copies

    @pl.when(step == 0)
    def _():
        for c in page_copies(step, slot):
            c.start()

    @pl.when(step + 1 < n_seq * n_h)
    def _():
        for c in page_copies(step + 1, 1 - slot):
            c.start()

    for j in range(n_copies):
        pltpu.make_async_copy(ck_hbm.at[layer, 0, 0], kbuf.at[slot, j], sems.at[0, slot]).wait()
        pltpu.make_async_copy(cv_hbm.at[layer, 0, 0], vbuf.at[slot, j], sems.at[1, slot]).wait()

    k_refs = [kbuf.at[slot, j] for j in range(n_copies)]
    v_refs = [vbuf.at[slot, j] for j in range(n_copies)]
    qt = qt_ref[0, 0] * ATT_SCALE
    knt = knt_ref[0, 0]
    vnt = vnt_ref[0, 0]
    b_far = rb_ref[N_BUCKETS - 1, h]
    out_lane = lax.broadcasted_iota(I32, (HEAD_DIM, n_t), 1)
    out = jnp.zeros((HEAD_DIM, n_t), F32)
    for t in range(n_t):
        q_col = qt[:, t:t + 1]
        base = ((b * n_h + h) * n_t + t) * MOBA_TOPK
        scores = [jnp.sum(knt * q_col, axis=0, keepdims=True) + town_ref[0, t:t + 1, :]]
        for s in range(MOBA_TOPK):
            last = sel_ref[base + s] == n_full - 1
            for r in range(PAGES_PER_BLOCK):
                sc = jnp.sum(k_refs[(t * MOBA_TOPK + s) * PAGES_PER_BLOCK + r][...] * q_col, axis=0, keepdims=True)
                bias = jnp.where(last, tprev_ref[0, t:t + 1, r * PAGE_SIZE:(r + 1) * PAGE_SIZE], b_far)
                scores.append(sc + bias)
        m = functools.reduce(jnp.maximum, [jnp.max(sc, axis=1, keepdims=True) for sc in scores])
        probs = [jnp.exp(sc - m) for sc in scores]
        denom = functools.reduce(lambda a, c: a + c, [jnp.sum(p, axis=1, keepdims=True) for p in probs])
        acc = functools.reduce(lambda a, c: a + c,
                               [probs[1 + j] * v_refs[t * n_pg + j][...] for j in range(n_pg)])
        col = (jnp.sum(acc, axis=1, keepdims=True) + jnp.sum(probs[0] * vnt, axis=1, keepdims=True)) / denom
        out = jnp.where(out_lane == t, col, out)
    o_ref[0, 0] = out


def _moba_sample_call(qt, knt, vnt, cache_kt, cache_vt, layer, page_table, sel_flat, rel_bias, ts_own, ts_prev, n_full):
    n_seq, n_h, hd, n_t = qt.shape
    ps = cache_kt.shape[4]
    tok = pl.BlockSpec((1, 1, hd, n_t), lambda b, h, pt, sel: (b, h, 0, 0))
    n_copies = n_t * MOBA_TOPK * PAGES_PER_BLOCK
    return pl.pallas_call(
        functools.partial(_moba_sample_kernel, layer=layer, n_seq=n_seq, n_h=n_h, n_t=n_t, n_full=n_full),
        grid_spec=pltpu.PrefetchScalarGridSpec(
            num_scalar_prefetch=2,
            grid=(n_seq, n_h),
            in_specs=[pl.BlockSpec(memory_space=pltpu.SMEM), tok, tok, tok,
                      pl.BlockSpec((1, n_t, n_t), lambda b, h, pt, sel: (h, 0, 0)),
                      pl.BlockSpec((1, n_t, MOBA_BLOCK), lambda b, h, pt, sel: (h, 0, 0)),
                      pl.BlockSpec(memory_space=pl.ANY), pl.BlockSpec(memory_space=pl.ANY)],
            out_specs=tok,
            scratch_shapes=[pltpu.VMEM((2, n_copies, hd, ps), F32), pltpu.VMEM((2, n_copies, hd, ps), F32),
                            pltpu.SemaphoreType.DMA((2, 2))]),
        out_shape=jax.ShapeDtypeStruct((n_seq, n_h, hd, n_t), F32),
        compiler_params=_cparams("arbitrary", "arbitrary"),
        name="moba_sample",
    )(page_table, sel_flat, rel_bias, qt, knt, vnt, ts_own, ts_prev, cache_kt, cache_vt)


def _moba_sample(p_att, cache_kt, cache_vt, layer, page_table, rel_bias, ts_own, ts_prev):
    n_seq, n_t, _ = p_att.shape
    n_full = page_table.shape[1] // PAGES_PER_BLOCK
    qkv = p_att.reshape(n_seq, n_t, 3, H_ATT, HEAD_DIM).transpose(2, 0, 3, 1, 4)
    q, k_new, v_new = qkv[0], qkv[1], qkv[2]
    k_mean_t = _kmean_call(cache_kt, layer, page_table, n_full).reshape(n_seq, H_ATT, HEAD_DIM, n_full)
    picks = _top3_call(q, k_mean_t)
    sel_flat = picks[..., :MOBA_TOPK].reshape(-1)
    tr = lambda z: jnp.swapaxes(z, 2, 3)
    out_t = _moba_sample_call(tr(q), tr(k_new), tr(v_new), cache_kt, cache_vt, layer, page_table, sel_flat, rel_bias,
                              ts_own, ts_prev, n_full)
    return out_t.transpose(0, 3, 1, 2).reshape(n_seq, n_t, D_ATT), k_new, v_new


PROMPT_TM = 512
PROMPT_KEY_GROUP = 4
PROMPT_FF_TM = 2048
PROMPT_RW_TC = 256
PROMPT_RW_TL = 512
PROMPT_RW_CHUNK = 32
PROMPT_RW_UNROLL = 8
FF_TILE = 256


def _to_pages(z):
    b, s, _ = z.shape
    return z.reshape(b, s // PAGE_SIZE, PAGE_SIZE, H_ATT, HEAD_DIM).transpose(0, 1, 3, 2, 4)


def kernel(x_prompt, x_sample, cache_k, cache_v, state_wkv, state_shift, page_table, c_prompt, c_sample, rel_bias, w_ada, b_ada, w_in, rw_mu, rw_w0, rw_w2, rw_a0, rw_a2, rw_g2, rw_k_k, rw_k_a, rw_r_k, rw_lnx_g, rw_lnx_b, w_out, ln1_g, ln1_b, ln2_g, ln2_b, ffn_w_gate, ffn_w_up, ffn_w_down, moe_w_router, moe_w_gate, moe_w_up, moe_w_down):
    n_p, seq, d = x_prompt.shape
    n_s, t_s, _ = x_sample.shape
    depth = w_in.shape[0]
    assert page_table.shape[1] % PAGES_PER_BLOCK == 0 and seq % MOBA_BLOCK == 0
    rows_s = n_s * t_s

    n_c = n_p + n_s
    pad_c = -n_c % 8
    c_all = jnp.pad(jnp.concatenate([c_prompt, c_sample], axis=0), ((0, pad_c), (0, 0)))
    mods = _ada_call(c_all, w_ada, b_ada)

    t_own, t_prev, ts_own, ts_prev = _bias_call(
        rel_bias, [(MOBA_BLOCK, MOBA_BLOCK), (MOBA_BLOCK, MOBA_BLOCK), (t_s, t_s), (t_s, MOBA_BLOCK)],
        [0, MOBA_BLOCK, 0, MOBA_BLOCK], key_major=[True, True, False, False])

    cache_kt = jnp.swapaxes(cache_k, 3, 4)
    cache_vt = jnp.swapaxes(cache_v, 3, 4)

    xp = x_prompt
    xs = x_sample.reshape(1, rows_s, d)
    zero_shift = jnp.zeros((n_p, RW_IN), F32)
    zero_state = jnp.zeros((n_p, H_RWKV, HEAD_DIM, HEAD_DIM), F32)
    outs = {k: [] for k in ("kp", "vp", "ks", "vs", "wp", "ws", "hp", "hs")}
    for l in range(depth):
        mod_p = [m[:, None, :] for m in jnp.split(mods[l, :n_p], 6, axis=-1)]
        mod_s = [jnp.repeat(m, t_s, axis=0)[None] for m in jnp.split(mods[l, n_p:n_c], 6, axis=-1)]
        prm = dict(mu=rw_mu[l], w0=rw_w0[l], w2=rw_w2[l], a0=rw_a0[l], a2=rw_a2[l], g2=rw_g2[l], k_k=rw_k_k[l],
                   k_a=rw_k_a[l], r_k=rw_r_k[l], lnx_g=rw_lnx_g[l], lnx_b=rw_lnx_b[l])
        w_in_bf = w_in[l].astype(BF16)
        w_out_bf = w_out[l].astype(BF16)
        if l % 2 == 0:
            w_router = jnp.zeros((d, LANES), F32)
            wg, wu, wd = (w[l // 2][None].astype(BF16) for w in (ffn_w_gate, ffn_w_up, ffn_w_down))
        else:
            w_router = jnp.pad(moe_w_router[l // 2], ((0, 0), (0, LANES - N_EXPERTS)))
            wg, wu, wd = (w[l // 2].astype(BF16) for w in (moe_w_gate, moe_w_up, moe_w_down))

        pa, pr = _inproj_call(xp, mod_p[1], mod_p[0], w_in_bf, PROMPT_TM)
        y_att = _moba_prompt_call(pa, rel_bias, t_own, t_prev)
        y_rw, s_new, sh_new = _rwkv_mix(pr, zero_shift, zero_state, prm, PROMPT_RW_TC, PROMPT_RW_TL, PROMPT_RW_CHUNK,
                                        PROMPT_RW_UNROLL)
        xp = _outproj_call(y_att, y_rw, xp, mod_p[2], w_out_bf, ln1_g[l], ln1_b[l], PROMPT_TM)
        xp = _ffn_call(xp, mod_p[4], mod_p[3], mod_p[5], w_router, wg, wu, wd, ln2_g[l], ln2_b[l], PROMPT_FF_TM,
                       FF_TILE)
        outs["kp"].append(_to_pages(pa[..., D_ATT:2 * D_ATT]))
        outs["vp"].append(_to_pages(pa[..., 2 * D_ATT:]))
        outs["wp"].append(s_new)
        outs["hp"].append(sh_new)

        pa, pr = _inproj_call(xs, mod_s[1], mod_s[0], w_in_bf, rows_s)
        y_att, k_new, v_new = _moba_sample(pa.reshape(n_s, t_s, ATT_IN), cache_kt, cache_vt, l, page_table, rel_bias,
                                           ts_own, ts_prev)
        y_rw, s_new, sh_new = _rwkv_mix(pr.reshape(n_s, t_s, RW_IN), state_shift[l], state_wkv[l], prm, t_s, t_s, t_s)
        xs = _outproj_call(y_att.reshape(1, rows_s, D_ATT), y_rw.reshape(1, rows_s, D_RWKV), xs, mod_s[2], w_out_bf,
                           ln1_g[l], ln1_b[l], rows_s)
        xs = _ffn_call(xs, mod_s[4], mod_s[3], mod_s[5], w_router, wg, wu, wd, ln2_g[l], ln2_b[l], rows_s, FF_TILE)
        outs["ks"].append(k_new)
        outs["vs"].append(v_new)
        outs["ws"].append(s_new)
        outs["hs"].append(sh_new)

    st = lambda k: jnp.stack(outs[k])
    return (xp, xs.reshape(n_s, t_s, d), st("kp"), st("vp"), st("ks"), st("vs"), st("wp"), st("ws"), st("hp"), st("hs"))
```

```python
import functools
import math

import jax
import jax.numpy as jnp
from jax import lax
from jax.experimental import pallas as pl
from jax.experimental.pallas import tpu as pltpu

F32 = jnp.float32
BF16 = jnp.bfloat16
I32 = jnp.int32
HI = lax.Precision.HIGHEST

D_MODEL = 1024
HEAD_DIM = 64
H_ATT = 8
H_RWKV = 8
D_ATT = H_ATT * HEAD_DIM
D_RWKV = H_RWKV * HEAD_DIM
ATT_IN = 3 * D_ATT
LORA_W = 64
LORA_A = 64
LORA_G = 128
RW_IN = 3 * D_RWKV + LORA_W + LORA_A + LORA_G
MOBA_BLOCK = 256
MOBA_TOPK = 3
PAGE_SIZE = 128
N_BUCKETS = 32
MAX_EXACT = N_BUCKETS // 2
MAX_DISTANCE = 128
D_FF = 2816
N_EXPERTS = 8
DEPTH = 2
ALPHA = (2 * DEPTH) ** 0.25
LN_EPS = 1e-5
LNX_EPS = 64e-5
NEG_INF = -1e30
ATT_SCALE = HEAD_DIM ** -0.5

LANES = 128
HEAD_PAIR = LANES // HEAD_DIM
BF16_ROWS = 16
VMEM_LIMIT = 56 * 1024 * 1024


def _cparams(*sem):
    return pltpu.CompilerParams(dimension_semantics=sem, vmem_limit_bytes=VMEM_LIMIT)


def _sigmoid(x):
    return 1.0 / (1.0 + jnp.exp(-x))


def _dot_nt(a, b, precision=None):
    return lax.dot_general(a, b, (((1,), (1,)), ((), ())), precision=precision, preferred_element_type=F32)


def _dot_tn(a, b, precision=None):
    return lax.dot_general(a, b, (((0,), (0,)), ((), ())), precision=precision, preferred_element_type=F32)


def _dot(a, b, precision=None):
    return jnp.dot(a, b, precision=precision, preferred_element_type=F32)


def _ada_kernel(c_ref, w_ref, b_ref, o_ref):
    c = c_ref[...]
    o_ref[0] = _dot(c * _sigmoid(c), w_ref[0], precision=HI) + b_ref[0]


def _ada_call(c_all, w_ada, b_ada):
    n = c_all.shape[0]
    depth, d, cols = w_ada.shape
    tn = 768
    return pl.pallas_call(
        _ada_kernel,
        grid=(depth, cols // tn),
        in_specs=[pl.BlockSpec((n, d), lambda l, j: (0, 0)),
                  pl.BlockSpec((1, d, tn), lambda l, j: (l, 0, j)),
                  pl.BlockSpec((1, 1, tn), lambda l, j: (l, 0, j))],
        out_specs=pl.BlockSpec((1, n, tn), lambda l, j: (l, 0, j)),
        out_shape=jax.ShapeDtypeStruct((depth, n, cols), F32),
        compiler_params=_cparams("parallel", "parallel"),
        name="ada_mod",
    )(c_all, w_ada, b_ada.reshape(depth, 1, cols))


def _bucket_of(dist):
    d = jnp.maximum(dist, 0)
    log_ratio = jnp.log(jnp.maximum(d, 1).astype(F32) / MAX_EXACT) / math.log(MAX_DISTANCE / MAX_EXACT)
    large = jnp.minimum(MAX_EXACT + (log_ratio * (N_BUCKETS - MAX_EXACT)).astype(I32), N_BUCKETS - 1)
    return jnp.where(d < MAX_EXACT, d, large)


def _bias_kernel(rb_ref, *o_refs, offsets, key_major):
    for o_ref, off, km in zip(o_refs, offsets, key_major):
        n_h, rows, cols = o_ref.shape
        dist = (lax.broadcasted_iota(I32, (rows, cols), 0) - lax.broadcasted_iota(I32, (rows, cols), 1))
        dist = (-dist if km else dist) + off
        bucket = _bucket_of(dist)
        for h in range(n_h):
            acc = jnp.zeros((rows, cols), F32)
            for b in range(N_BUCKETS):
                acc = jnp.where(bucket == b, rb_ref[b, h], acc)
            o_ref[h] = jnp.where(dist >= 0, acc, NEG_INF)


def _bias_call(rel_bias, shapes, offsets, key_major=None):
    n_h = rel_bias.shape[1]
    key_major = tuple(key_major) if key_major is not None else (False,) * len(shapes)
    return pl.pallas_call(
        functools.partial(_bias_kernel, offsets=tuple(offsets), key_major=key_major),
        in_specs=[pl.BlockSpec(memory_space=pltpu.SMEM)],
        out_specs=[pl.BlockSpec(memory_space=pltpu.VMEM) for _ in shapes],
        out_shape=[jax.ShapeDtypeStruct((n_h,) + s, F32) for s in shapes],
        name="rel_bias_tables",
    )(rel_bias)


def _inproj_kernel(x_ref, sc_ref, sh_ref, w_ref, oa_ref, or_ref, *, chunk):
    u = (x_ref[0] * (1.0 + sc_ref[0]) + sh_ref[0]).astype(BF16)
    att = oa_ref.shape[-1]
    for c0 in range(0, w_ref.shape[1], chunk):
        res = _dot(u, w_ref[:, c0:c0 + chunk])
        if c0 < att:
            oa_ref[0, :, c0:c0 + chunk] = res
        else:
            or_ref[0, :, c0 - att:c0 - att + chunk] = res


def _mod_spec(mod, tm):
    if mod.shape[1] == 1:
        return pl.BlockSpec((1, 1, mod.shape[2]), lambda n, i: (n, 0, 0))
    return pl.BlockSpec((1, tm, mod.shape[2]), lambda n, i: (n, i, 0))


def _inproj_call(x, sc, sh, w_bf, tm):
    n, l, d = x.shape
    cols = w_bf.shape[1]
    return pl.pallas_call(
        functools.partial(_inproj_kernel, chunk=256),
        grid=(n, l // tm),
        in_specs=[pl.BlockSpec((1, tm, d), lambda n, i: (n, i, 0)),
                  _mod_spec(sc, tm), _mod_spec(sh, tm),
                  pl.BlockSpec((d, cols), lambda n, i: (0, 0))],
        out_specs=[pl.BlockSpec((1, tm, ATT_IN), lambda n, i: (n, i, 0)),
                   pl.BlockSpec((1, tm, RW_IN), lambda n, i: (n, i, 0))],
        out_shape=[jax.ShapeDtypeStruct((n, l, ATT_IN), F32),
                   jax.ShapeDtypeStruct((n, l, RW_IN), F32)],
        compiler_params=_cparams("parallel", "parallel"),
        name="in_proj",
    )(x, sc, sh, w_bf)


def _moba_prompt_kernel(rb_ref, q_ref, k_ref, v_ref, town_ref, tprev_ref, o_ref, kb_ref, vb_ref, km_ref,
                        *, n_blk):
    hp = pl.program_id(1)
    i = pl.program_id(2)
    blk = MOBA_BLOCK

    @pl.when(i == 0)
    def _():
        k = k_ref[0]
        kb_ref[...] = k.astype(BF16)
        km_ref[...] = jnp.sum(k.reshape(n_blk, blk, LANES), axis=1) * (1.0 / blk)
        for c in range(n_blk):
            vt = v_ref[0, c * blk:(c + 1) * blk, :].T.astype(BF16)
            for h2 in range(HEAD_PAIR):
                vb_ref[h2, :HEAD_DIM, c * blk:(c + 1) * blk] = vt[h2 * HEAD_DIM:(h2 + 1) * HEAD_DIM]
                vb_ref[h2, HEAD_DIM:, c * blk:(c + 1) * blk] = jnp.ones((BF16_ROWS, blk), BF16)

    qt = q_ref[0].T
    dim_row = lax.broadcasted_iota(I32, qt.shape, 0)
    own0 = pl.multiple_of(i * blk, blk)
    heads = range(HEAD_PAIR)
    bidx = lax.broadcasted_iota(I32, (n_blk, blk), 0)
    past = bidx < i
    qt_h = [jnp.where((dim_row // HEAD_DIM) == h2, qt, 0.0) for h2 in heads]
    qb = [(z * ATT_SCALE).astype(BF16) for z in qt_h]
    sel = []
    for h2 in heads:
        g = jnp.where(past, _dot(km_ref[...], qt_h[h2], precision=HI), NEG_INF)
        rank = jnp.zeros(g.shape, I32)
        for jp in range(n_blk):
            gj = g[jp:jp + 1, :]
            rank = rank + ((gj > g) | ((gj == g) & (jp < bidx))).astype(I32)
        sel.append(jnp.where((rank < MOBA_TOPK) & past, 1.0, 0.0))
    b_far = [rb_ref[N_BUCKETS - 1, hp * HEAD_PAIR + h2] for h2 in heads]

    def attend(blocks, carry):
        s = []
        for r0, tables, rows in blocks:
            kb = kb_ref[pl.ds(r0, blk), :]
            s_b = [_dot(kb, qb[h2]) for h2 in heads]
            if tables is not None:
                s_b = [s_b[h2] + tables[h2] for h2 in heads]
            if rows is not None:
                s_b = [s_b[h2] + rows[h2] for h2 in heads]
            s.append(s_b)
        out = []
        for h2 in heads:
            m_new = functools.reduce(jnp.maximum, [jnp.max(s_b[h2], axis=0, keepdims=True) for s_b in s])
            if carry is not None:
                m, acc = carry[h2]
                m_new = jnp.maximum(m, m_new)
                acc = jnp.exp(m - m_new) * acc
            else:
                acc = 0.0
            for (r0, _, _), s_b in zip(blocks, s):
                p = jnp.exp(s_b[h2] - m_new).astype(BF16)
                acc = acc + _dot(vb_ref[h2, :, pl.ds(r0, blk)], p)
            out.append((m_new, acc))
        return tuple(out)

    def picked_row(j, h2, value):
        hit = jnp.sum(jnp.where(bidx == j, sel[h2], 0.0), axis=0, keepdims=True) > 0.5
        return jnp.where(hit, value, NEG_INF)

    def far_block(j):
        return pl.multiple_of(j * blk, blk), None, [picked_row(j, h2, b_far[h2]) for h2 in heads]

    def near_block(j):
        return pl.multiple_of(j * blk, blk), [tprev_ref[h2] for h2 in heads], [picked_row(j, h2, 0.0) for h2 in heads]

    carry = attend([(own0, [town_ref[h2] for h2 in heads], None)], None)
    n_far = jnp.maximum(i - 1, 0)
    grp = PROMPT_KEY_GROUP
    carry = lax.fori_loop(0, n_far // grp,
                          lambda j, c: attend([far_block(grp * j + u) for u in range(grp)], c), carry)
    carry = lax.fori_loop(grp * (n_far // grp), n_far, lambda j, c: attend([far_block(j)], c), carry)
    carry = lax.fori_loop(n_far, i, lambda j, c: attend([near_block(j)], c), carry)
    o_ref[0] = jnp.concatenate([acc[:HEAD_DIM] / acc[HEAD_DIM:HEAD_DIM + 1] for _, acc in carry], axis=0).T


def _moba_prompt_call(p_att, rel_bias, t_own, t_prev):
    n, l, _ = p_att.shape
    n_blk = l // MOBA_BLOCK
    n_hp = H_ATT // HEAD_PAIR
    tbl = pl.BlockSpec((HEAD_PAIR, MOBA_BLOCK, MOBA_BLOCK), lambda n, h, i: (h, 0, 0))
    return pl.pallas_call(
        functools.partial(_moba_prompt_kernel, n_blk=n_blk),
        grid=(n, n_hp, n_blk),
        in_specs=[pl.BlockSpec(memory_space=pltpu.SMEM),
                  pl.BlockSpec((1, MOBA_BLOCK, LANES), lambda n, h, i: (n, i, h)),
                  pl.BlockSpec((1, l, LANES), lambda n, h, i: (n, 0, n_hp + h)),
                  pl.BlockSpec((1, l, LANES), lambda n, h, i: (n, 0, 2 * n_hp + h)),
                  tbl, tbl],
        out_specs=pl.BlockSpec((1, MOBA_BLOCK, LANES), lambda n, h, i: (n, i, h)),
        out_shape=jax.ShapeDtypeStruct((n, l, D_ATT), F32),
        scratch_shapes=[pltpu.VMEM((l, LANES), BF16), pltpu.VMEM((HEAD_PAIR, HEAD_DIM + BF16_ROWS, l), BF16),
                        pltpu.VMEM((n_blk, LANES), F32)],
        compiler_params=_cparams("parallel", "parallel", "arbitrary"),
        name="moba_prompt",
    )(rel_bias, p_att, p_att, p_att, t_own, t_prev)


def _softplus(x):
    return jnp.maximum(x, 0.0) + jnp.log(1.0 + jnp.exp(-jnp.abs(x)))


def _rwkv_pre_kernel(p_ref, prev_ref, sh0_ref, mu_ref, w0_ref, w2_ref, a0_ref, a2_ref, g2_ref, kkw_ref, kaw_ref,
                     r_o, lw_o, k_o, v_o, kk_o, b_o, g_o):
    i = pl.program_id(1)
    p = p_ref[0]
    prev_row = jnp.where(i == 0, sh0_ref[0], prev_ref[0][7:8])
    row = lax.broadcasted_iota(I32, p.shape, 0)
    p_prev = jnp.where(row == 0, prev_row, pltpu.roll(p, 1, 0))
    xm = p + (p_prev - p) * mu_ref[...]
    o = 3 * D_RWKV
    r = xm[:, :D_RWKV]
    k = xm[:, D_RWKV:2 * D_RWKV]
    v = xm[:, 2 * D_RWKV:o]
    xw = xm[:, o:o + LORA_W]
    xa = xm[:, o + LORA_W:o + LORA_W + LORA_A]
    xg = xm[:, o + LORA_W + LORA_A:]
    w_log = -_softplus(-(w0_ref[...] + _dot(jnp.tanh(xw), w2_ref[...], precision=HI))) - 0.5
    a = _sigmoid(a0_ref[...] + _dot(xa, a2_ref[...], precision=HI))
    g = _dot(_sigmoid(xg), g2_ref[...], precision=HI)
    kk = k * kkw_ref[...]
    same_head = (lax.broadcasted_iota(I32, (D_RWKV, D_RWKV), 0) // HEAD_DIM
                 == lax.broadcasted_iota(I32, (D_RWKV, D_RWKV), 1) // HEAD_DIM)
    ss = _dot(kk * kk, jnp.where(same_head, 1.0, 0.0), precision=HI)
    kk = kk / jnp.maximum(jnp.sqrt(ss), 1e-12)
    r_o[0] = r
    lw_o[0] = -jnp.exp(w_log)
    k_o[0] = k * (1.0 + (a - 1.0) * kaw_ref[...])
    v_o[0] = v
    kk_o[0] = kk
    b_o[0] = kk * a
    g_o[0] = g


def _rwkv_pre_call(p_rw, shift0, mu, w0, w2, a0, a2, g2, k_k, k_a, tc):
    n, l, _ = p_rw.shape
    row = lambda z: z.reshape(1, -1)
    full = lambda z: pl.BlockSpec(z.shape, lambda n, i: (0,) * z.ndim)
    params = [row(mu), row(w0), w2, row(a0), a2, g2, row(k_k), row(k_a)]
    out_spec = pl.BlockSpec((1, tc, D_RWKV), lambda n, i: (n, i, 0))
    return pl.pallas_call(
        _rwkv_pre_kernel,
        grid=(n, l // tc),
        in_specs=[pl.BlockSpec((1, tc, RW_IN), lambda n, i: (n, i, 0)),
                  pl.BlockSpec((1, 8, RW_IN), lambda n, i: (n, jnp.maximum(i * (tc // 8) - 1, 0), 0)),
                  pl.BlockSpec((1, 1, RW_IN), lambda n, i: (n, 0, 0))] + [full(z) for z in params],
        out_specs=[out_spec] * 7,
        out_shape=[jax.ShapeDtypeStruct((n, l, D_RWKV), F32)] * 7,
        compiler_params=_cparams("parallel", "parallel"),
        name="rwkv_pre",
    )(p_rw, p_rw, shift0.reshape(n, 1, RW_IN), *params)


def _chunk_cumsum(x):
    rows = x.shape[0]
    row = lax.broadcasted_iota(I32, x.shape, 0)
    s = 1
    while s < rows:
        x = x + jnp.where(row >= s, pltpu.roll(x, s, 0), 0.0)
        s *= 2
    return x


def _rwkv_scan_kernel(r_ref, lw_ref, k_ref, v_ref, kk_ref, b_ref, g_ref, s0_ref, rk_ref, lg_ref, lb_ref,
                      y_ref, sf_ref, st_ref, yraw_ref, *, chunk, unroll, mxu):
    i = pl.program_id(2)
    C = chunk
    tl = r_ref.shape[1]
    n_dbl = max(C.bit_length() - 2, 0)

    @pl.when(i == 0)
    def _():
        st_ref[...] = s0_ref[0, 0]

    lane = lax.broadcasted_iota(I32, (1, LANES), 1)
    head0 = lane < HEAD_DIM
    ti = lax.broadcasted_iota(I32, (C, C), 0)
    si = lax.broadcasted_iota(I32, (C, C), 1)
    strict = si < ti
    incl = si <= ti
    eye_c = jnp.where(si == ti, 1.0, 0.0)
    rj = lax.broadcasted_iota(I32, (LANES, LANES), 0)
    cj = lax.broadcasted_iota(I32, (LANES, LANES), 1)
    same_head = (rj // HEAD_DIM) == (cj // HEAD_DIM)
    eye_l = rj == cj
    mm = lambda a, b: _dot(a.astype(mxu), b.astype(mxu))
    mm_nt = lambda a, b: _dot_nt(a.astype(mxu), b.astype(mxu))
    mm_tn = lambda a, b: _dot_tn(a.astype(mxu), b.astype(mxu))

    def chunk_operands(rows):
        lw = lw_ref[0, rows, :]
        r = r_ref[0, rows, :]
        k = k_ref[0, rows, :]
        v = v_ref[0, rows, :]
        kk = kk_ref[0, rows, :]
        b = b_ref[0, rows, :]
        cum = _chunk_cumsum(lw)
        c_end = cum[C - 1:C, :]
        e_neg = jnp.exp(-cum)
        e_end = jnp.exp(c_end - cum)
        al = -kk * jnp.exp(cum - lw)
        rt = r * jnp.exp(cum)
        return dict(v=v, al=al, rt=rt, bt=b * e_neg, kt=k * e_neg, bt2=b * e_end, kt2=k * e_end,
                    lhs=jnp.concatenate([al, rt], axis=0), c_end=c_end)

    def group_terms(rows_list):
        ops = [chunk_operands(rw) for rw in rows_list]
        ch = [(u, h) for u in range(len(ops)) for h in range(HEAD_PAIR)]
        mine = [head0, jnp.logical_not(head0)]
        lhs = [jnp.where(mine[h], ops[u]["lhs"], 0.0) for u, h in ch]
        ab = [mm_nt(lhs[n], ops[u]["bt"]) for n, (u, h) in enumerate(ch)]
        ak = [mm_nt(lhs[n], ops[u]["kt"]) for n, (u, h) in enumerate(ch)]
        m_ab = [jnp.where(strict, z[:C], 0.0) for z in ab]
        m_ak = [jnp.where(strict, z[:C], 0.0) for z in ak]
        b_br = [jnp.where(incl, z[C:], 0.0) for z in ab]
        b_kr = [jnp.where(incl, z[C:], 0.0) for z in ak]
        t_inv = [eye_c + z for z in m_ab]
        pk = m_ab
        for _ in range(n_dbl):
            pk = [mm(z, z) for z in pk]
            t_inv = [t + mm(z, t) for z, t in zip(pk, t_inv)]
        mv = [mm(m_ak[n], ops[u]["v"]) for n, (u, h) in enumerate(ch)]
        aw = [mm(t_inv[n], jnp.concatenate([ops[u]["al"], mv[n]], axis=1)) for n, (u, h) in enumerate(ch)]
        baw = [mm(b_br[n], aw[n]) for n in range(len(ch))]
        bkv = [mm(b_kr[n], ops[u]["v"]) for n, (u, h) in enumerate(ch)]
        terms = []
        for u, o in enumerate(ops):
            n0, n1 = u * HEAD_PAIR, u * HEAD_PAIR + 1
            pick = lambda f: jnp.where(head0, f(n0), f(n1))
            a2 = pick(lambda n: aw[n][:, :LANES])
            w = pick(lambda n: aw[n][:, LANES:])
            r2 = o["rt"] + pick(lambda n: baw[n][:, :LANES])
            y0 = pick(lambda n: baw[n][:, LANES:] + bkv[n])
            tc_t = jnp.where(same_head, mm_tn(o["bt2"], a2), 0.0)
            gc_t = jnp.where(same_head, mm_tn(jnp.concatenate([o["bt2"], o["kt2"]], axis=0),
                                              jnp.concatenate([w, o["v"]], axis=0)), 0.0)
            dec_col = jnp.sum(jnp.where(eye_l, jnp.exp(o["c_end"]), 0.0), axis=1, keepdims=True)
            terms.append((r2, y0, tc_t, gc_t, dec_col))
        return terms

    def group_body(c, carry):
        rows = [pl.ds(pl.multiple_of((c * unroll + u) * C, C), C) for u in range(unroll)]
        terms = group_terms(rows)
        st = st_ref[...]
        for rw, (r2, y0, tc_t, gc_t, dec_col) in zip(rows, terms):
            yraw_ref[rw, :] = mm(r2, st) + y0
            st = dec_col * st + mm(tc_t, st) + gc_t
        st_ref[...] = st
        return carry

    lax.fori_loop(0, tl // (C * unroll), group_body, 0)
    sf_ref[0, 0] = st_ref[...]

    def head_sum(z):
        s0 = jnp.sum(jnp.where(head0, z, 0.0), axis=1, keepdims=True)
        s1 = jnp.sum(jnp.where(head0, 0.0, z), axis=1, keepdims=True)
        return jnp.where(head0, s0, s1)

    y = yraw_ref[...]
    d = y - head_sum(y) * (1.0 / HEAD_DIM)
    var = head_sum(d * d) * (1.0 / HEAD_DIM)
    yn = d * lax.rsqrt(var + LNX_EPS) * lg_ref[...] + lb_ref[...]
    bonus = head_sum(r_ref[0] * k_ref[0] * rk_ref[...]) * v_ref[0]
    y_ref[0] = (yn + bonus) * g_ref[0]


def _rwkv_scan_call(ops, st0, r_k, lnx_g, lnx_b, tl, chunk, unroll):
    n, l, _ = ops[0].shape
    n_hp = H_RWKV // HEAD_PAIR
    mxu = BF16 if chunk >= 16 else F32
    tok = pl.BlockSpec((1, tl, LANES), lambda n, h, i: (n, i, h))
    par = pl.BlockSpec((1, LANES), lambda n, h, i: (0, h))
    st_spec = pl.BlockSpec((1, 1, LANES, LANES), lambda n, h, i: (n, h, 0, 0))
    return pl.pallas_call(
        functools.partial(_rwkv_scan_kernel, chunk=chunk, unroll=unroll, mxu=mxu),
        grid=(n, n_hp, l // tl),
        in_specs=[tok] * 7 + [st_spec, par, par, par],
        out_specs=[tok, st_spec],
        out_shape=[jax.ShapeDtypeStruct((n, l, D_RWKV), F32),
                   jax.ShapeDtypeStruct((n, n_hp, LANES, LANES), F32)],
        scratch_shapes=[pltpu.VMEM((LANES, LANES), F32), pltpu.VMEM((tl, LANES), F32)],
        compiler_params=_cparams("parallel", "parallel", "arbitrary"),
        name="rwkv_scan",
    )(*ops, st0, r_k.reshape(1, D_RWKV), lnx_g.reshape(1, D_RWKV), lnx_b.reshape(1, D_RWKV))


def _state_to_pairs(s):
    n, h = s.shape[:2]
    st = jnp.swapaxes(s, -1, -2).reshape(n, h // HEAD_PAIR, HEAD_PAIR, HEAD_DIM, HEAD_DIM)
    z = jnp.zeros_like(st[:, :, 0])
    top = jnp.concatenate([st[:, :, 0], z], axis=-1)
    bot = jnp.concatenate([z, st[:, :, 1]], axis=-1)
    return jnp.concatenate([top, bot], axis=-2)


def _pairs_to_state(sp):
    n, n_hp = sp.shape[:2]
    d0 = sp[:, :, :HEAD_DIM, :HEAD_DIM]
    d1 = sp[:, :, HEAD_DIM:, HEAD_DIM:]
    st = jnp.stack([d0, d1], axis=2).reshape(n, n_hp * HEAD_PAIR, HEAD_DIM, HEAD_DIM)
    return jnp.swapaxes(st, -1, -2)


def _rwkv_mix(p_rw, shift0, s0, prm, tc, tl, chunk, unroll=1):
    ops = _rwkv_pre_call(p_rw, shift0, prm["mu"], prm["w0"], prm["w2"], prm["a0"], prm["a2"], prm["g2"],
                         prm["k_k"], prm["k_a"], tc)
    y, sp = _rwkv_scan_call(ops, _state_to_pairs(s0), prm["r_k"], prm["lnx_g"], prm["lnx_b"], tl, chunk, unroll)
    return y, _pairs_to_state(sp), p_rw[:, -1]


def _layer_norm(z, g, b):
    mu = jnp.mean(z, axis=-1, keepdims=True)
    d = z - mu
    var = jnp.mean(d * d, axis=-1, keepdims=True)
    return d * lax.rsqrt(var + LN_EPS) * g + b


def _outproj_kernel(ya_ref, yr_ref, x_ref, gt_ref, w_ref, g_ref, b_ref, o_ref):
    o = (_dot(ya_ref[0].astype(BF16), w_ref[:D_ATT, :]) + _dot(yr_ref[0].astype(BF16), w_ref[D_ATT:, :]))
    o_ref[0] = _layer_norm(ALPHA * x_ref[0] + (1.0 + gt_ref[0]) * o, g_ref[...], b_ref[...])


def _outproj_call(y_att, y_rw, x, gt, w_bf, ln_g, ln_b, tm):
    n, l, d = x.shape
    row = pl.BlockSpec((1, d), lambda n, i: (0, 0))
    return pl.pallas_call(
        _outproj_kernel,
        grid=(n, l // tm),
        in_specs=[pl.BlockSpec((1, tm, D_ATT), lambda n, i: (n, i, 0)),
                  pl.BlockSpec((1, tm, D_RWKV), lambda n, i: (n, i, 0)),
                  pl.BlockSpec((1, tm, d), lambda n, i: (n, i, 0)),
                  _mod_spec(gt, tm),
                  pl.BlockSpec(w_bf.shape, lambda n, i: (0, 0)), row, row],
        out_specs=pl.BlockSpec((1, tm, d), lambda n, i: (n, i, 0)),
        out_shape=jax.ShapeDtypeStruct((n, l, d), F32),
        compiler_params=_cparams("parallel", "parallel"),
        name="out_proj_ln",
    )(y_att, y_rw, x, gt, w_bf, ln_g.reshape(1, d), ln_b.reshape(1, d))


def _ffn_kernel(x_ref, sc_ref, sh_ref, gt_ref, wr_ref, wg_ref, wu_ref, wd_ref, g_ref, b_ref, o_ref,
                u_ref, gate_ref, *, n_experts):
    e = pl.program_id(2)
    f = pl.program_id(3)
    last = (e == pl.num_programs(2) - 1) & (f == pl.num_programs(3) - 1)

    @pl.when((e == 0) & (f == 0))
    def _():
        u = x_ref[0] * (1.0 + sc_ref[0]) + sh_ref[0]
        u_ref[...] = u.astype(BF16)
        o_ref[...] = jnp.zeros_like(o_ref)
        if n_experts > 1:
            logits = _dot(u, wr_ref[...], precision=HI)
            lane = lax.broadcasted_iota(I32, logits.shape, 1)
            logits = jnp.where(lane < n_experts, logits, NEG_INF)
            v1 = jnp.max(logits, axis=1, keepdims=True)
            i1 = jnp.min(jnp.where(logits == v1, lane, LANES), axis=1, keepdims=True)
            rest = jnp.where(lane == i1, NEG_INF, logits)
            v2 = jnp.max(rest, axis=1, keepdims=True)
            i2 = jnp.min(jnp.where(rest == v2, lane, LANES), axis=1, keepdims=True)
            w2 = 1.0 / (1.0 + jnp.exp(v1 - v2))
            gate_ref[...] = jnp.where(lane == i1, 1.0 - w2, 0.0) + jnp.where(lane == i2, w2, 0.0)

    u = u_ref[...]
    hg = _dot(u, wg_ref[0])
    hu = _dot(u, wu_ref[0])
    h = hg * _sigmoid(hg) * hu
    if n_experts > 1:
        lane = lax.broadcasted_iota(I32, gate_ref.shape, 1)
        h = h * jnp.sum(jnp.where(lane == e, gate_ref[...], 0.0), axis=1, keepdims=True)
    o_ref[0] += _dot(h.astype(BF16), wd_ref[0])

    @pl.when(last)
    def _():
        o_ref[0] = _layer_norm(ALPHA * x_ref[0] + (1.0 + gt_ref[0]) * o_ref[0], g_ref[...], b_ref[...])


def _ffn_call(x, sc, sh, gt, w_router, wg_bf, wu_bf, wd_bf, ln_g, ln_b, tm, tf):
    n, l, d = x.shape
    n_e, _, d_ff = wg_bf.shape
    row = pl.BlockSpec((1, d), lambda n, i, e, f: (0, 0))
    mod = lambda m: (pl.BlockSpec((1, 1, d), lambda n, i, e, f: (n, 0, 0)) if m.shape[1] == 1
                     else pl.BlockSpec((1, tm, d), lambda n, i, e, f: (n, i, 0)))
    return pl.pallas_call(
        functools.partial(_ffn_kernel, n_experts=n_e),
        grid=(n, l // tm, n_e, d_ff // tf),
        in_specs=[pl.BlockSpec((1, tm, d), lambda n, i, e, f: (n, i, 0), pipeline_mode=pl.Buffered(1)),
                  mod(sc), mod(sh), mod(gt),
                  pl.BlockSpec(w_router.shape, lambda n, i, e, f: (0, 0)),
                  pl.BlockSpec((1, d, tf), lambda n, i, e, f: (e, 0, f)),
                  pl.BlockSpec((1, d, tf), lambda n, i, e, f: (e, 0, f)),
                  pl.BlockSpec((1, tf, d), lambda n, i, e, f: (e, f, 0)),
                  row, row],
        out_specs=pl.BlockSpec((1, tm, d), lambda n, i, e, f: (n, i, 0)),
        out_shape=jax.ShapeDtypeStruct((n, l, d), F32),
        scratch_shapes=[pltpu.VMEM((tm, d), BF16), pltpu.VMEM((tm, LANES), F32)],
        compiler_params=_cparams("parallel", "parallel", "arbitrary", "arbitrary"),
        name="ffn_ln" if n_e == 1 else "moe_ln",
    )(x, sc, sh, gt, w_router, wg_bf, wu_bf, wd_bf, ln_g.reshape(1, d), ln_b.reshape(1, d))


ROUTE_BLOCK = 256


def _router_kernel(x_ref, sc_ref, sh_ref, wr_ref, gate_o, pos_o, post_o, cnt_o, *, n_experts):
    u = x_ref[0] * (1.0 + sc_ref[0]) + sh_ref[0]
    tm = u.shape[0]
    logits = _dot(u, wr_ref[...], precision=HI)
    lane = lax.broadcasted_iota(I32, logits.shape, 1)
    logits = jnp.where(lane < n_experts, logits, NEG_INF)
    v1 = jnp.max(logits, axis=1, keepdims=True)
    i1 = jnp.min(jnp.where(logits == v1, lane, LANES), axis=1, keepdims=True)
    rest = jnp.where(lane == i1, NEG_INF, logits)
    v2 = jnp.max(rest, axis=1, keepdims=True)
    i2 = jnp.min(jnp.where(rest == v2, lane, LANES), axis=1, keepdims=True)
    w2 = 1.0 / (1.0 + jnp.exp(v1 - v2))
    gate_o[0] = jnp.where(lane == i1, 1.0 - w2, 0.0) + jnp.where(lane == i2, w2, 0.0)
    chosen = (lane == i1) | (lane == i2)
    sel = jnp.where(chosen, 1.0, 0.0)
    rb = ROUTE_BLOCK
    tri = jnp.where(lax.broadcasted_iota(I32, (rb, rb), 1) < lax.broadcasted_iota(I32, (rb, rb), 0), 1.0, 0.0).astype(BF16)
    carry = jnp.zeros((1, LANES), F32)
    for c in range(tm // rb):
        blk = sel[c * rb:(c + 1) * rb]
        pos = _dot(tri, blk.astype(BF16)) + carry
        pos = jnp.where(chosen[c * rb:(c + 1) * rb], pos, -1.0)
        pos_o[0, c * rb:(c + 1) * rb, :] = pos
        post_o[0, :, c * rb:(c + 1) * rb] = pos.T
        carry = carry + jnp.sum(blk, axis=0, keepdims=True)
    cnt_o[0, 0] = jnp.broadcast_to(carry, (8, LANES)).astype(I32)


def _router_call(x, sc, sh, w_router, n_experts, tm):
    n, l, d = x.shape
    mod = pl.BlockSpec((1, 1, d), lambda n, i: (n, 0, 0))
    tok = pl.BlockSpec((1, tm, LANES), lambda n, i: (n, i, 0))
    return pl.pallas_call(
        functools.partial(_router_kernel, n_experts=n_experts),
        grid=(n, l // tm),
        in_specs=[pl.BlockSpec((1, tm, d), lambda n, i: (n, i, 0)), mod, mod,
                  pl.BlockSpec(w_router.shape, lambda n, i: (0, 0))],
        out_specs=[tok, tok, pl.BlockSpec((1, LANES, tm), lambda n, i: (n, 0, i)),
                   pl.BlockSpec((1, 1, 8, LANES), lambda n, i: (n, i, 0, 0))],
        out_shape=[jax.ShapeDtypeStruct((n, l, LANES), F32), jax.ShapeDtypeStruct((n, l, LANES), F32),
                   jax.ShapeDtypeStruct((n, LANES, l), F32), jax.ShapeDtypeStruct((n, l // tm, 8, LANES), I32)],
        compiler_params=_cparams("parallel", "parallel"),
        name="moe_router",
    )(x, sc, sh, w_router)


def _moe_routed_kernel(cnt_ref, x_ref, sc_ref, sh_ref, gt_ref, gate_ref, pos_ref, post_ref, wg_ref, wu_ref, wd_ref,
                       g_ref, b_ref, o_ref, u_ref, xs_ref, oacc_ref, *, n_experts, sub):
    n = pl.program_id(0)
    i = pl.program_id(1)
    e = pl.program_id(2)
    f = pl.program_id(3)
    n_f = pl.num_programs(3)
    tm = x_ref.shape[1]
    count = cnt_ref[(n * pl.num_programs(1) + i) * n_experts + e]
    n_sub = (count + sub - 1) // sub

    @pl.when((e == 0) & (f == 0))
    def _():
        u_ref[...] = (x_ref[0] * (1.0 + sc_ref[0]) + sh_ref[0]).astype(BF16)
        o_ref[...] = jnp.zeros_like(o_ref)

    @pl.when(f == 0)
    def _():
        erow = lax.broadcasted_iota(I32, (8, tm), 0)
        pos_row = jnp.sum(jnp.where(erow == e, post_ref[0, 0:8, :], 0.0), axis=0, keepdims=True)

        def gather(s, c):
            rows = pl.ds(pl.multiple_of(s * sub, BF16_ROWS), sub)
            want = (lax.broadcasted_iota(I32, (sub, tm), 0) + s * sub).astype(F32)
            onehot = jnp.where(pos_row == want, 1.0, 0.0).astype(BF16)
            xs_ref[rows, :] = _dot(onehot, u_ref[...]).astype(BF16)
            oacc_ref[rows, :] = jnp.zeros((sub, oacc_ref.shape[1]), F32)
            return c

        lax.fori_loop(0, n_sub, gather, 0)

    def expert(s, c):
        rows = pl.ds(pl.multiple_of(s * sub, BF16_ROWS), sub)
        xs = xs_ref[rows, :]
        hg = _dot(xs, wg_ref[0])
        h = hg * _sigmoid(hg) * _dot(xs, wu_ref[0])
        oacc_ref[rows, :] += _dot(h.astype(BF16), wd_ref[0])
        return c

    lax.fori_loop(0, n_sub, expert, 0)

    @pl.when(f == n_f - 1)
    def _():
        lane = lax.broadcasted_iota(I32, (tm, LANES), 1)
        pos_col = jnp.sum(jnp.where(lane == e, pos_ref[0], 0.0), axis=1, keepdims=True)
        gate_col = jnp.sum(jnp.where(lane == e, gate_ref[0], 0.0), axis=1, keepdims=True)

        def scatter(s, c):
            rows = pl.ds(pl.multiple_of(s * sub, BF16_ROWS), sub)
            want = (lax.broadcasted_iota(I32, (tm, sub), 1) + s * sub).astype(F32)
            w = jnp.where(pos_col == want, gate_col, 0.0)
            w_hi = w.astype(BF16)
            w_lo = (w - w_hi.astype(F32)).astype(BF16)
            y = oacc_ref[rows, :].astype(BF16)
            o_ref[0] += _dot(w_hi, y) + _dot(w_lo, y)
            return c

        lax.fori_loop(0, n_sub, scatter, 0)

    @pl.when((e == n_experts - 1) & (f == n_f - 1))
    def _():
        o_ref[0] = _layer_norm(ALPHA * x_ref[0] + (1.0 + gt_ref[0]) * o_ref[0], g_ref[...], b_ref[...])


def _moe_routed_call(x, sc, sh, gt, w_router, wg_bf, wu_bf, wd_bf, ln_g, ln_b, tm, tf, sub):
    n, l, d = x.shape
    n_e, _, d_ff = wg_bf.shape
    gate, pos, post, cnt = _router_call(x, sc, sh, w_router, n_e, tm)
    cnt_flat = cnt[:, :, 0, :n_e].reshape(-1)
    n_sub_max = -(-tm // sub)
    row = pl.BlockSpec((1, d), lambda n, i, e, f, c: (0, 0))
    mod = pl.BlockSpec((1, 1, d), lambda n, i, e, f, c: (n, 0, 0))
    tok = pl.BlockSpec((1, tm, LANES), lambda n, i, e, f, c: (n, i, 0))
    return pl.pallas_call(
        functools.partial(_moe_routed_kernel, n_experts=n_e, sub=sub),
        grid_spec=pltpu.PrefetchScalarGridSpec(
            num_scalar_prefetch=1,
            grid=(n, l // tm, n_e, d_ff // tf),
            in_specs=[pl.BlockSpec((1, tm, d), lambda n, i, e, f, c: (n, i, 0)), mod, mod, mod, tok, tok,
                      pl.BlockSpec((1, LANES, tm), lambda n, i, e, f, c: (n, 0, i)),
                      pl.BlockSpec((1, d, tf), lambda n, i, e, f, c: (e, 0, f)),
                      pl.BlockSpec((1, d, tf), lambda n, i, e, f, c: (e, 0, f)),
                      pl.BlockSpec((1, tf, d), lambda n, i, e, f, c: (e, f, 0)),
                      row, row],
            out_specs=pl.BlockSpec((1, tm, d), lambda n, i, e, f, c: (n, i, 0)),
            scratch_shapes=[pltpu.VMEM((tm, d), BF16), pltpu.VMEM((n_sub_max * sub, d), BF16),
                            pltpu.VMEM((n_sub_max * sub, d), F32)]),
        out_shape=jax.ShapeDtypeStruct((n, l, d), F32),
        compiler_params=_cparams("parallel", "parallel", "arbitrary", "arbitrary"),
        name="moe_routed_ln",
    )(cnt_flat, x, sc, sh, gt, gate, pos, post, wg_bf, wu_bf, wd_bf, ln_g.reshape(1, d), ln_b.reshape(1, d))


PAGES_PER_BLOCK = MOBA_BLOCK // PAGE_SIZE
KMEAN_PAGES = 32


def _kmean_kernel(pt_ref, *refs):
    o_ref = refs[-1]
    g = pl.program_id(1)

    @pl.when(g == 0)
    def _():
        o_ref[...] = jnp.zeros_like(o_ref)

    lane = lax.broadcasted_iota(I32, o_ref.shape[1:], 1)
    out = o_ref[0]
    for blk in range(KMEAN_PAGES // PAGES_PER_BLOCK):
        tot = functools.reduce(lambda a, c: a + c,
                               [refs[blk * PAGES_PER_BLOCK + r][...] for r in range(PAGES_PER_BLOCK)])
        col = jnp.sum(tot, axis=-1, keepdims=True).reshape(-1, 1) * (1.0 / MOBA_BLOCK)
        out = jnp.where(lane == g * (KMEAN_PAGES // PAGES_PER_BLOCK) + blk, col, out)
    o_ref[0] = out


def _kmean_call(cache_kt, layer, page_table, n_full):
    n_seq = page_table.shape[0]
    _, _, n_h, hd, ps = cache_kt.shape
    page = lambda r: pl.BlockSpec((None, None, n_h, hd, ps),
                                  lambda b, g, pt: (layer, pt[b, g * KMEAN_PAGES + r], 0, 0, 0))
    blocks_per_step = KMEAN_PAGES // PAGES_PER_BLOCK
    return pl.pallas_call(
        _kmean_kernel,
        grid_spec=pltpu.PrefetchScalarGridSpec(
            num_scalar_prefetch=1,
            grid=(n_seq, n_full // blocks_per_step),
            in_specs=[page(r) for r in range(KMEAN_PAGES)],
            out_specs=pl.BlockSpec((1, n_h * hd, n_full), lambda b, g, pt: (b, 0, 0))),
        out_shape=jax.ShapeDtypeStruct((n_seq, n_h * hd, n_full), F32),
        compiler_params=_cparams("parallel", "arbitrary"),
        name="moba_kmean",
    )(page_table, *([cache_kt] * KMEAN_PAGES))


def _top3_kernel(q_ref, km_ref, o_ref):
    n_h = q_ref.shape[1]
    for h in range(n_h):
        g = _dot(q_ref[0, h], km_ref[0, h], precision=HI)
        lane = lax.broadcasted_iota(I32, g.shape, 1)
        out_lane = lax.broadcasted_iota(I32, (g.shape[0], LANES), 1)
        out = jnp.zeros((g.shape[0], LANES), I32)
        for s in range(MOBA_TOPK):
            mx = jnp.max(g, axis=1, keepdims=True)
            idx = jnp.min(jnp.where(g == mx, lane, g.shape[1]), axis=1, keepdims=True)
            out = jnp.where(out_lane == s, idx, out)
            g = jnp.where(lane == idx, -jnp.inf, g)
        o_ref[0, h] = out


def _top3_call(q, k_mean_t):
    n_seq, n_h, t, hd = q.shape
    n_full = k_mean_t.shape[3]
    return pl.pallas_call(
        _top3_kernel,
        grid=(n_seq,),
        in_specs=[pl.BlockSpec((1, n_h, t, hd), lambda b: (b, 0, 0, 0)),
                  pl.BlockSpec((1, n_h, hd, n_full), lambda b: (b, 0, 0, 0))],
        out_specs=pl.BlockSpec((1, n_h, t, LANES), lambda b: (b, 0, 0, 0)),
        out_shape=jax.ShapeDtypeStruct((n_seq, n_h, t, LANES), I32),
        compiler_params=_cparams("parallel"),
        name="moba_top3",
    )(q, k_mean_t)


def _moba_sample_kernel(pt_ref, sel_ref, rb_ref, qt_ref, knt_ref, vnt_ref, town_ref, tprev_ref, ck_hbm, cv_hbm,
                        o_ref, kbuf, vbuf, sems, *, layer, n_seq, n_h, n_t, n_full):
    n_pg = MOBA_TOPK * PAGES_PER_BLOCK
    n_copies = n_t * n_pg
    b = pl.program_id(0)
    h = pl.program_id(1)
    step = b * n_h + h
    slot = step % 2

    def page_copies(step_f, slot_f):
        b_f = step_f // n_h
        h_f = step_f % n_h
        copies = []
        for t in range(n_t):
            for s in range(MOBA_TOPK):
                blk_id = sel_ref[(step_f * n_t + t) * MOBA_TOPK + s]
                for r in range(PAGES_PER_BLOCK):
                    pg = pt_ref[b_f, blk_id * PAGES_PER_BLOCK + r]
                    j = (t * MOBA_TOPK + s) * PAGES_PER_BLOCK + r
                    copies.append(pltpu.make_async_copy(ck_hbm.at[layer, pg, h_f], kbuf.at[slot_f, j], sems.at[0, slot_f]))
                    copies.append(pltpu.make_async_copy(cv_hbm.at[layer, pg, h_f], vbuf.at[slot_f, j], sems.at[1, slot_f]))
        return copies

    @pl.when(step == 0)
    def _():
        for c in page_copies(step, slot):
            c.start()

    @pl.when(step + 1 < n_seq * n_h)
    def _():
        for c in page_copies(step + 1, 1 - slot):
            c.start()

    for j in range(n_copies):
        pltpu.make_async_copy(ck_hbm.at[layer, 0, 0], kbuf.at[slot, j], sems.at[0, slot]).wait()
        pltpu.make_async_copy(cv_hbm.at[layer, 0, 0], vbuf.at[slot, j], sems.at[1, slot]).wait()

    k_refs = [kbuf.at[slot, j] for j in range(n_copies)]
    v_refs = [vbuf.at[slot, j] for j in range(n_copies)]
    qt = qt_ref[0, 0] * ATT_SCALE
    knt = knt_ref[0, 0]
    vnt = vnt_ref[0, 0]
    b_far = rb_ref[N_BUCKETS - 1, h]
    out_lane = lax.broadcasted_iota(I32, (HEAD_DIM, n_t), 1)
    out = jnp.zeros((HEAD_DIM, n_t), F32)
    for t in range(n_t):
        q_col = qt[:, t:t + 1]
        base = ((b * n_h + h) * n_t + t) * MOBA_TOPK
        scores = [jnp.sum(knt * q_col, axis=0, keepdims=True) + town_ref[0, t:t + 1, :]]
        for s in range(MOBA_TOPK):
            last = sel_ref[base + s] == n_full - 1
            for r in range(PAGES_PER_BLOCK):
                sc = jnp.sum(k_refs[(t * MOBA_TOPK + s) * PAGES_PER_BLOCK + r][...] * q_col, axis=0, keepdims=True)
                bias = jnp.where(last, tprev_ref[0, t:t + 1, r * PAGE_SIZE:(r + 1) * PAGE_SIZE], b_far)
                scores.append(sc + bias)
        m = functools.reduce(jnp.maximum, [jnp.max(sc, axis=1, keepdims=True) for sc in scores])
        probs = [jnp.exp(sc - m) for sc in scores]
        denom = functools.reduce(lambda a, c: a + c, [jnp.sum(p, axis=1, keepdims=True) for p in probs])
        acc = functools.reduce(lambda a, c: a + c,
                               [probs[1 + j] * v_refs[t * n_pg + j][...] for j in range(n_pg)])
        col = (jnp.sum(acc, axis=1, keepdims=True) + jnp.sum(probs[0] * vnt, axis=1, keepdims=True)) / denom
        out = jnp.where(out_lane == t, col, out)
    o_ref[0, 0] = out


def _moba_sample_call(qt, knt, vnt, cache_kt, cache_vt, layer, page_table, sel_flat, rel_bias, ts_own, ts_prev, n_full):
    n_seq, n_h, hd, n_t = qt.shape
    ps = cache_kt.shape[4]
    tok = pl.BlockSpec((1, 1, hd, n_t), lambda b, h, pt, sel: (b, h, 0, 0))
    n_copies = n_t * MOBA_TOPK * PAGES_PER_BLOCK
    return pl.pallas_call(
        functools.partial(_moba_sample_kernel, layer=layer, n_seq=n_seq, n_h=n_h, n_t=n_t, n_full=n_full),
        grid_spec=pltpu.PrefetchScalarGridSpec(
            num_scalar_prefetch=2,
            grid=(n_seq, n_h),
            in_specs=[pl.BlockSpec(memory_space=pltpu.SMEM), tok, tok, tok,
                      pl.BlockSpec((1, n_t, n_t), lambda b, h, pt, sel: (h, 0, 0)),
                      pl.BlockSpec((1, n_t, MOBA_BLOCK), lambda b, h, pt, sel: (h, 0, 0)),
                      pl.BlockSpec(memory_space=pl.ANY), pl.BlockSpec(memory_space=pl.ANY)],
            out_specs=tok,
            scratch_shapes=[pltpu.VMEM((2, n_copies, hd, ps), F32), pltpu.VMEM((2, n_copies, hd, ps), F32),
                            pltpu.SemaphoreType.DMA((2, 2))]),
        out_shape=jax.ShapeDtypeStruct((n_seq, n_h, hd, n_t), F32),
        compiler_params=_cparams("arbitrary", "arbitrary"),
        name="moba_sample",
    )(page_table, sel_flat, rel_bias, qt, knt, vnt, ts_own, ts_prev, cache_kt, cache_vt)


def _moba_sample(p_att, cache_kt, cache_vt, layer, page_table, rel_bias, ts_own, ts_prev):
    n_seq, n_t, _ = p_att.shape
    n_full = page_table.shape[1] // PAGES_PER_BLOCK
    qkv = p_att.reshape(n_seq, n_t, 3, H_ATT, HEAD_DIM).transpose(2, 0, 3, 1, 4)
    q, k_new, v_new = qkv[0], qkv[1], qkv[2]
    k_mean_t = _kmean_call(cache_kt, layer, page_table, n_full).reshape(n_seq, H_ATT, HEAD_DIM, n_full)
    picks = _top3_call(q, k_mean_t)
    sel_flat = picks[..., :MOBA_TOPK].reshape(-1)
    tr = lambda z: jnp.swapaxes(z, 2, 3)
    out_t = _moba_sample_call(tr(q), tr(k_new), tr(v_new), cache_kt, cache_vt, layer, page_table, sel_flat, rel_bias,
                              ts_own, ts_prev, n_full)
    return out_t.transpose(0, 3, 1, 2).reshape(n_seq, n_t, D_ATT), k_new, v_new


PROMPT_TM = 512
PROMPT_KEY_GROUP = 4
PROMPT_FF_TM = 1024
PROMPT_MOE_TM = 1024
PROMPT_MOE_SUB = 320
PROMPT_RW_TC = 256
PROMPT_RW_TL = 512
PROMPT_RW_CHUNK = 32
PROMPT_RW_UNROLL = 8
FF_TILE = 256


def _to_pages(z):
    b, s, _ = z.shape
    return z.reshape(b, s // PAGE_SIZE, PAGE_SIZE, H_ATT, HEAD_DIM).transpose(0, 1, 3, 2, 4)


def kernel(x_prompt, x_sample, cache_k, cache_v, state_wkv, state_shift, page_table, c_prompt, c_sample, rel_bias, w_ada, b_ada, w_in, rw_mu, rw_w0, rw_w2, rw_a0, rw_a2, rw_g2, rw_k_k, rw_k_a, rw_r_k, rw_lnx_g, rw_lnx_b, w_out, ln1_g, ln1_b, ln2_g, ln2_b, ffn_w_gate, ffn_w_up, ffn_w_down, moe_w_router, moe_w_gate, moe_w_up, moe_w_down):
    n_p, seq, d = x_prompt.shape
    n_s, t_s, _ = x_sample.shape
    depth = w_in.shape[0]
    assert page_table.shape[1] % PAGES_PER_BLOCK == 0 and seq % MOBA_BLOCK == 0
    rows_s = n_s * t_s

    n_c = n_p + n_s
    pad_c = -n_c % 8
    c_all = jnp.pad(jnp.concatenate([c_prompt, c_sample], axis=0), ((0, pad_c), (0, 0)))
    mods = _ada_call(c_all, w_ada, b_ada)

    t_own, t_prev, ts_own, ts_prev = _bias_call(
        rel_bias, [(MOBA_BLOCK, MOBA_BLOCK), (MOBA_BLOCK, MOBA_BLOCK), (t_s, t_s), (t_s, MOBA_BLOCK)],
        [0, MOBA_BLOCK, 0, MOBA_BLOCK], key_major=[True, True, False, False])

    cache_kt = jnp.swapaxes(cache_k, 3, 4)
    cache_vt = jnp.swapaxes(cache_v, 3, 4)

    xp = x_prompt
    xs = x_sample.reshape(1, rows_s, d)
    zero_shift = jnp.zeros((n_p, RW_IN), F32)
    zero_state = jnp.zeros((n_p, H_RWKV, HEAD_DIM, HEAD_DIM), F32)
    outs = {k: [] for k in ("kp", "vp", "ks", "vs", "wp", "ws", "hp", "hs")}
    for l in range(depth):
        mod_p = [m[:, None, :] for m in jnp.split(mods[l, :n_p], 6, axis=-1)]
        mod_s = [jnp.repeat(m, t_s, axis=0)[None] for m in jnp.split(mods[l, n_p:n_c], 6, axis=-1)]
        prm = dict(mu=rw_mu[l], w0=rw_w0[l], w2=rw_w2[l], a0=rw_a0[l], a2=rw_a2[l], g2=rw_g2[l], k_k=rw_k_k[l],
                   k_a=rw_k_a[l], r_k=rw_r_k[l], lnx_g=rw_lnx_g[l], lnx_b=rw_lnx_b[l])
        w_in_bf = w_in[l].astype(BF16)
        w_out_bf = w_out[l].astype(BF16)
        if l % 2 == 0:
            w_router = jnp.zeros((d, LANES), F32)
            wg, wu, wd = (w[l // 2][None].astype(BF16) for w in (ffn_w_gate, ffn_w_up, ffn_w_down))
        else:
            w_router = jnp.pad(moe_w_router[l // 2], ((0, 0), (0, LANES - N_EXPERTS)))
            wg, wu, wd = (w[l // 2].astype(BF16) for w in (moe_w_gate, moe_w_up, moe_w_down))

        pa, pr = _inproj_call(xp, mod_p[1], mod_p[0], w_in_bf, PROMPT_TM)
        y_att = _moba_prompt_call(pa, rel_bias, t_own, t_prev)
        y_rw, s_new, sh_new = _rwkv_mix(pr, zero_shift, zero_state, prm, PROMPT_RW_TC, PROMPT_RW_TL, PROMPT_RW_CHUNK,
                                        PROMPT_RW_UNROLL)
        xp = _outproj_call(y_att, y_rw, xp, mod_p[2], w_out_bf, ln1_g[l], ln1_b[l], PROMPT_TM)
        if l % 2 == 0:
            xp = _ffn_call(xp, mod_p[4], mod_p[3], mod_p[5], w_router, wg, wu, wd, ln2_g[l], ln2_b[l], PROMPT_FF_TM,
                           FF_TILE)
        else:
            xp = _moe_routed_call(xp, mod_p[4], mod_p[3], mod_p[5], w_router, wg, wu, wd, ln2_g[l], ln2_b[l],
                                  PROMPT_MOE_TM, FF_TILE, PROMPT_MOE_SUB)
        outs["kp"].append(_to_pages(pa[..., D_ATT:2 * D_ATT]))
        outs["vp"].append(_to_pages(pa[..., 2 * D_ATT:]))
        outs["wp"].append(s_new)
        outs["hp"].append(sh_new)

        pa, pr = _inproj_call(xs, mod_s[1], mod_s[0], w_in_bf, rows_s)
        y_att, k_new, v_new = _moba_sample(pa.reshape(n_s, t_s, ATT_IN), cache_kt, cache_vt, l, page_table, rel_bias,
                                           ts_own, ts_prev)
        y_rw, s_new, sh_new = _rwkv_mix(pr.reshape(n_s, t_s, RW_IN), state_shift[l], state_wkv[l], prm, t_s, t_s, t_s)
        xs = _outproj_call(y_att.reshape(1, rows_s, D_ATT), y_rw.reshape(1, rows_s, D_RWKV), xs, mod_s[2], w_out_bf,
                           ln1_g[l], ln1_b[l], rows_s)
        xs = _ffn_call(xs, mod_s[4], mod_s[3], mod_s[5], w_router, wg, wu, wd, ln2_g[l], ln2_b[l], rows_s, FF_TILE)
        outs["ks"].append(k_new)
        outs["vs"].append(v_new)
        outs["ws"].append(s_new)
        outs["hs"].append(sh_new)

    st = lambda k: jnp.stack(outs[k])
    return (xp, xs.reshape(n_s, t_s, d), st("kp"), st("vp"), st("ks"), st("vs"), st("wp"), st("ws"), st("hp"), st("hs"))
```

```python
import functools
import math

import jax
import jax.numpy as jnp
from jax import lax
from jax.experimental import pallas as pl
from jax.experimental.pallas import tpu as pltpu

F32 = jnp.float32
BF16 = jnp.bfloat16
I32 = jnp.int32
HI = lax.Precision.HIGHEST

D_MODEL = 1024
HEAD_DIM = 64
H_ATT = 8
H_RWKV = 8
D_ATT = H_ATT * HEAD_DIM
D_RWKV = H_RWKV * HEAD_DIM
ATT_IN = 3 * D_ATT
LORA_W = 64
LORA_A = 64
LORA_G = 128
RW_IN = 3 * D_RWKV + LORA_W + LORA_A + LORA_G
MOBA_BLOCK = 256
MOBA_TOPK = 3
PAGE_SIZE = 128
N_BUCKETS = 32
MAX_EXACT = N_BUCKETS // 2
MAX_DISTANCE = 128
D_FF = 2816
N_EXPERTS = 8
DEPTH = 2
ALPHA = (2 * DEPTH) ** 0.25
LN_EPS = 1e-5
LNX_EPS = 64e-5
NEG_INF = -1e30
ATT_SCALE = HEAD_DIM ** -0.5

LANES = 128
HEAD_PAIR = LANES // HEAD_DIM
BF16_ROWS = 16
VMEM_LIMIT = 56 * 1024 * 1024


def _cparams(*sem):
    return pltpu.CompilerParams(dimension_semantics=sem, vmem_limit_bytes=VMEM_LIMIT)


def _sigmoid(x):
    return 1.0 / (1.0 + jnp.exp(-x))


def _dot_nt(a, b, precision=None):
    return lax.dot_general(a, b, (((1,), (1,)), ((), ())), precision=precision, preferred_element_type=F32)


def _dot_tn(a, b, precision=None):
    return lax.dot_general(a, b, (((0,), (0,)), ((), ())), precision=precision, preferred_element_type=F32)


def _dot(a, b, precision=None):
    return jnp.dot(a, b, precision=precision, preferred_element_type=F32)


def _ada_kernel(c_ref, w_ref, b_ref, o_ref):
    c = c_ref[...]
    o_ref[0] = _dot(c * _sigmoid(c), w_ref[0], precision=HI) + b_ref[0]


def _ada_call(c_all, w_ada, b_ada):
    n = c_all.shape[0]
    depth, d, cols = w_ada.shape
    tn = 768
    return pl.pallas_call(
        _ada_kernel,
        grid=(depth, cols // tn),
        in_specs=[pl.BlockSpec((n, d), lambda l, j: (0, 0)),
                  pl.BlockSpec((1, d, tn), lambda l, j: (l, 0, j)),
                  pl.BlockSpec((1, 1, tn), lambda l, j: (l, 0, j))],
        out_specs=pl.BlockSpec((1, n, tn), lambda l, j: (l, 0, j)),
        out_shape=jax.ShapeDtypeStruct((depth, n, cols), F32),
        compiler_params=_cparams("parallel", "parallel"),
        name="ada_mod",
    )(c_all, w_ada, b_ada.reshape(depth, 1, cols))


def _bucket_of(dist):
    d = jnp.maximum(dist, 0)
    log_ratio = jnp.log(jnp.maximum(d, 1).astype(F32) / MAX_EXACT) / math.log(MAX_DISTANCE / MAX_EXACT)
    large = jnp.minimum(MAX_EXACT + (log_ratio * (N_BUCKETS - MAX_EXACT)).astype(I32), N_BUCKETS - 1)
    return jnp.where(d < MAX_EXACT, d, large)


def _bias_kernel(rb_ref, *o_refs, offsets, key_major):
    for o_ref, off, km in zip(o_refs, offsets, key_major):
        n_h, rows, cols = o_ref.shape
        dist = (lax.broadcasted_iota(I32, (rows, cols), 0) - lax.broadcasted_iota(I32, (rows, cols), 1))
        dist = (-dist if km else dist) + off
        bucket = _bucket_of(dist)
        for h in range(n_h):
            acc = jnp.zeros((rows, cols), F32)
            for b in range(N_BUCKETS):
                acc = jnp.where(bucket == b, rb_ref[b, h], acc)
            o_ref[h] = jnp.where(dist >= 0, acc, NEG_INF)


def _bias_call(rel_bias, shapes, offsets, key_major=None):
    n_h = rel_bias.shape[1]
    key_major = tuple(key_major) if key_major is not None else (False,) * len(shapes)
    return pl.pallas_call(
        functools.partial(_bias_kernel, offsets=tuple(offsets), key_major=key_major),
        in_specs=[pl.BlockSpec(memory_space=pltpu.SMEM)],
        out_specs=[pl.BlockSpec(memory_space=pltpu.VMEM) for _ in shapes],
        out_shape=[jax.ShapeDtypeStruct((n_h,) + s, F32) for s in shapes],
        name="rel_bias_tables",
    )(rel_bias)


def _inproj_kernel(x_ref, sc_ref, sh_ref, w_ref, oa_ref, or_ref, *, chunk):
    u = (x_ref[0] * (1.0 + sc_ref[0]) + sh_ref[0]).astype(BF16)
    att = oa_ref.shape[-1]
    for c0 in range(0, w_ref.shape[1], chunk):
        res = _dot(u, w_ref[:, c0:c0 + chunk])
        if c0 < att:
            oa_ref[0, :, c0:c0 + chunk] = res
        else:
            or_ref[0, :, c0 - att:c0 - att + chunk] = res


def _mod_spec(mod, tm):
    if mod.shape[1] == 1:
        return pl.BlockSpec((1, 1, mod.shape[2]), lambda n, i: (n, 0, 0))
    return pl.BlockSpec((1, tm, mod.shape[2]), lambda n, i: (n, i, 0))


def _inproj_call(x, sc, sh, w_bf, tm):
    n, l, d = x.shape
    cols = w_bf.shape[1]
    return pl.pallas_call(
        functools.partial(_inproj_kernel, chunk=256),
        grid=(n, l // tm),
        in_specs=[pl.BlockSpec((1, tm, d), lambda n, i: (n, i, 0)),
                  _mod_spec(sc, tm), _mod_spec(sh, tm),
                  pl.BlockSpec((d, cols), lambda n, i: (0, 0))],
        out_specs=[pl.BlockSpec((1, tm, ATT_IN), lambda n, i: (n, i, 0)),
                   pl.BlockSpec((1, tm, RW_IN), lambda n, i: (n, i, 0))],
        out_shape=[jax.ShapeDtypeStruct((n, l, ATT_IN), F32),
                   jax.ShapeDtypeStruct((n, l, RW_IN), F32)],
        compiler_params=_cparams("parallel", "parallel"),
        name="in_proj",
    )(x, sc, sh, w_bf)


def _moba_prompt_kernel(rb_ref, q_ref, k_ref, v_ref, town_ref, tprev_ref, o_ref, kb_ref, vb_ref, km_ref,
                        *, n_blk):
    hp = pl.program_id(1)
    i = pl.program_id(2)
    blk = MOBA_BLOCK

    @pl.when(i == 0)
    def _():
        k = k_ref[0]
        kb_ref[...] = k.astype(BF16)
        km_ref[...] = jnp.sum(k.reshape(n_blk, blk, LANES), axis=1) * (1.0 / blk)
        for c in range(n_blk):
            vt = v_ref[0, c * blk:(c + 1) * blk, :].T.astype(BF16)
            for h2 in range(HEAD_PAIR):
                vb_ref[h2, :HEAD_DIM, c * blk:(c + 1) * blk] = vt[h2 * HEAD_DIM:(h2 + 1) * HEAD_DIM]
                vb_ref[h2, HEAD_DIM:, c * blk:(c + 1) * blk] = jnp.ones((BF16_ROWS, blk), BF16)

    qt = q_ref[0].T
    dim_row = lax.broadcasted_iota(I32, qt.shape, 0)
    own0 = pl.multiple_of(i * blk, blk)
    heads = range(HEAD_PAIR)
    bidx = lax.broadcasted_iota(I32, (n_blk, blk), 0)
    past = bidx < i
    qt_h = [jnp.where((dim_row // HEAD_DIM) == h2, qt, 0.0) for h2 in heads]
    qb = [(z * ATT_SCALE).astype(BF16) for z in qt_h]
    sel = []
    for h2 in heads:
        g = jnp.where(past, _dot(km_ref[...], qt_h[h2], precision=HI), NEG_INF)
        rank = jnp.zeros(g.shape, I32)
        for jp in range(n_blk):
            gj = g[jp:jp + 1, :]
            rank = rank + ((gj > g) | ((gj == g) & (jp < bidx))).astype(I32)
        sel.append(jnp.where((rank < MOBA_TOPK) & past, 1.0, 0.0))
    b_far = [rb_ref[N_BUCKETS - 1, hp * HEAD_PAIR + h2] for h2 in heads]

    def attend(blocks, carry):
        s = []
        for r0, tables, rows in blocks:
            kb = kb_ref[pl.ds(r0, blk), :]
            s_b = [_dot(kb, qb[h2]) for h2 in heads]
            if tables is not None:
                s_b = [s_b[h2] + tables[h2] for h2 in heads]
            if rows is not None:
                s_b = [s_b[h2] + rows[h2] for h2 in heads]
            s.append(s_b)
        out = []
        for h2 in heads:
            m_new = functools.reduce(jnp.maximum, [jnp.max(s_b[h2], axis=0, keepdims=True) for s_b in s])
            if carry is not None:
                m, acc = carry[h2]
                m_new = jnp.maximum(m, m_new)
                acc = jnp.exp(m - m_new) * acc
            else:
                acc = 0.0
            for (r0, _, _), s_b in zip(blocks, s):
                p = jnp.exp(s_b[h2] - m_new).astype(BF16)
                acc = acc + _dot(vb_ref[h2, :, pl.ds(r0, blk)], p)
            out.append((m_new, acc))
        return tuple(out)

    def picked_row(j, h2, value):
        hit = jnp.sum(jnp.where(bidx == j, sel[h2], 0.0), axis=0, keepdims=True) > 0.5
        return jnp.where(hit, value, NEG_INF)

    def far_block(j):
        return pl.multiple_of(j * blk, blk), None, [picked_row(j, h2, b_far[h2]) for h2 in heads]

    def near_block(j):
        return pl.multiple_of(j * blk, blk), [tprev_ref[h2] for h2 in heads], [picked_row(j, h2, 0.0) for h2 in heads]

    carry = attend([(own0, [town_ref[h2] for h2 in heads], None)], None)
    n_far = jnp.maximum(i - 1, 0)
    grp = PROMPT_KEY_GROUP
    carry = lax.fori_loop(0, n_far // grp,
                          lambda j, c: attend([far_block(grp * j + u) for u in range(grp)], c), carry)
    carry = lax.fori_loop(grp * (n_far // grp), n_far, lambda j, c: attend([far_block(j)], c), carry)
    carry = lax.fori_loop(n_far, i, lambda j, c: attend([near_block(j)], c), carry)
    o_ref[0] = jnp.concatenate([acc[:HEAD_DIM] / acc[HEAD_DIM:HEAD_DIM + 1] for _, acc in carry], axis=0).T


def _moba_prompt_call(p_att, rel_bias, t_own, t_prev):
    n, l, _ = p_att.shape
    n_blk = l // MOBA_BLOCK
    n_hp = H_ATT // HEAD_PAIR
    tbl = pl.BlockSpec((HEAD_PAIR, MOBA_BLOCK, MOBA_BLOCK), lambda n, h, i: (h, 0, 0))
    return pl.pallas_call(
        functools.partial(_moba_prompt_kernel, n_blk=n_blk),
        grid=(n, n_hp, n_blk),
        in_specs=[pl.BlockSpec(memory_space=pltpu.SMEM),
                  pl.BlockSpec((1, MOBA_BLOCK, LANES), lambda n, h, i: (n, i, h)),
                  pl.BlockSpec((1, l, LANES), lambda n, h, i: (n, 0, n_hp + h)),
                  pl.BlockSpec((1, l, LANES), lambda n, h, i: (n, 0, 2 * n_hp + h)),
                  tbl, tbl],
        out_specs=pl.BlockSpec((1, MOBA_BLOCK, LANES), lambda n, h, i: (n, i, h)),
        out_shape=jax.ShapeDtypeStruct((n, l, D_ATT), F32),
        scratch_shapes=[pltpu.VMEM((l, LANES), BF16), pltpu.VMEM((HEAD_PAIR, HEAD_DIM + BF16_ROWS, l), BF16),
                        pltpu.VMEM((n_blk, LANES), F32)],
        compiler_params=_cparams("parallel", "parallel", "arbitrary"),
        name="moba_prompt",
    )(rel_bias, p_att, p_att, p_att, t_own, t_prev)


def _softplus(x):
    return jnp.maximum(x, 0.0) + jnp.log(1.0 + jnp.exp(-jnp.abs(x)))


def _rwkv_pre_kernel(p_ref, prev_ref, sh0_ref, mu_ref, w0_ref, w2_ref, a0_ref, a2_ref, g2_ref, kkw_ref, kaw_ref,
                     r_o, lw_o, k_o, v_o, kk_o, b_o, g_o):
    i = pl.program_id(1)
    p = p_ref[0]
    prev_row = jnp.where(i == 0, sh0_ref[0], prev_ref[0][7:8])
    row = lax.broadcasted_iota(I32, p.shape, 0)
    p_prev = jnp.where(row == 0, prev_row, pltpu.roll(p, 1, 0))
    xm = p + (p_prev - p) * mu_ref[...]
    o = 3 * D_RWKV
    r = xm[:, :D_RWKV]
    k = xm[:, D_RWKV:2 * D_RWKV]
    v = xm[:, 2 * D_RWKV:o]
    xw = xm[:, o:o + LORA_W]
    xa = xm[:, o + LORA_W:o + LORA_W + LORA_A]
    xg = xm[:, o + LORA_W + LORA_A:]
    w_log = -_softplus(-(w0_ref[...] + _dot(jnp.tanh(xw), w2_ref[...], precision=HI))) - 0.5
    a = _sigmoid(a0_ref[...] + _dot(xa, a2_ref[...], precision=HI))
    g = _dot(_sigmoid(xg), g2_ref[...], precision=HI)
    kk = k * kkw_ref[...]
    same_head = (lax.broadcasted_iota(I32, (D_RWKV, D_RWKV), 0) // HEAD_DIM
                 == lax.broadcasted_iota(I32, (D_RWKV, D_RWKV), 1) // HEAD_DIM)
    ss = _dot(kk * kk, jnp.where(same_head, 1.0, 0.0), precision=HI)
    kk = kk / jnp.maximum(jnp.sqrt(ss), 1e-12)
    r_o[0] = r
    lw_o[0] = -jnp.exp(w_log)
    k_o[0] = k * (1.0 + (a - 1.0) * kaw_ref[...])
    v_o[0] = v
    kk_o[0] = kk
    b_o[0] = kk * a
    g_o[0] = g


def _rwkv_pre_call(p_rw, shift0, mu, w0, w2, a0, a2, g2, k_k, k_a, tc):
    n, l, _ = p_rw.shape
    row = lambda z: z.reshape(1, -1)
    full = lambda z: pl.BlockSpec(z.shape, lambda n, i: (0,) * z.ndim)
    params = [row(mu), row(w0), w2, row(a0), a2, g2, row(k_k), row(k_a)]
    out_spec = pl.BlockSpec((1, tc, D_RWKV), lambda n, i: (n, i, 0))
    return pl.pallas_call(
        _rwkv_pre_kernel,
        grid=(n, l // tc),
        in_specs=[pl.BlockSpec((1, tc, RW_IN), lambda n, i: (n, i, 0)),
                  pl.BlockSpec((1, 8, RW_IN), lambda n, i: (n, jnp.maximum(i * (tc // 8) - 1, 0), 0)),
                  pl.BlockSpec((1, 1, RW_IN), lambda n, i: (n, 0, 0))] + [full(z) for z in params],
        out_specs=[out_spec] * 7,
        out_shape=[jax.ShapeDtypeStruct((n, l, D_RWKV), F32)] * 7,
        compiler_params=_cparams("parallel", "parallel"),
        name="rwkv_pre",
    )(p_rw, p_rw, shift0.reshape(n, 1, RW_IN), *params)


def _chunk_cumsum(x):
    rows = x.shape[0]
    row = lax.broadcasted_iota(I32, x.shape, 0)
    s = 1
    while s < rows:
        x = x + jnp.where(row >= s, pltpu.roll(x, s, 0), 0.0)
        s *= 2
    return x


def _rwkv_scan_kernel(r_ref, lw_ref, k_ref, v_ref, kk_ref, b_ref, g_ref, s0_ref, rk_ref, lg_ref, lb_ref,
                      y_ref, sf_ref, st_ref, yraw_ref, *, chunk, unroll, mxu):
    i = pl.program_id(2)
    C = chunk
    tl = r_ref.shape[1]
    n_dbl = max(C.bit_length() - 2, 0)

    @pl.when(i == 0)
    def _():
        st_ref[...] = s0_ref[0, 0]

    lane = lax.broadcasted_iota(I32, (1, LANES), 1)
    head0 = lane < HEAD_DIM
    ti = lax.broadcasted_iota(I32, (C, C), 0)
    si = lax.broadcasted_iota(I32, (C, C), 1)
    strict = si < ti
    incl = si <= ti
    eye_c = jnp.where(si == ti, 1.0, 0.0)
    rj = lax.broadcasted_iota(I32, (LANES, LANES), 0)
    cj = lax.broadcasted_iota(I32, (LANES, LANES), 1)
    same_head = (rj // HEAD_DIM) == (cj // HEAD_DIM)
    eye_l = rj == cj
    mm = lambda a, b: _dot(a.astype(mxu), b.astype(mxu))
    mm_nt = lambda a, b: _dot_nt(a.astype(mxu), b.astype(mxu))
    mm_tn = lambda a, b: _dot_tn(a.astype(mxu), b.astype(mxu))

    def chunk_operands(rows):
        lw = lw_ref[0, rows, :]
        r = r_ref[0, rows, :]
        k = k_ref[0, rows, :]
        v = v_ref[0, rows, :]
        kk = kk_ref[0, rows, :]
        b = b_ref[0, rows, :]
        cum = _chunk_cumsum(lw)
        c_end = cum[C - 1:C, :]
        e_neg = jnp.exp(-cum)
        e_end = jnp.exp(c_end - cum)
        al = -kk * jnp.exp(cum - lw)
        rt = r * jnp.exp(cum)
        return dict(v=v, al=al, rt=rt, bt=b * e_neg, kt=k * e_neg, bt2=b * e_end, kt2=k * e_end,
                    lhs=jnp.concatenate([al, rt], axis=0), c_end=c_end)

    def group_terms(rows_list):
        ops = [chunk_operands(rw) for rw in rows_list]
        ch = [(u, h) for u in range(len(ops)) for h in range(HEAD_PAIR)]
        mine = [head0, jnp.logical_not(head0)]
        lhs = [jnp.where(mine[h], ops[u]["lhs"], 0.0) for u, h in ch]
        ab = [mm_nt(lhs[n], ops[u]["bt"]) for n, (u, h) in enumerate(ch)]
        ak = [mm_nt(lhs[n], ops[u]["kt"]) for n, (u, h) in enumerate(ch)]
        m_ab = [jnp.where(strict, z[:C], 0.0) for z in ab]
        m_ak = [jnp.where(strict, z[:C], 0.0) for z in ak]
        b_br = [jnp.where(incl, z[C:], 0.0) for z in ab]
        b_kr = [jnp.where(incl, z[C:], 0.0) for z in ak]
        t_inv = [eye_c + z for z in m_ab]
        pk = m_ab
        for _ in range(n_dbl):
            pk = [mm(z, z) for z in pk]
            t_inv = [t + mm(z, t) for z, t in zip(pk, t_inv)]
        mv = [mm(m_ak[n], ops[u]["v"]) for n, (u, h) in enumerate(ch)]
        aw = [mm(t_inv[n], jnp.concatenate([ops[u]["al"], mv[n]], axis=1)) for n, (u, h) in enumerate(ch)]
        baw = [mm(b_br[n], aw[n]) for n in range(len(ch))]
        bkv = [mm(b_kr[n], ops[u]["v"]) for n, (u, h) in enumerate(ch)]
        terms = []
        for u, o in enumerate(ops):
            n0, n1 = u * HEAD_PAIR, u * HEAD_PAIR + 1
            pick = lambda f: jnp.where(head0, f(n0), f(n1))
            a2 = pick(lambda n: aw[n][:, :LANES])
            w = pick(lambda n: aw[n][:, LANES:])
            r2 = o["rt"] + pick(lambda n: baw[n][:, :LANES])
            y0 = pick(lambda n: baw[n][:, LANES:] + bkv[n])
            tc_t = jnp.where(same_head, mm_tn(o["bt2"], a2), 0.0)
            gc_t = jnp.where(same_head, mm_tn(jnp.concatenate([o["bt2"], o["kt2"]], axis=0),
                                              jnp.concatenate([w, o["v"]], axis=0)), 0.0)
            dec_col = jnp.sum(jnp.where(eye_l, jnp.exp(o["c_end"]), 0.0), axis=1, keepdims=True)
            terms.append((r2, y0, tc_t, gc_t, dec_col))
        return terms

    def group_body(c, carry):
        rows = [pl.ds(pl.multiple_of((c * unroll + u) * C, C), C) for u in range(unroll)]
        terms = group_terms(rows)
        st = st_ref[...]
        for rw, (r2, y0, tc_t, gc_t, dec_col) in zip(rows, terms):
            yraw_ref[rw, :] = mm(r2, st) + y0
            st = dec_col * st + mm(tc_t, st) + gc_t
        st_ref[...] = st
        return carry

    lax.fori_loop(0, tl // (C * unroll), group_body, 0)
    sf_ref[0, 0] = st_ref[...]

    def head_sum(z):
        s0 = jnp.sum(jnp.where(head0, z, 0.0), axis=1, keepdims=True)
        s1 = jnp.sum(jnp.where(head0, 0.0, z), axis=1, keepdims=True)
        return jnp.where(head0, s0, s1)

    y = yraw_ref[...]
    d = y - head_sum(y) * (1.0 / HEAD_DIM)
    var = head_sum(d * d) * (1.0 / HEAD_DIM)
    yn = d * lax.rsqrt(var + LNX_EPS) * lg_ref[...] + lb_ref[...]
    bonus = head_sum(r_ref[0] * k_ref[0] * rk_ref[...]) * v_ref[0]
    y_ref[0] = (yn + bonus) * g_ref[0]


def _rwkv_scan_call(ops, st0, r_k, lnx_g, lnx_b, tl, chunk, unroll):
    n, l, _ = ops[0].shape
    n_hp = H_RWKV // HEAD_PAIR
    mxu = BF16 if chunk >= 16 else F32
    tok = pl.BlockSpec((1, tl, LANES), lambda n, h, i: (n, i, h))
    par = pl.BlockSpec((1, LANES), lambda n, h, i: (0, h))
    st_spec = pl.BlockSpec((1, 1, LANES, LANES), lambda n, h, i: (n, h, 0, 0))
    return pl.pallas_call(
        functools.partial(_rwkv_scan_kernel, chunk=chunk, unroll=unroll, mxu=mxu),
        grid=(n, n_hp, l // tl),
        in_specs=[tok] * 7 + [st_spec, par, par, par],
        out_specs=[tok, st_spec],
        out_shape=[jax.ShapeDtypeStruct((n, l, D_RWKV), F32),
                   jax.ShapeDtypeStruct((n, n_hp, LANES, LANES), F32)],
        scratch_shapes=[pltpu.VMEM((LANES, LANES), F32), pltpu.VMEM((tl, LANES), F32)],
        compiler_params=_cparams("parallel", "parallel", "arbitrary"),
        name="rwkv_scan",
    )(*ops, st0, r_k.reshape(1, D_RWKV), lnx_g.reshape(1, D_RWKV), lnx_b.reshape(1, D_RWKV))


def _state_to_pairs(s):
    n, h = s.shape[:2]
    st = jnp.swapaxes(s, -1, -2).reshape(n, h // HEAD_PAIR, HEAD_PAIR, HEAD_DIM, HEAD_DIM)
    z = jnp.zeros_like(st[:, :, 0])
    top = jnp.concatenate([st[:, :, 0], z], axis=-1)
    bot = jnp.concatenate([z, st[:, :, 1]], axis=-1)
    return jnp.concatenate([top, bot], axis=-2)


def _pairs_to_state(sp):
    n, n_hp = sp.shape[:2]
    d0 = sp[:, :, :HEAD_DIM, :HEAD_DIM]
    d1 = sp[:, :, HEAD_DIM:, HEAD_DIM:]
    st = jnp.stack([d0, d1], axis=2).reshape(n, n_hp * HEAD_PAIR, HEAD_DIM, HEAD_DIM)
    return jnp.swapaxes(st, -1, -2)


def _rwkv_mix(p_rw, shift0, s0, prm, tc, tl, chunk, unroll=1):
    ops = _rwkv_pre_call(p_rw, shift0, prm["mu"], prm["w0"], prm["w2"], prm["a0"], prm["a2"], prm["g2"],
                         prm["k_k"], prm["k_a"], tc)
    y, sp = _rwkv_scan_call(ops, _state_to_pairs(s0), prm["r_k"], prm["lnx_g"], prm["lnx_b"], tl, chunk, unroll)
    return y, _pairs_to_state(sp), p_rw[:, -1]


def _layer_norm(z, g, b):
    mu = jnp.mean(z, axis=-1, keepdims=True)
    d = z - mu
    var = jnp.mean(d * d, axis=-1, keepdims=True)
    return d * lax.rsqrt(var + LN_EPS) * g + b


def _outproj_kernel(ya_ref, yr_ref, x_ref, gt_ref, w_ref, g_ref, b_ref, o_ref):
    o = (_dot(ya_ref[0].astype(BF16), w_ref[:D_ATT, :]) + _dot(yr_ref[0].astype(BF16), w_ref[D_ATT:, :]))
    o_ref[0] = _layer_norm(ALPHA * x_ref[0] + (1.0 + gt_ref[0]) * o, g_ref[...], b_ref[...])


def _outproj_call(y_att, y_rw, x, gt, w_bf, ln_g, ln_b, tm):
    n, l, d = x.shape
    row = pl.BlockSpec((1, d), lambda n, i: (0, 0))
    return pl.pallas_call(
        _outproj_kernel,
        grid=(n, l // tm),
        in_specs=[pl.BlockSpec((1, tm, D_ATT), lambda n, i: (n, i, 0)),
                  pl.BlockSpec((1, tm, D_RWKV), lambda n, i: (n, i, 0)),
                  pl.BlockSpec((1, tm, d), lambda n, i: (n, i, 0)),
                  _mod_spec(gt, tm),
                  pl.BlockSpec(w_bf.shape, lambda n, i: (0, 0)), row, row],
        out_specs=pl.BlockSpec((1, tm, d), lambda n, i: (n, i, 0)),
        out_shape=jax.ShapeDtypeStruct((n, l, d), F32),
        compiler_params=_cparams("parallel", "parallel"),
        name="out_proj_ln",
    )(y_att, y_rw, x, gt, w_bf, ln_g.reshape(1, d), ln_b.reshape(1, d))


def _ffn_kernel(x_ref, sc_ref, sh_ref, gt_ref, wr_ref, wg_ref, wu_ref, wd_ref, g_ref, b_ref, o_ref,
                u_ref, acc_ref, gate_ref, *, n_experts):
    e = pl.program_id(2)
    f = pl.program_id(3)
    last = (e == pl.num_programs(2) - 1) & (f == pl.num_programs(3) - 1)

    @pl.when((e == 0) & (f == 0))
    def _():
        u = x_ref[0] * (1.0 + sc_ref[0]) + sh_ref[0]
        u_ref[...] = u.astype(BF16)
        acc_ref[...] = jnp.zeros_like(acc_ref)
        if n_experts > 1:
            logits = _dot(u, wr_ref[...], precision=HI)
            lane = lax.broadcasted_iota(I32, logits.shape, 1)
            logits = jnp.where(lane < n_experts, logits, NEG_INF)
            v1 = jnp.max(logits, axis=1, keepdims=True)
            i1 = jnp.min(jnp.where(logits == v1, lane, LANES), axis=1, keepdims=True)
            rest = jnp.where(lane == i1, NEG_INF, logits)
            v2 = jnp.max(rest, axis=1, keepdims=True)
            i2 = jnp.min(jnp.where(rest == v2, lane, LANES), axis=1, keepdims=True)
            w2 = 1.0 / (1.0 + jnp.exp(v1 - v2))
            gate_ref[...] = jnp.where(lane == i1, 1.0 - w2, 0.0) + jnp.where(lane == i2, w2, 0.0)

    u = u_ref[...]
    hg = _dot(u, wg_ref[0])
    hu = _dot(u, wu_ref[0])
    h = hg * _sigmoid(hg) * hu
    if n_experts > 1:
        lane = lax.broadcasted_iota(I32, gate_ref.shape, 1)
        h = h * jnp.sum(jnp.where(lane == e, gate_ref[...], 0.0), axis=1, keepdims=True)
    acc_ref[...] += _dot(h.astype(BF16), wd_ref[0])

    @pl.when(last)
    def _():
        o_ref[0] = _layer_norm(ALPHA * x_ref[0] + (1.0 + gt_ref[0]) * acc_ref[...], g_ref[...], b_ref[...])


def _ffn_call(x, sc, sh, gt, w_router, wg_bf, wu_bf, wd_bf, ln_g, ln_b, tm, tf):
    n, l, d = x.shape
    n_e, _, d_ff = wg_bf.shape
    row = pl.BlockSpec((1, d), lambda n, i, e, f: (0, 0))
    mod = lambda m: (pl.BlockSpec((1, 1, d), lambda n, i, e, f: (n, 0, 0)) if m.shape[1] == 1
                     else pl.BlockSpec((1, tm, d), lambda n, i, e, f: (n, i, 0)))
    return pl.pallas_call(
        functools.partial(_ffn_kernel, n_experts=n_e),
        grid=(n, l // tm, n_e, d_ff // tf),
        in_specs=[pl.BlockSpec((1, tm, d), lambda n, i, e, f: (n, i, 0)),
                  mod(sc), mod(sh), mod(gt),
                  pl.BlockSpec(w_router.shape, lambda n, i, e, f: (0, 0)),
                  pl.BlockSpec((1, d, tf), lambda n, i, e, f: (e, 0, f)),
                  pl.BlockSpec((1, d, tf), lambda n, i, e, f: (e, 0, f)),
                  pl.BlockSpec((1, tf, d), lambda n, i, e, f: (e, f, 0)),
                  row, row],
        out_specs=pl.BlockSpec((1, tm, d), lambda n, i, e, f: (n, i, 0)),
        out_shape=jax.ShapeDtypeStruct((n, l, d), F32),
        scratch_shapes=[pltpu.VMEM((tm, d), BF16), pltpu.VMEM((tm, d), F32), pltpu.VMEM((tm, LANES), F32)],
        compiler_params=_cparams("parallel", "parallel", "arbitrary", "arbitrary"),
        name="ffn_ln" if n_e == 1 else "moe_ln",
    )(x, sc, sh, gt, w_router, wg_bf, wu_bf, wd_bf, ln_g.reshape(1, d), ln_b.reshape(1, d))


ROUTE_BLOCK = 256


def _router_kernel(x_ref, sc_ref, sh_ref, wr_ref, gate_o, pos_o, post_o, cnt_o, *, n_experts):
    u = x_ref[0] * (1.0 + sc_ref[0]) + sh_ref[0]
    tm = u.shape[0]
    logits = _dot(u, wr_ref[...], precision=HI)
    lane = lax.broadcasted_iota(I32, logits.shape, 1)
    logits = jnp.where(lane < n_experts, logits, NEG_INF)
    v1 = jnp.max(logits, axis=1, keepdims=True)
    i1 = jnp.min(jnp.where(logits == v1, lane, LANES), axis=1, keepdims=True)
    rest = jnp.where(lane == i1, NEG_INF, logits)
    v2 = jnp.max(rest, axis=1, keepdims=True)
    i2 = jnp.min(jnp.where(rest == v2, lane, LANES), axis=1, keepdims=True)
    w2 = 1.0 / (1.0 + jnp.exp(v1 - v2))
    gate_o[0] = jnp.where(lane == i1, 1.0 - w2, 0.0) + jnp.where(lane == i2, w2, 0.0)
    chosen = (lane == i1) | (lane == i2)
    sel = jnp.where(chosen, 1.0, 0.0)
    rb = ROUTE_BLOCK
    tri = jnp.where(lax.broadcasted_iota(I32, (rb, rb), 1) < lax.broadcasted_iota(I32, (rb, rb), 0), 1.0, 0.0).astype(BF16)
    carry = jnp.zeros((1, LANES), F32)
    for c in range(tm // rb):
        blk = sel[c * rb:(c + 1) * rb]
        pos = _dot(tri, blk.astype(BF16)) + carry
        pos = jnp.where(chosen[c * rb:(c + 1) * rb], pos, -1.0)
        pos_o[0, c * rb:(c + 1) * rb, :] = pos
        post_o[0, :, c * rb:(c + 1) * rb] = pos.T
        carry = carry + jnp.sum(blk, axis=0, keepdims=True)
    cnt_o[0, 0] = jnp.broadcast_to(carry, (8, LANES)).astype(I32)


def _router_call(x, sc, sh, w_router, n_experts, tm):
    n, l, d = x.shape
    mod = pl.BlockSpec((1, 1, d), lambda n, i: (n, 0, 0))
    tok = pl.BlockSpec((1, tm, LANES), lambda n, i: (n, i, 0))
    return pl.pallas_call(
        functools.partial(_router_kernel, n_experts=n_experts),
        grid=(n, l // tm),
        in_specs=[pl.BlockSpec((1, tm, d), lambda n, i: (n, i, 0)), mod, mod,
                  pl.BlockSpec(w_router.shape, lambda n, i: (0, 0))],
        out_specs=[tok, tok, pl.BlockSpec((1, LANES, tm), lambda n, i: (n, 0, i)),
                   pl.BlockSpec((1, 1, 8, LANES), lambda n, i: (n, i, 0, 0))],
        out_shape=[jax.ShapeDtypeStruct((n, l, LANES), F32), jax.ShapeDtypeStruct((n, l, LANES), F32),
                   jax.ShapeDtypeStruct((n, LANES, l), F32), jax.ShapeDtypeStruct((n, l // tm, 8, LANES), I32)],
        compiler_params=_cparams("parallel", "parallel"),
        name="moe_router",
    )(x, sc, sh, w_router)


def _moe_routed_kernel(cnt_ref, x_ref, sc_ref, sh_ref, gt_ref, gate_ref, pos_ref, post_ref, wg_ref, wu_ref, wd_ref,
                       g_ref, b_ref, o_ref, u_ref, xs_ref, oacc_ref, *, n_experts, sub):
    n = pl.program_id(0)
    i = pl.program_id(1)
    e = pl.program_id(2)
    f = pl.program_id(3)
    n_f = pl.num_programs(3)
    tm = x_ref.shape[1]
    count = cnt_ref[(n * pl.num_programs(1) + i) * n_experts + e]
    n_sub = (count + sub - 1) // sub

    @pl.when((e == 0) & (f == 0))
    def _():
        u_ref[...] = (x_ref[0] * (1.0 + sc_ref[0]) + sh_ref[0]).astype(BF16)
        o_ref[...] = jnp.zeros_like(o_ref)

    @pl.when(f == 0)
    def _():
        erow = lax.broadcasted_iota(I32, (8, tm), 0)
        pos_row = jnp.sum(jnp.where(erow == e, post_ref[0, 0:8, :], 0.0), axis=0, keepdims=True)

        def gather(s, c):
            rows = pl.ds(pl.multiple_of(s * sub, BF16_ROWS), sub)
            want = (lax.broadcasted_iota(I32, (sub, tm), 0) + s * sub).astype(F32)
            onehot = jnp.where(pos_row == want, 1.0, 0.0).astype(BF16)
            xs_ref[rows, :] = _dot(onehot, u_ref[...]).astype(BF16)
            oacc_ref[rows, :] = jnp.zeros((sub, oacc_ref.shape[1]), F32)
            return c

        lax.fori_loop(0, n_sub, gather, 0)

    def expert(s, c):
        rows = pl.ds(pl.multiple_of(s * sub, BF16_ROWS), sub)
        xs = xs_ref[rows, :]
        hg = _dot(xs, wg_ref[0])
        h = hg * _sigmoid(hg) * _dot(xs, wu_ref[0])
        oacc_ref[rows, :] += _dot(h.astype(BF16), wd_ref[0])
        return c

    lax.fori_loop(0, n_sub, expert, 0)

    @pl.when(f == n_f - 1)
    def _():
        lane = lax.broadcasted_iota(I32, (tm, LANES), 1)
        pos_col = jnp.sum(jnp.where(lane == e, pos_ref[0], 0.0), axis=1, keepdims=True)
        gate_col = jnp.sum(jnp.where(lane == e, gate_ref[0], 0.0), axis=1, keepdims=True)

        def scatter(s, c):
            rows = pl.ds(pl.multiple_of(s * sub, BF16_ROWS), sub)
            want = (lax.broadcasted_iota(I32, (tm, sub), 1) + s * sub).astype(F32)
            w = jnp.where(pos_col == want, gate_col, 0.0).astype(BF16)
            o_ref[0] += _dot(w, oacc_ref[rows, :].astype(BF16))
            return c

        lax.fori_loop(0, n_sub, scatter, 0)

    @pl.when((e == n_experts - 1) & (f == n_f - 1))
    def _():
        o_ref[0] = _layer_norm(ALPHA * x_ref[0] + (1.0 + gt_ref[0]) * o_ref[0], g_ref[...], b_ref[...])


def _moe_routed_call(x, sc, sh, gt, w_router, wg_bf, wu_bf, wd_bf, ln_g, ln_b, tm, tf, sub):
    n, l, d = x.shape
    n_e, _, d_ff = wg_bf.shape
    gate, pos, post, cnt = _router_call(x, sc, sh, w_router, n_e, tm)
    cnt_flat = cnt[:, :, 0, :n_e].reshape(-1)
    n_sub_max = -(-tm // sub)
    row = pl.BlockSpec((1, d), lambda n, i, e, f, c: (0, 0))
    mod = pl.BlockSpec((1, 1, d), lambda n, i, e, f, c: (n, 0, 0))
    tok = pl.BlockSpec((1, tm, LANES), lambda n, i, e, f, c: (n, i, 0))
    return pl.pallas_call(
        functools.partial(_moe_routed_kernel, n_experts=n_e, sub=sub),
        grid_spec=pltpu.PrefetchScalarGridSpec(
            num_scalar_prefetch=1,
            grid=(n, l // tm, n_e, d_ff // tf),
            in_specs=[pl.BlockSpec((1, tm, d), lambda n, i, e, f, c: (n, i, 0)), mod, mod, mod, tok, tok,
                      pl.BlockSpec((1, LANES, tm), lambda n, i, e, f, c: (n, 0, i)),
                      pl.BlockSpec((1, d, tf), lambda n, i, e, f, c: (e, 0, f)),
                      pl.BlockSpec((1, d, tf), lambda n, i, e, f, c: (e, 0, f)),
                      pl.BlockSpec((1, tf, d), lambda n, i, e, f, c: (e, f, 0)),
                      row, row],
            out_specs=pl.BlockSpec((1, tm, d), lambda n, i, e, f, c: (n, i, 0)),
            scratch_shapes=[pltpu.VMEM((tm, d), BF16), pltpu.VMEM((n_sub_max * sub, d), BF16),
                            pltpu.VMEM((n_sub_max * sub, d), F32)]),
        out_shape=jax.ShapeDtypeStruct((n, l, d), F32),
        compiler_params=_cparams("parallel", "parallel", "arbitrary", "arbitrary"),
        name="moe_routed_ln",
    )(cnt_flat, x, sc, sh, gt, gate, pos, post, wg_bf, wu_bf, wd_bf, ln_g.reshape(1, d), ln_b.reshape(1, d))


PAGES_PER_BLOCK = MOBA_BLOCK // PAGE_SIZE
KMEAN_PAGES = 32


def _kmean_kernel(pt_ref, *refs):
    o_ref = refs[-1]
    g = pl.program_id(1)

    @pl.when(g == 0)
    def _():
        o_ref[...] = jnp.zeros_like(o_ref)

    lane = lax.broadcasted_iota(I32, o_ref.shape[1:], 1)
    out = o_ref[0]
    for blk in range(KMEAN_PAGES // PAGES_PER_BLOCK):
        tot = functools.reduce(lambda a, c: a + c,
                               [refs[blk * PAGES_PER_BLOCK + r][...] for r in range(PAGES_PER_BLOCK)])
        col = jnp.sum(tot, axis=-1, keepdims=True).reshape(-1, 1) * (1.0 / MOBA_BLOCK)
        out = jnp.where(lane == g * (KMEAN_PAGES // PAGES_PER_BLOCK) + blk, col, out)
    o_ref[0] = out


def _kmean_call(cache_kt, layer, page_table, n_full):
    n_seq = page_table.shape[0]
    _, _, n_h, hd, ps = cache_kt.shape
    page = lambda r: pl.BlockSpec((None, None, n_h, hd, ps),
                                  lambda b, g, pt: (layer, pt[b, g * KMEAN_PAGES + r], 0, 0, 0))
    blocks_per_step = KMEAN_PAGES // PAGES_PER_BLOCK
    return pl.pallas_call(
        _kmean_kernel,
        grid_spec=pltpu.PrefetchScalarGridSpec(
            num_scalar_prefetch=1,
            grid=(n_seq, n_full // blocks_per_step),
            in_specs=[page(r) for r in range(KMEAN_PAGES)],
            out_specs=pl.BlockSpec((1, n_h * hd, n_full), lambda b, g, pt: (b, 0, 0))),
        out_shape=jax.ShapeDtypeStruct((n_seq, n_h * hd, n_full), F32),
        compiler_params=_cparams("parallel", "arbitrary"),
        name="moba_kmean",
    )(page_table, *([cache_kt] * KMEAN_PAGES))


def _top3_kernel(q_ref, km_ref, o_ref):
    n_h = q_ref.shape[1]
    for h in range(n_h):
        g = _dot(q_ref[0, h], km_ref[0, h], precision=HI)
        lane = lax.broadcasted_iota(I32, g.shape, 1)
        out_lane = lax.broadcasted_iota(I32, (g.shape[0], LANES), 1)
        out = jnp.zeros((g.shape[0], LANES), I32)
        for s in range(MOBA_TOPK):
            mx = jnp.max(g, axis=1, keepdims=True)
            idx = jnp.min(jnp.where(g == mx, lane, g.shape[1]), axis=1, keepdims=True)
            out = jnp.where(out_lane == s, idx, out)
            g = jnp.where(lane == idx, -jnp.inf, g)
        o_ref[0, h] = out


def _top3_call(q, k_mean_t):
    n_seq, n_h, t, hd = q.shape
    n_full = k_mean_t.shape[3]
    return pl.pallas_call(
        _top3_kernel,
        grid=(n_seq,),
        in_specs=[pl.BlockSpec((1, n_h, t, hd), lambda b: (b, 0, 0, 0)),
                  pl.BlockSpec((1, n_h, hd, n_full), lambda b: (b, 0, 0, 0))],
        out_specs=pl.BlockSpec((1, n_h, t, LANES), lambda b: (b, 0, 0, 0)),
        out_shape=jax.ShapeDtypeStruct((n_seq, n_h, t, LANES), I32),
        compiler_params=_cparams("parallel"),
        name="moba_top3",
    )(q, k_mean_t)


def _moba_sample_kernel(pt_ref, sel_ref, rb_ref, qt_ref, knt_ref, vnt_ref, town_ref, tprev_ref, ck_hbm, cv_hbm,
                        o_ref, kbuf, vbuf, sems, *, layer, n_seq, n_h, n_t, n_full):
    n_pg = MOBA_TOPK * PAGES_PER_BLOCK
    n_copies = n_t * n_pg
    b = pl.program_id(0)
    h = pl.program_id(1)
    step = b * n_h + h
    slot = step % 2

    def page_copies(step_f, slot_f):
        b_f = step_f // n_h
        h_f = step_f % n_h
        copies = []
        for t in range(n_t):
            for s in range(MOBA_TOPK):
                blk_id = sel_ref[(step_f * n_t + t) * MOBA_TOPK + s]
                for r in range(PAGES_PER_BLOCK):
                    pg = pt_ref[b_f, blk_id * PAGES_PER_BLOCK + r]
                    j = (t * MOBA_TOPK + s) * PAGES_PER_BLOCK + r
                    copies.append(pltpu.make_async_copy(ck_hbm.at[layer, pg, h_f], kbuf.at[slot_f, j], sems.at[0, slot_f]))
                    copies.append(pltpu.make_async_copy(cv_hbm.at[layer, pg, h_f], vbuf.at[slot_f, j], sems.at[1, slot_f]))
        return copies

    @pl.when(step == 0)
    def _():
        for c in page_copies(step, slot):
            c.start()

    @pl.when(step + 1 < n_seq * n_h)
    def _():
        for c in page_copies(step + 1, 1 - slot):
            c.start()

    for j in range(n_copies):
        pltpu.make_async_copy(ck_hbm.at[layer, 0, 0], kbuf.at[slot, j], sems.at[0, slot]).wait()
        pltpu.make_async_copy(cv_hbm.at[layer, 0, 0], vbuf.at[slot, j], sems.at[1, slot]).wait()

    k_refs = [kbuf.at[slot, j] for j in range(n_copies)]
    v_refs = [vbuf.at[slot, j] for j in range(n_copies)]
    qt = qt_ref[0, 0] * ATT_SCALE
    knt = knt_ref[0, 0]
    vnt = vnt_ref[0, 0]
    b_far = rb_ref[N_BUCKETS - 1, h]
    out_lane = lax.broadcasted_iota(I32, (HEAD_DIM, n_t), 1)
    out = jnp.zeros((HEAD_DIM, n_t), F32)
    for t in range(n_t):
        q_col = qt[:, t:t + 1]
        base = ((b * n_h + h) * n_t + t) * MOBA_TOPK
        scores = [jnp.sum(knt * q_col, axis=0, keepdims=True) + town_ref[0, t:t + 1, :]]
        for s in range(MOBA_TOPK):
            last = sel_ref[base + s] == n_full - 1
            for r in range(PAGES_PER_BLOCK):
                sc = jnp.sum(k_refs[(t * MOBA_TOPK + s) * PAGES_PER_BLOCK + r][...] * q_col, axis=0, keepdims=True)
                bias = jnp.where(last, tprev_ref[0, t:t + 1, r * PAGE_SIZE:(r + 1) * PAGE_SIZE], b_far)
                scores.append(sc + bias)
        m = functools.reduce(jnp.maximum, [jnp.max(sc, axis=1, keepdims=True) for sc in scores])
        probs = [jnp.exp(sc - m) for sc in scores]
        denom = functools.reduce(lambda a, c: a + c, [jnp.sum(p, axis=1, keepdims=True) for p in probs])
        acc = functools.reduce(lambda a, c: a + c,
                               [probs[1 + j] * v_refs[t * n_pg + j][...] for j in range(n_pg)])
        col = (jnp.sum(acc, axis=1, keepdims=True) + jnp.sum(probs[0] * vnt, axis=1, keepdims=True)) / denom
        out = jnp.where(out_lane == t, col, out)
    o_ref[0, 0] = out


def _moba_sample_call(qt, knt, vnt, cache_kt, cache_vt, layer, page_table, sel_flat, rel_bias, ts_own, ts_prev, n_full):
    n_seq, n_h, hd, n_t = qt.shape
    ps = cache_kt.shape[4]
    tok = pl.BlockSpec((1, 1, hd, n_t), lambda b, h, pt, sel: (b, h, 0, 0))
    n_copies = n_t * MOBA_TOPK * PAGES_PER_BLOCK
    return pl.pallas_call(
        functools.partial(_moba_sample_kernel, layer=layer, n_seq=n_seq, n_h=n_h, n_t=n_t, n_full=n_full),
        grid_spec=pltpu.PrefetchScalarGridSpec(
            num_scalar_prefetch=2,
            grid=(n_seq, n_h),
            in_specs=[pl.BlockSpec(memory_space=pltpu.SMEM), tok, tok, tok,
                      pl.BlockSpec((1, n_t, n_t), lambda b, h, pt, sel: (h, 0, 0)),
                      pl.BlockSpec((1, n_t, MOBA_BLOCK), lambda b, h, pt, sel: (h, 0, 0)),
                      pl.BlockSpec(memory_space=pl.ANY), pl.BlockSpec(memory_space=pl.ANY)],
            out_specs=tok,
            scratch_shapes=[pltpu.VMEM((2, n_copies, hd, ps), F32), pltpu.VMEM((2, n_copies, hd, ps), F32),
                            pltpu.SemaphoreType.DMA((2, 2))]),
        out_shape=jax.ShapeDtypeStruct((n_seq, n_h, hd, n_t), F32),
        compiler_params=_cparams("arbitrary", "arbitrary"),
        name="moba_sample",
    )(page_table, sel_flat, rel_bias, qt, knt, vnt, ts_own, ts_prev, cache_kt, cache_vt)


def _moba_sample(p_att, cache_kt, cache_vt, layer, page_table, rel_bias, ts_own, ts_prev):
    n_seq, n_t, _ = p_att.shape
    n_full = page_table.shape[1] // PAGES_PER_BLOCK
    qkv = p_att.reshape(n_seq, n_t, 3, H_ATT, HEAD_DIM).transpose(2, 0, 3, 1, 4)
    q, k_new, v_new = qkv[0], qkv[1], qkv[2]
    k_mean_t = _kmean_call(cache_kt, layer, page_table, n_full).reshape(n_seq, H_ATT, HEAD_DIM, n_full)
    picks = _top3_call(q, k_mean_t)
    sel_flat = picks[..., :MOBA_TOPK].reshape(-1)
    tr = lambda z: jnp.swapaxes(z, 2, 3)
    out_t = _moba_sample_call(tr(q), tr(k_new), tr(v_new), cache_kt, cache_vt, layer, page_table, sel_flat, rel_bias,
                              ts_own, ts_prev, n_full)
    return out_t.transpose(0, 3, 1, 2).reshape(n_seq, n_t, D_ATT), k_new, v_new


PROMPT_TM = 512
PROMPT_KEY_GROUP = 4
PROMPT_FF_TM = 1024
PROMPT_MOE_TM = 1024
PROMPT_MOE_SUB = 320
PROMPT_MOE_FF_TILE = D_FF // 2
PROMPT_RW_TC = 256
PROMPT_RW_TL = 512
PROMPT_RW_CHUNK = 64
PROMPT_RW_UNROLL = 4
FF_TILE = 256


def _to_pages(z):
    b, s, _ = z.shape
    return z.reshape(b, s // PAGE_SIZE, PAGE_SIZE, H_ATT, HEAD_DIM).transpose(0, 1, 3, 2, 4)


def kernel(x_prompt, x_sample, cache_k, cache_v, state_wkv, state_shift, page_table, c_prompt, c_sample, rel_bias, w_ada, b_ada, w_in, rw_mu, rw_w0, rw_w2, rw_a0, rw_a2, rw_g2, rw_k_k, rw_k_a, rw_r_k, rw_lnx_g, rw_lnx_b, w_out, ln1_g, ln1_b, ln2_g, ln2_b, ffn_w_gate, ffn_w_up, ffn_w_down, moe_w_router, moe_w_gate, moe_w_up, moe_w_down):
    n_p, seq, d = x_prompt.shape
    n_s, t_s, _ = x_sample.shape
    depth = w_in.shape[0]
    assert page_table.shape[1] % PAGES_PER_BLOCK == 0 and seq % MOBA_BLOCK == 0
    rows_s = n_s * t_s

    n_c = n_p + n_s
    pad_c = -n_c % 8
    c_all = jnp.pad(jnp.concatenate([c_prompt, c_sample], axis=0), ((0, pad_c), (0, 0)))
    mods = _ada_call(c_all, w_ada, b_ada)

    t_own, t_prev, ts_own, ts_prev = _bias_call(
        rel_bias, [(MOBA_BLOCK, MOBA_BLOCK), (MOBA_BLOCK, MOBA_BLOCK), (t_s, t_s), (t_s, MOBA_BLOCK)],
        [0, MOBA_BLOCK, 0, MOBA_BLOCK], key_major=[True, True, False, False])

    cache_kt = jnp.swapaxes(cache_k, 3, 4)
    cache_vt = jnp.swapaxes(cache_v, 3, 4)

    xp = x_prompt
    xs = x_sample.reshape(1, rows_s, d)
    zero_shift = jnp.zeros((n_p, RW_IN), F32)
    zero_state = jnp.zeros((n_p, H_RWKV, HEAD_DIM, HEAD_DIM), F32)
    outs = {k: [] for k in ("kp", "vp", "ks", "vs", "wp", "ws", "hp", "hs")}
    for l in range(depth):
        mod_p = [m[:, None, :] for m in jnp.split(mods[l, :n_p], 6, axis=-1)]
        mod_s = [jnp.repeat(m, t_s, axis=0)[None] for m in jnp.split(mods[l, n_p:n_c], 6, axis=-1)]
        prm = dict(mu=rw_mu[l], w0=rw_w0[l], w2=rw_w2[l], a0=rw_a0[l], a2=rw_a2[l], g2=rw_g2[l], k_k=rw_k_k[l],
                   k_a=rw_k_a[l], r_k=rw_r_k[l], lnx_g=rw_lnx_g[l], lnx_b=rw_lnx_b[l])
        w_in_bf = w_in[l].astype(BF16)
        w_out_bf = w_out[l].astype(BF16)
        if l % 2 == 0:
            w_router = jnp.zeros((d, LANES), F32)
            wg, wu, wd = (w[l // 2][None].astype(BF16) for w in (ffn_w_gate, ffn_w_up, ffn_w_down))
        else:
            w_router = jnp.pad(moe_w_router[l // 2], ((0, 0), (0, LANES - N_EXPERTS)))
            wg, wu, wd = (w[l // 2].astype(BF16) for w in (moe_w_gate, moe_w_up, moe_w_down))

        pa, pr = _inproj_call(xp, mod_p[1], mod_p[0], w_in_bf, PROMPT_TM)
        y_att = _moba_prompt_call(pa, rel_bias, t_own, t_prev)
        y_rw, s_new, sh_new = _rwkv_mix(pr, zero_shift, zero_state, prm, PROMPT_RW_TC, PROMPT_RW_TL, PROMPT_RW_CHUNK,
                                        PROMPT_RW_UNROLL)
        xp = _outproj_call(y_att, y_rw, xp, mod_p[2], w_out_bf, ln1_g[l], ln1_b[l], PROMPT_TM)
        if l % 2 == 0:
            xp = _ffn_call(xp, mod_p[4], mod_p[3], mod_p[5], w_router, wg, wu, wd, ln2_g[l], ln2_b[l], PROMPT_FF_TM,
                           FF_TILE)
        else:
            xp = _moe_routed_call(xp, mod_p[4], mod_p[3], mod_p[5], w_router, wg, wu, wd, ln2_g[l], ln2_b[l],
                                  PROMPT_MOE_TM, PROMPT_MOE_FF_TILE, PROMPT_MOE_SUB)
        outs["kp"].append(_to_pages(pa[..., D_ATT:2 * D_ATT]))
        outs["vp"].append(_to_pages(pa[..., 2 * D_ATT:]))
        outs["wp"].append(s_new)
        outs["hp"].append(sh_new)

        pa, pr = _inproj_call(xs, mod_s[1], mod_s[0], w_in_bf, rows_s)
        y_att, k_new, v_new = _moba_sample(pa.reshape(n_s, t_s, ATT_IN), cache_kt, cache_vt, l, page_table, rel_bias,
                                           ts_own, ts_prev)
        y_rw, s_new, sh_new = _rwkv_mix(pr.reshape(n_s, t_s, RW_IN), state_shift[l], state_wkv[l], prm, t_s, t_s, t_s)
        xs = _outproj_call(y_att.reshape(1, rows_s, D_ATT), y_rw.reshape(1, rows_s, D_RWKV), xs, mod_s[2], w_out_bf,
                           ln1_g[l], ln1_b[l], rows_s)
        xs = _ffn_call(xs, mod_s[4], mod_s[3], mod_s[5], w_router, wg, wu, wd, ln2_g[l], ln2_b[l], rows_s, FF_TILE)
        outs["ks"].append(k_new)
        outs["vs"].append(v_new)
        outs["ws"].append(s_new)
        outs["hs"].append(sh_new)

    st = lambda k: jnp.stack(outs[k])
    return (xp, xs.reshape(n_s, t_s, d), st("kp"), st("vp"), st("ks"), st("vs"), st("wp"), st("ws"), st("hp"), st("hs"))
```

```python
import functools
import math

import jax
import jax.numpy as jnp
from jax import lax
from jax.experimental import pallas as pl
from jax.experimental.pallas import tpu as pltpu

F32 = jnp.float32
BF16 = jnp.bfloat16
I32 = jnp.int32
HI = lax.Precision.HIGHEST

D_MODEL = 1024
HEAD_DIM = 64
H_ATT = 8
H_RWKV = 8
D_ATT = H_ATT * HEAD_DIM
D_RWKV = H_RWKV * HEAD_DIM
ATT_IN = 3 * D_ATT
LORA_W = 64
LORA_A = 64
LORA_G = 128
RW_IN = 3 * D_RWKV + LORA_W + LORA_A + LORA_G
MOBA_BLOCK = 256
MOBA_TOPK = 3
PAGE_SIZE = 128
N_BUCKETS = 32
MAX_EXACT = N_BUCKETS // 2
MAX_DISTANCE = 128
D_FF = 2816
N_EXPERTS = 8
DEPTH = 2
ALPHA = (2 * DEPTH) ** 0.25
LN_EPS = 1e-5
LNX_EPS = 64e-5
NEG_INF = -1e30
ATT_SCALE = HEAD_DIM ** -0.5

LANES = 128
HEAD_PAIR = LANES // HEAD_DIM
BF16_ROWS = 16
VMEM_LIMIT = 56 * 1024 * 1024


def _cparams(*sem):
    return pltpu.CompilerParams(dimension_semantics=sem, vmem_limit_bytes=VMEM_LIMIT)


def _sigmoid(x):
    return 1.0 / (1.0 + jnp.exp(-x))


def _dot_nt(a, b, precision=None):
    return lax.dot_general(a, b, (((1,), (1,)), ((), ())), precision=precision, preferred_element_type=F32)


def _dot_tn(a, b, precision=None):
    return lax.dot_general(a, b, (((0,), (0,)), ((), ())), precision=precision, preferred_element_type=F32)


def _dot(a, b, precision=None):
    return jnp.dot(a, b, precision=precision, preferred_element_type=F32)


def _ada_kernel(c_ref, w_ref, b_ref, o_ref):
    c = c_ref[...]
    o_ref[0] = _dot(c * _sigmoid(c), w_ref[0], precision=HI) + b_ref[0]


def _ada_call(c_all, w_ada, b_ada):
    n = c_all.shape[0]
    depth, d, cols = w_ada.shape
    tn = 768
    return pl.pallas_call(
        _ada_kernel,
        grid=(depth, cols // tn),
        in_specs=[pl.BlockSpec((n, d), lambda l, j: (0, 0)),
                  pl.BlockSpec((1, d, tn), lambda l, j: (l, 0, j)),
                  pl.BlockSpec((1, 1, tn), lambda l, j: (l, 0, j))],
        out_specs=pl.BlockSpec((1, n, tn), lambda l, j: (l, 0, j)),
        out_shape=jax.ShapeDtypeStruct((depth, n, cols), F32),
        compiler_params=_cparams("parallel", "parallel"),
        name="ada_mod",
    )(c_all, w_ada, b_ada.reshape(depth, 1, cols))


def _bucket_of(dist):
    d = jnp.maximum(dist, 0)
    log_ratio = jnp.log(jnp.maximum(d, 1).astype(F32) / MAX_EXACT) / math.log(MAX_DISTANCE / MAX_EXACT)
    large = jnp.minimum(MAX_EXACT + (log_ratio * (N_BUCKETS - MAX_EXACT)).astype(I32), N_BUCKETS - 1)
    return jnp.where(d < MAX_EXACT, d, large)


def _bias_kernel(rb_ref, *o_refs, offsets, key_major):
    for o_ref, off, km in zip(o_refs, offsets, key_major):
        n_h, rows, cols = o_ref.shape
        dist = (lax.broadcasted_iota(I32, (rows, cols), 0) - lax.broadcasted_iota(I32, (rows, cols), 1))
        dist = (-dist if km else dist) + off
        bucket = _bucket_of(dist)
        for h in range(n_h):
            acc = jnp.zeros((rows, cols), F32)
            for b in range(N_BUCKETS):
                acc = jnp.where(bucket == b, rb_ref[b, h], acc)
            o_ref[h] = jnp.where(dist >= 0, acc, NEG_INF)


def _bias_call(rel_bias, shapes, offsets, key_major=None):
    n_h = rel_bias.shape[1]
    key_major = tuple(key_major) if key_major is not None else (False,) * len(shapes)
    return pl.pallas_call(
        functools.partial(_bias_kernel, offsets=tuple(offsets), key_major=key_major),
        in_specs=[pl.BlockSpec(memory_space=pltpu.SMEM)],
        out_specs=[pl.BlockSpec(memory_space=pltpu.VMEM) for _ in shapes],
        out_shape=[jax.ShapeDtypeStruct((n_h,) + s, F32) for s in shapes],
        name="rel_bias_tables",
    )(rel_bias)


def _inproj_kernel(x_ref, sc_ref, sh_ref, w_ref, oa_ref, or_ref, *, chunk):
    u = (x_ref[0] * (1.0 + sc_ref[0]) + sh_ref[0]).astype(BF16)
    att = oa_ref.shape[-1]
    for c0 in range(0, w_ref.shape[1], chunk):
        res = _dot(u, w_ref[:, c0:c0 + chunk])
        if c0 < att:
            oa_ref[0, :, c0:c0 + chunk] = res
        else:
            or_ref[0, :, c0 - att:c0 - att + chunk] = res


def _mod_spec(mod, tm):
    if mod.shape[1] == 1:
        return pl.BlockSpec((1, 1, mod.shape[2]), lambda n, i: (n, 0, 0))
    return pl.BlockSpec((1, tm, mod.shape[2]), lambda n, i: (n, i, 0))


def _inproj_call(x, sc, sh, w_bf, tm):
    n, l, d = x.shape
    cols = w_bf.shape[1]
    return pl.pallas_call(
        functools.partial(_inproj_kernel, chunk=256),
        grid=(n, l // tm),
        in_specs=[pl.BlockSpec((1, tm, d), lambda n, i: (n, i, 0)),
                  _mod_spec(sc, tm), _mod_spec(sh, tm),
                  pl.BlockSpec((d, cols), lambda n, i: (0, 0))],
        out_specs=[pl.BlockSpec((1, tm, ATT_IN), lambda n, i: (n, i, 0)),
                   pl.BlockSpec((1, tm, RW_IN), lambda n, i: (n, i, 0))],
        out_shape=[jax.ShapeDtypeStruct((n, l, ATT_IN), F32),
                   jax.ShapeDtypeStruct((n, l, RW_IN), F32)],
        compiler_params=_cparams("parallel", "parallel"),
        name="in_proj",
    )(x, sc, sh, w_bf)


def _moba_prompt_kernel(rb_ref, q_ref, k_ref, v_ref, town_ref, tprev_ref, o_ref, kb_ref, vb_ref, km_ref,
                        *, n_blk):
    hp = pl.program_id(1)
    i = pl.program_id(2)
    blk = MOBA_BLOCK

    @pl.when(i == 0)
    def _():
        k = k_ref[0]
        kb_ref[...] = k.astype(BF16)
        km_ref[...] = jnp.sum(k.reshape(n_blk, blk, LANES), axis=1) * (1.0 / blk)
        for c in range(n_blk):
            vt = v_ref[0, c * blk:(c + 1) * blk, :].T.astype(BF16)
            for h2 in range(HEAD_PAIR):
                vb_ref[h2, :HEAD_DIM, c * blk:(c + 1) * blk] = vt[h2 * HEAD_DIM:(h2 + 1) * HEAD_DIM]
                vb_ref[h2, HEAD_DIM:, c * blk:(c + 1) * blk] = jnp.ones((BF16_ROWS, blk), BF16)

    qt = q_ref[0].T
    dim_row = lax.broadcasted_iota(I32, qt.shape, 0)
    own0 = pl.multiple_of(i * blk, blk)
    heads = range(HEAD_PAIR)
    bidx = lax.broadcasted_iota(I32, (n_blk, blk), 0)
    past = bidx < i
    qt_h = [jnp.where((dim_row // HEAD_DIM) == h2, qt, 0.0) for h2 in heads]
    qb = [(z * ATT_SCALE).astype(BF16) for z in qt_h]
    sel = []
    for h2 in heads:
        g = jnp.where(past, _dot(km_ref[...], qt_h[h2], precision=HI), NEG_INF)
        rank = jnp.zeros(g.shape, I32)
        for jp in range(n_blk):
            gj = g[jp:jp + 1, :]
            rank = rank + ((gj > g) | ((gj == g) & (jp < bidx))).astype(I32)
        sel.append(jnp.where((rank < MOBA_TOPK) & past, 1.0, 0.0))
    b_far = [rb_ref[N_BUCKETS - 1, hp * HEAD_PAIR + h2] for h2 in heads]

    def attend(blocks, carry):
        s = []
        for r0, tables, rows in blocks:
            kb = kb_ref[pl.ds(r0, blk), :]
            s_b = [_dot(kb, qb[h2]) for h2 in heads]
            if tables is not None:
                s_b = [s_b[h2] + tables[h2] for h2 in heads]
            if rows is not None:
                s_b = [s_b[h2] + rows[h2] for h2 in heads]
            s.append(s_b)
        out = []
        for h2 in heads:
            m_new = functools.reduce(jnp.maximum, [jnp.max(s_b[h2], axis=0, keepdims=True) for s_b in s])
            if carry is not None:
                m, acc = carry[h2]
                m_new = jnp.maximum(m, m_new)
                acc = jnp.exp(m - m_new) * acc
            else:
                acc = 0.0
            for (r0, _, _), s_b in zip(blocks, s):
                p = jnp.exp(s_b[h2] - m_new).astype(BF16)
                acc = acc + _dot(vb_ref[h2, :, pl.ds(r0, blk)], p)
            out.append((m_new, acc))
        return tuple(out)

    def picked_row(j, h2, value):
        hit = jnp.sum(jnp.where(bidx == j, sel[h2], 0.0), axis=0, keepdims=True) > 0.5
        return jnp.where(hit, value, NEG_INF)

    def far_block(j):
        return pl.multiple_of(j * blk, blk), None, [picked_row(j, h2, b_far[h2]) for h2 in heads]

    def near_block(j):
        return pl.multiple_of(j * blk, blk), [tprev_ref[h2] for h2 in heads], [picked_row(j, h2, 0.0) for h2 in heads]

    carry = attend([(own0, [town_ref[h2] for h2 in heads], None)], None)
    n_far = jnp.maximum(i - 1, 0)
    grp = PROMPT_KEY_GROUP
    carry = lax.fori_loop(0, n_far // grp,
                          lambda j, c: attend([far_block(grp * j + u) for u in range(grp)], c), carry)
    carry = lax.fori_loop(grp * (n_far // grp), n_far, lambda j, c: attend([far_block(j)], c), carry)
    carry = lax.fori_loop(n_far, i, lambda j, c: attend([near_block(j)], c), carry)
    o_ref[0] = jnp.concatenate([acc[:HEAD_DIM] / acc[HEAD_DIM:HEAD_DIM + 1] for _, acc in carry], axis=0).T


def _moba_prompt_call(p_att, rel_bias, t_own, t_prev):
    n, l, _ = p_att.shape
    n_blk = l // MOBA_BLOCK
    n_hp = H_ATT // HEAD_PAIR
    tbl = pl.BlockSpec((HEAD_PAIR, MOBA_BLOCK, MOBA_BLOCK), lambda n, h, i: (h, 0, 0))
    return pl.pallas_call(
        functools.partial(_moba_prompt_kernel, n_blk=n_blk),
        grid=(n, n_hp, n_blk),
        in_specs=[pl.BlockSpec(memory_space=pltpu.SMEM),
                  pl.BlockSpec((1, MOBA_BLOCK, LANES), lambda n, h, i: (n, i, h)),
                  pl.BlockSpec((1, l, LANES), lambda n, h, i: (n, 0, n_hp + h)),
                  pl.BlockSpec((1, l, LANES), lambda n, h, i: (n, 0, 2 * n_hp + h)),
                  tbl, tbl],
        out_specs=pl.BlockSpec((1, MOBA_BLOCK, LANES), lambda n, h, i: (n, i, h)),
        out_shape=jax.ShapeDtypeStruct((n, l, D_ATT), F32),
        scratch_shapes=[pltpu.VMEM((l, LANES), BF16), pltpu.VMEM((HEAD_PAIR, HEAD_DIM + BF16_ROWS, l), BF16),
                        pltpu.VMEM((n_blk, LANES), F32)],
        compiler_params=_cparams("parallel", "parallel", "arbitrary"),
        name="moba_prompt",
    )(rel_bias, p_att, p_att, p_att, t_own, t_prev)


def _softplus(x):
    return jnp.maximum(x, 0.0) + jnp.log(1.0 + jnp.exp(-jnp.abs(x)))


def _rwkv_pre_kernel(p_ref, prev_ref, sh0_ref, mu_ref, w0_ref, w2_ref, a0_ref, a2_ref, g2_ref, kkw_ref, kaw_ref,
                     r_o, lw_o, k_o, v_o, kk_o, b_o, g_o):
    i = pl.program_id(1)
    p = p_ref[0]
    prev_row = jnp.where(i == 0, sh0_ref[0], prev_ref[0][7:8])
    row = lax.broadcasted_iota(I32, p.shape, 0)
    p_prev = jnp.where(row == 0, prev_row, pltpu.roll(p, 1, 0))
    xm = p + (p_prev - p) * mu_ref[...]
    o = 3 * D_RWKV
    r = xm[:, :D_RWKV]
    k = xm[:, D_RWKV:2 * D_RWKV]
    v = xm[:, 2 * D_RWKV:o]
    xw = xm[:, o:o + LORA_W]
    xa = xm[:, o + LORA_W:o + LORA_W + LORA_A]
    xg = xm[:, o + LORA_W + LORA_A:]
    w_log = -_softplus(-(w0_ref[...] + _dot(jnp.tanh(xw), w2_ref[...], precision=HI))) - 0.5
    a = _sigmoid(a0_ref[...] + _dot(xa, a2_ref[...], precision=HI))
    g = _dot(_sigmoid(xg), g2_ref[...], precision=HI)
    kk = k * kkw_ref[...]
    same_head = (lax.broadcasted_iota(I32, (D_RWKV, D_RWKV), 0) // HEAD_DIM
                 == lax.broadcasted_iota(I32, (D_RWKV, D_RWKV), 1) // HEAD_DIM)
    ss = _dot(kk * kk, jnp.where(same_head, 1.0, 0.0), precision=HI)
    kk = kk / jnp.maximum(jnp.sqrt(ss), 1e-12)
    r_o[0] = r
    lw_o[0] = -jnp.exp(w_log)
    k_o[0] = k * (1.0 + (a - 1.0) * kaw_ref[...])
    v_o[0] = v
    kk_o[0] = kk
    b_o[0] = kk * a
    g_o[0] = g


def _rwkv_pre_call(p_rw, shift0, mu, w0, w2, a0, a2, g2, k_k, k_a, tc):
    n, l, _ = p_rw.shape
    row = lambda z: z.reshape(1, -1)
    full = lambda z: pl.BlockSpec(z.shape, lambda n, i: (0,) * z.ndim)
    params = [row(mu), row(w0), w2, row(a0), a2, g2, row(k_k), row(k_a)]
    out_spec = pl.BlockSpec((1, tc, D_RWKV), lambda n, i: (n, i, 0))
    return pl.pallas_call(
        _rwkv_pre_kernel,
        grid=(n, l // tc),
        in_specs=[pl.BlockSpec((1, tc, RW_IN), lambda n, i: (n, i, 0)),
                  pl.BlockSpec((1, 8, RW_IN), lambda n, i: (n, jnp.maximum(i * (tc // 8) - 1, 0), 0)),
                  pl.BlockSpec((1, 1, RW_IN), lambda n, i: (n, 0, 0))] + [full(z) for z in params],
        out_specs=[out_spec] * 7,
        out_shape=[jax.ShapeDtypeStruct((n, l, D_RWKV), F32)] * 7,
        compiler_params=_cparams("parallel", "parallel"),
        name="rwkv_pre",
    )(p_rw, p_rw, shift0.reshape(n, 1, RW_IN), *params)


def _chunk_cumsum(x):
    rows = x.shape[0]
    row = lax.broadcasted_iota(I32, x.shape, 0)
    s = 1
    while s < rows:
        x = x + jnp.where(row >= s, pltpu.roll(x, s, 0), 0.0)
        s *= 2
    return x


def _rwkv_scan_kernel(r_ref, lw_ref, k_ref, v_ref, kk_ref, b_ref, g_ref, s0_ref, rk_ref, lg_ref, lb_ref,
                      y_ref, sf_ref, st_ref, yraw_ref, *, chunk, unroll, mxu):
    i = pl.program_id(2)
    C = chunk
    tl = r_ref.shape[1]
    n_dbl = max(C.bit_length() - 2, 0)

    @pl.when(i == 0)
    def _():
        st_ref[...] = s0_ref[0, 0]

    lane = lax.broadcasted_iota(I32, (1, LANES), 1)
    head0 = lane < HEAD_DIM
    ti = lax.broadcasted_iota(I32, (C, C), 0)
    si = lax.broadcasted_iota(I32, (C, C), 1)
    strict = si < ti
    incl = si <= ti
    eye_c = jnp.where(si == ti, 1.0, 0.0)
    rj = lax.broadcasted_iota(I32, (LANES, LANES), 0)
    cj = lax.broadcasted_iota(I32, (LANES, LANES), 1)
    same_head = (rj // HEAD_DIM) == (cj // HEAD_DIM)
    eye_l = rj == cj
    mm = lambda a, b: _dot(a.astype(mxu), b.astype(mxu))
    mm_nt = lambda a, b: _dot_nt(a.astype(mxu), b.astype(mxu))
    mm_tn = lambda a, b: _dot_tn(a.astype(mxu), b.astype(mxu))

    def chunk_operands(rows):
        lw = lw_ref[0, rows, :]
        r = r_ref[0, rows, :]
        k = k_ref[0, rows, :]
        v = v_ref[0, rows, :]
        kk = kk_ref[0, rows, :]
        b = b_ref[0, rows, :]
        cum = _chunk_cumsum(lw)
        c_end = cum[C - 1:C, :]
        e_neg = jnp.exp(-cum)
        e_end = jnp.exp(c_end - cum)
        al = -kk * jnp.exp(cum - lw)
        rt = r * jnp.exp(cum)
        return dict(v=v, al=al, rt=rt, bt=b * e_neg, kt=k * e_neg, bt2=b * e_end, kt2=k * e_end,
                    lhs=jnp.concatenate([al, rt], axis=0), c_end=c_end)

    def group_terms(rows_list):
        ops = [chunk_operands(rw) for rw in rows_list]
        ch = [(u, h) for u in range(len(ops)) for h in range(HEAD_PAIR)]
        mine = [head0, jnp.logical_not(head0)]
        lhs = [jnp.where(mine[h], ops[u]["lhs"], 0.0) for u, h in ch]
        ab = [mm_nt(lhs[n], ops[u]["bt"]) for n, (u, h) in enumerate(ch)]
        ak = [mm_nt(lhs[n], ops[u]["kt"]) for n, (u, h) in enumerate(ch)]
        m_ab = [jnp.where(strict, z[:C], 0.0) for z in ab]
        m_ak = [jnp.where(strict, z[:C], 0.0) for z in ak]
        b_br = [jnp.where(incl, z[C:], 0.0) for z in ab]
        b_kr = [jnp.where(incl, z[C:], 0.0) for z in ak]
        t_inv = [eye_c + z for z in m_ab]
        pk = m_ab
        for _ in range(n_dbl):
            pk = [mm(z, z) for z in pk]
            t_inv = [t + mm(z, t) for z, t in zip(pk, t_inv)]
        mv = [mm(m_ak[n], ops[u]["v"]) for n, (u, h) in enumerate(ch)]
        aw = [mm(t_inv[n], jnp.concatenate([ops[u]["al"], mv[n]], axis=1)) for n, (u, h) in enumerate(ch)]
        baw = [mm(b_br[n], aw[n]) for n in range(len(ch))]
        bkv = [mm(b_kr[n], ops[u]["v"]) for n, (u, h) in enumerate(ch)]
        terms = []
        for u, o in enumerate(ops):
            n0, n1 = u * HEAD_PAIR, u * HEAD_PAIR + 1
            pick = lambda f: jnp.where(head0, f(n0), f(n1))
            a2 = pick(lambda n: aw[n][:, :LANES])
            w = pick(lambda n: aw[n][:, LANES:])
            r2 = o["rt"] + pick(lambda n: baw[n][:, :LANES])
            y0 = pick(lambda n: baw[n][:, LANES:] + bkv[n])
            tc_t = jnp.where(same_head, mm_tn(o["bt2"], a2), 0.0)
            gc_t = jnp.where(same_head, mm_tn(jnp.concatenate([o["bt2"], o["kt2"]], axis=0),
                                              jnp.concatenate([w, o["v"]], axis=0)), 0.0)
            dec_col = jnp.sum(jnp.where(eye_l, jnp.exp(o["c_end"]), 0.0), axis=1, keepdims=True)
            terms.append((r2, y0, tc_t, gc_t, dec_col))
        return terms

    def group_body(c, carry):
        rows = [pl.ds(pl.multiple_of((c * unroll + u) * C, C), C) for u in range(unroll)]
        terms = group_terms(rows)
        st = st_ref[...]
        for rw, (r2, y0, tc_t, gc_t, dec_col) in zip(rows, terms):
            yraw_ref[rw, :] = mm(r2, st) + y0
            st = dec_col * st + mm(tc_t, st) + gc_t
        st_ref[...] = st
        return carry

    lax.fori_loop(0, tl // (C * unroll), group_body, 0)
    sf_ref[0, 0] = st_ref[...]

    def head_sum(z):
        s0 = jnp.sum(jnp.where(head0, z, 0.0), axis=1, keepdims=True)
        s1 = jnp.sum(jnp.where(head0, 0.0, z), axis=1, keepdims=True)
        return jnp.where(head0, s0, s1)

    y = yraw_ref[...]
    d = y - head_sum(y) * (1.0 / HEAD_DIM)
    var = head_sum(d * d) * (1.0 / HEAD_DIM)
    yn = d * lax.rsqrt(var + LNX_EPS) * lg_ref[...] + lb_ref[...]
    bonus = head_sum(r_ref[0] * k_ref[0] * rk_ref[...]) * v_ref[0]
    y_ref[0] = (yn + bonus) * g_ref[0]


def _rwkv_scan_call(ops, st0, r_k, lnx_g, lnx_b, tl, chunk, unroll):
    n, l, _ = ops[0].shape
    n_hp = H_RWKV // HEAD_PAIR
    mxu = BF16 if chunk >= 16 else F32
    tok = pl.BlockSpec((1, tl, LANES), lambda n, h, i: (n, i, h))
    par = pl.BlockSpec((1, LANES), lambda n, h, i: (0, h))
    st_spec = pl.BlockSpec((1, 1, LANES, LANES), lambda n, h, i: (n, h, 0, 0))
    return pl.pallas_call(
        functools.partial(_rwkv_scan_kernel, chunk=chunk, unroll=unroll, mxu=mxu),
        grid=(n, n_hp, l // tl),
        in_specs=[tok] * 7 + [st_spec, par, par, par],
        out_specs=[tok, st_spec],
        out_shape=[jax.ShapeDtypeStruct((n, l, D_RWKV), F32),
                   jax.ShapeDtypeStruct((n, n_hp, LANES, LANES), F32)],
        scratch_shapes=[pltpu.VMEM((LANES, LANES), F32), pltpu.VMEM((tl, LANES), F32)],
        compiler_params=_cparams("parallel", "parallel", "arbitrary"),
        name="rwkv_scan",
    )(*ops, st0, r_k.reshape(1, D_RWKV), lnx_g.reshape(1, D_RWKV), lnx_b.reshape(1, D_RWKV))


def _state_to_pairs(s):
    n, h = s.shape[:2]
    st = jnp.swapaxes(s, -1, -2).reshape(n, h // HEAD_PAIR, HEAD_PAIR, HEAD_DIM, HEAD_DIM)
    z = jnp.zeros_like(st[:, :, 0])
    top = jnp.concatenate([st[:, :, 0], z], axis=-1)
    bot = jnp.concatenate([z, st[:, :, 1]], axis=-1)
    return jnp.concatenate([top, bot], axis=-2)


def _pairs_to_state(sp):
    n, n_hp = sp.shape[:2]
    d0 = sp[:, :, :HEAD_DIM, :HEAD_DIM]
    d1 = sp[:, :, HEAD_DIM:, HEAD_DIM:]
    st = jnp.stack([d0, d1], axis=2).reshape(n, n_hp * HEAD_PAIR, HEAD_DIM, HEAD_DIM)
    return jnp.swapaxes(st, -1, -2)


def _rwkv_mix(p_rw, shift0, s0, prm, tc, tl, chunk, unroll=1):
    ops = _rwkv_pre_call(p_rw, shift0, prm["mu"], prm["w0"], prm["w2"], prm["a0"], prm["a2"], prm["g2"],
                         prm["k_k"], prm["k_a"], tc)
    y, sp = _rwkv_scan_call(ops, _state_to_pairs(s0), prm["r_k"], prm["lnx_g"], prm["lnx_b"], tl, chunk, unroll)
    return y, _pairs_to_state(sp), p_rw[:, -1]


def _layer_norm(z, g, b):
    mu = jnp.mean(z, axis=-1, keepdims=True)
    d = z - mu
    var = jnp.mean(d * d, axis=-1, keepdims=True)
    return d * lax.rsqrt(var + LN_EPS) * g + b


def _outproj_kernel(ya_ref, yr_ref, x_ref, gt_ref, w_ref, g_ref, b_ref, o_ref):
    o = (_dot(ya_ref[0].astype(BF16), w_ref[:D_ATT, :]) + _dot(yr_ref[0].astype(BF16), w_ref[D_ATT:, :]))
    o_ref[0] = _layer_norm(ALPHA * x_ref[0] + (1.0 + gt_ref[0]) * o, g_ref[...], b_ref[...])


def _outproj_call(y_att, y_rw, x, gt, w_bf, ln_g, ln_b, tm):
    n, l, d = x.shape
    row = pl.BlockSpec((1, d), lambda n, i: (0, 0))
    return pl.pallas_call(
        _outproj_kernel,
        grid=(n, l // tm),
        in_specs=[pl.BlockSpec((1, tm, D_ATT), lambda n, i: (n, i, 0)),
                  pl.BlockSpec((1, tm, D_RWKV), lambda n, i: (n, i, 0)),
                  pl.BlockSpec((1, tm, d), lambda n, i: (n, i, 0)),
                  _mod_spec(gt, tm),
                  pl.BlockSpec(w_bf.shape, lambda n, i: (0, 0)), row, row],
        out_specs=pl.BlockSpec((1, tm, d), lambda n, i: (n, i, 0)),
        out_shape=jax.ShapeDtypeStruct((n, l, d), F32),
        compiler_params=_cparams("parallel", "parallel"),
        name="out_proj_ln",
    )(y_att, y_rw, x, gt, w_bf, ln_g.reshape(1, d), ln_b.reshape(1, d))


def _ffn_kernel(x_ref, sc_ref, sh_ref, gt_ref, wr_ref, wg_ref, wu_ref, wd_ref, g_ref, b_ref, o_ref,
                u_ref, acc_ref, gate_ref, *, n_experts):
    e = pl.program_id(2)
    f = pl.program_id(3)
    last = (e == pl.num_programs(2) - 1) & (f == pl.num_programs(3) - 1)

    @pl.when((e == 0) & (f == 0))
    def _():
        u = x_ref[0] * (1.0 + sc_ref[0]) + sh_ref[0]
        u_ref[...] = u.astype(BF16)
        acc_ref[...] = jnp.zeros_like(acc_ref)
        if n_experts > 1:
            logits = _dot(u, wr_ref[...], precision=HI)
            lane = lax.broadcasted_iota(I32, logits.shape, 1)
            logits = jnp.where(lane < n_experts, logits, NEG_INF)
            v1 = jnp.max(logits, axis=1, keepdims=True)
            i1 = jnp.min(jnp.where(logits == v1, lane, LANES), axis=1, keepdims=True)
            rest = jnp.where(lane == i1, NEG_INF, logits)
            v2 = jnp.max(rest, axis=1, keepdims=True)
            i2 = jnp.min(jnp.where(rest == v2, lane, LANES), axis=1, keepdims=True)
            w2 = 1.0 / (1.0 + jnp.exp(v1 - v2))
            gate_ref[...] = jnp.where(lane == i1, 1.0 - w2, 0.0) + jnp.where(lane == i2, w2, 0.0)

    u = u_ref[...]
    hg = _dot(u, wg_ref[0])
    hu = _dot(u, wu_ref[0])
    h = hg * _sigmoid(hg) * hu
    if n_experts > 1:
        lane = lax.broadcasted_iota(I32, gate_ref.shape, 1)
        h = h * jnp.sum(jnp.where(lane == e, gate_ref[...], 0.0), axis=1, keepdims=True)
    acc_ref[...] += _dot(h.astype(BF16), wd_ref[0])

    @pl.when(last)
    def _():
        o_ref[0] = _layer_norm(ALPHA * x_ref[0] + (1.0 + gt_ref[0]) * acc_ref[...], g_ref[...], b_ref[...])


def _ffn_call(x, sc, sh, gt, w_router, wg_bf, wu_bf, wd_bf, ln_g, ln_b, tm, tf):
    n, l, d = x.shape
    n_e, _, d_ff = wg_bf.shape
    row = pl.BlockSpec((1, d), lambda n, i, e, f: (0, 0))
    mod = lambda m: (pl.BlockSpec((1, 1, d), lambda n, i, e, f: (n, 0, 0)) if m.shape[1] == 1
                     else pl.BlockSpec((1, tm, d), lambda n, i, e, f: (n, i, 0)))
    return pl.pallas_call(
        functools.partial(_ffn_kernel, n_experts=n_e),
        grid=(n, l // tm, n_e, d_ff // tf),
        in_specs=[pl.BlockSpec((1, tm, d), lambda n, i, e, f: (n, i, 0)),
                  mod(sc), mod(sh), mod(gt),
                  pl.BlockSpec(w_router.shape, lambda n, i, e, f: (0, 0)),
                  pl.BlockSpec((1, d, tf), lambda n, i, e, f: (e, 0, f)),
                  pl.BlockSpec((1, d, tf), lambda n, i, e, f: (e, 0, f)),
                  pl.BlockSpec((1, tf, d), lambda n, i, e, f: (e, f, 0)),
                  row, row],
        out_specs=pl.BlockSpec((1, tm, d), lambda n, i, e, f: (n, i, 0)),
        out_shape=jax.ShapeDtypeStruct((n, l, d), F32),
        scratch_shapes=[pltpu.VMEM((tm, d), BF16), pltpu.VMEM((tm, d), F32), pltpu.VMEM((tm, LANES), F32)],
        compiler_params=_cparams("parallel", "parallel", "arbitrary", "arbitrary"),
        name="ffn_ln" if n_e == 1 else "moe_ln",
    )(x, sc, sh, gt, w_router, wg_bf, wu_bf, wd_bf, ln_g.reshape(1, d), ln_b.reshape(1, d))


ROUTE_BLOCK = 256


def _router_kernel(x_ref, sc_ref, sh_ref, wr_ref, gate_o, pos_o, post_o, cnt_o, *, n_experts):
    u = x_ref[0] * (1.0 + sc_ref[0]) + sh_ref[0]
    tm = u.shape[0]
    logits = _dot(u, wr_ref[...], precision=HI)
    lane = lax.broadcasted_iota(I32, logits.shape, 1)
    logits = jnp.where(lane < n_experts, logits, NEG_INF)
    v1 = jnp.max(logits, axis=1, keepdims=True)
    i1 = jnp.min(jnp.where(logits == v1, lane, LANES), axis=1, keepdims=True)
    rest = jnp.where(lane == i1, NEG_INF, logits)
    v2 = jnp.max(rest, axis=1, keepdims=True)
    i2 = jnp.min(jnp.where(rest == v2, lane, LANES), axis=1, keepdims=True)
    w2 = 1.0 / (1.0 + jnp.exp(v1 - v2))
    gate_o[0] = jnp.where(lane == i1, 1.0 - w2, 0.0) + jnp.where(lane == i2, w2, 0.0)
    chosen = (lane == i1) | (lane == i2)
    sel = jnp.where(chosen, 1.0, 0.0)
    rb = ROUTE_BLOCK
    tri = jnp.where(lax.broadcasted_iota(I32, (rb, rb), 1) < lax.broadcasted_iota(I32, (rb, rb), 0), 1.0, 0.0).astype(BF16)
    carry = jnp.zeros((1, LANES), F32)
    for c in range(tm // rb):
        blk = sel[c * rb:(c + 1) * rb]
        pos = _dot(tri, blk.astype(BF16)) + carry
        pos = jnp.where(chosen[c * rb:(c + 1) * rb], pos, -1.0)
        pos_o[0, c * rb:(c + 1) * rb, :] = pos
        post_o[0, :, c * rb:(c + 1) * rb] = pos.T
        carry = carry + jnp.sum(blk, axis=0, keepdims=True)
    cnt_o[0, 0] = jnp.broadcast_to(carry, (8, LANES)).astype(I32)


def _router_call(x, sc, sh, w_router, n_experts, tm):
    n, l, d = x.shape
    mod = pl.BlockSpec((1, 1, d), lambda n, i: (n, 0, 0))
    tok = pl.BlockSpec((1, tm, LANES), lambda n, i: (n, i, 0))
    return pl.pallas_call(
        functools.partial(_router_kernel, n_experts=n_experts),
        grid=(n, l // tm),
        in_specs=[pl.BlockSpec((1, tm, d), lambda n, i: (n, i, 0)), mod, mod,
                  pl.BlockSpec(w_router.shape, lambda n, i: (0, 0))],
        out_specs=[tok, tok, pl.BlockSpec((1, LANES, tm), lambda n, i: (n, 0, i)),
                   pl.BlockSpec((1, 1, 8, LANES), lambda n, i: (n, i, 0, 0))],
        out_shape=[jax.ShapeDtypeStruct((n, l, LANES), F32), jax.ShapeDtypeStruct((n, l, LANES), F32),
                   jax.ShapeDtypeStruct((n, LANES, l), F32), jax.ShapeDtypeStruct((n, l // tm, 8, LANES), I32)],
        compiler_params=_cparams("parallel", "parallel"),
        name="moe_router",
    )(x, sc, sh, w_router)


def _moe_routed_kernel(cnt_ref, x_ref, sc_ref, sh_ref, gt_ref, gate_ref, pos_ref, post_ref, wg_ref, wu_ref, wd_ref,
                       g_ref, b_ref, o_ref, u_ref, xs_ref, oacc_ref, *, n_experts, sub):
    n = pl.program_id(0)
    i = pl.program_id(1)
    e = pl.program_id(2)
    f = pl.program_id(3)
    n_f = pl.num_programs(3)
    tm = x_ref.shape[1]
    count = cnt_ref[(n * pl.num_programs(1) + i) * n_experts + e]
    n_sub = (count + sub - 1) // sub

    @pl.when((e == 0) & (f == 0))
    def _():
        u_ref[...] = (x_ref[0] * (1.0 + sc_ref[0]) + sh_ref[0]).astype(BF16)
        o_ref[...] = jnp.zeros_like(o_ref)

    @pl.when(f == 0)
    def _():
        erow = lax.broadcasted_iota(I32, (8, tm), 0)
        pos_row = jnp.sum(jnp.where(erow == e, post_ref[0, 0:8, :], 0.0), axis=0, keepdims=True)

        def gather(s, c):
            rows = pl.ds(pl.multiple_of(s * sub, BF16_ROWS), sub)
            want = (lax.broadcasted_iota(I32, (sub, tm), 0) + s * sub).astype(F32)
            onehot = jnp.where(pos_row == want, 1.0, 0.0).astype(BF16)
            xs_ref[rows, :] = _dot(onehot, u_ref[...]).astype(BF16)
            oacc_ref[rows, :] = jnp.zeros((sub, oacc_ref.shape[1]), F32)
            return c

        lax.fori_loop(0, n_sub, gather, 0)

    def expert(s, c):
        rows = pl.ds(pl.multiple_of(s * sub, BF16_ROWS), sub)
        xs = xs_ref[rows, :]
        hg = _dot(xs, wg_ref[0])
        h = hg * _sigmoid(hg) * _dot(xs, wu_ref[0])
        oacc_ref[rows, :] += _dot(h.astype(BF16), wd_ref[0])
        return c

    lax.fori_loop(0, n_sub, expert, 0)

    @pl.when(f == n_f - 1)
    def _():
        lane = lax.broadcasted_iota(I32, (tm, LANES), 1)
        pos_col = jnp.sum(jnp.where(lane == e, pos_ref[0], 0.0), axis=1, keepdims=True)
        gate_col = jnp.sum(jnp.where(lane == e, gate_ref[0], 0.0), axis=1, keepdims=True)

        def scatter(s, c):
            rows = pl.ds(pl.multiple_of(s * sub, BF16_ROWS), sub)
            want = (lax.broadcasted_iota(I32, (tm, sub), 1) + s * sub).astype(F32)
            w = jnp.where(pos_col == want, gate_col, 0.0).astype(BF16)
            o_ref[0] += _dot(w, oacc_ref[rows, :].astype(BF16))
            return c

        lax.fori_loop(0, n_sub, scatter, 0)

    @pl.when((e == n_experts - 1) & (f == n_f - 1))
    def _():
        o_ref[0] = _layer_norm(ALPHA * x_ref[0] + (1.0 + gt_ref[0]) * o_ref[0], g_ref[...], b_ref[...])


def _moe_routed_call(x, sc, sh, gt, w_router, wg_bf, wu_bf, wd_bf, ln_g, ln_b, tm, tf, sub):
    n, l, d = x.shape
    n_e, _, d_ff = wg_bf.shape
    gate, pos, post, cnt = _router_call(x, sc, sh, w_router, n_e, tm)
    cnt_flat = cnt[:, :, 0, :n_e].reshape(-1)
    n_sub_max = -(-tm // sub)
    row = pl.BlockSpec((1, d), lambda n, i, e, f, c: (0, 0))
    mod = pl.BlockSpec((1, 1, d), lambda n, i, e, f, c: (n, 0, 0))
    tok = pl.BlockSpec((1, tm, LANES), lambda n, i, e, f, c: (n, i, 0))
    return pl.pallas_call(
        functools.partial(_moe_routed_kernel, n_experts=n_e, sub=sub),
        grid_spec=pltpu.PrefetchScalarGridSpec(
            num_scalar_prefetch=1,
            grid=(n, l // tm, n_e, d_ff // tf),
            in_specs=[pl.BlockSpec((1, tm, d), lambda n, i, e, f, c: (n, i, 0)), mod, mod, mod, tok, tok,
                      pl.BlockSpec((1, LANES, tm), lambda n, i, e, f, c: (n, 0, i)),
                      pl.BlockSpec((1, d, tf), lambda n, i, e, f, c: (e, 0, f)),
                      pl.BlockSpec((1, d, tf), lambda n, i, e, f, c: (e, 0, f)),
                      pl.BlockSpec((1, tf, d), lambda n, i, e, f, c: (e, f, 0)),
                      row, row],
            out_specs=pl.BlockSpec((1, tm, d), lambda n, i, e, f, c: (n, i, 0)),
            scratch_shapes=[pltpu.VMEM((tm, d), BF16), pltpu.VMEM((n_sub_max * sub, d), BF16),
                            pltpu.VMEM((n_sub_max * sub, d), F32)]),
        out_shape=jax.ShapeDtypeStruct((n, l, d), F32),
        compiler_params=_cparams("parallel", "parallel", "arbitrary", "arbitrary"),
        name="moe_routed_ln",
    )(cnt_flat, x, sc, sh, gt, gate, pos, post, wg_bf, wu_bf, wd_bf, ln_g.reshape(1, d), ln_b.reshape(1, d))


PAGES_PER_BLOCK = MOBA_BLOCK // PAGE_SIZE
KMEAN_PAGES = 32


def _kmean_kernel(pt_ref, *refs):
    o_ref = refs[-1]
    g = pl.program_id(1)

    @pl.when(g == 0)
    def _():
        o_ref[...] = jnp.zeros_like(o_ref)

    lane = lax.broadcasted_iota(I32, o_ref.shape[1:], 1)
    out = o_ref[0]
    for blk in range(KMEAN_PAGES // PAGES_PER_BLOCK):
        tot = functools.reduce(lambda a, c: a + c,
                               [refs[blk * PAGES_PER_BLOCK + r][...] for r in range(PAGES_PER_BLOCK)])
        col = jnp.sum(tot, axis=-1, keepdims=True).reshape(-1, 1) * (1.0 / MOBA_BLOCK)
        out = jnp.where(lane == g * (KMEAN_PAGES // PAGES_PER_BLOCK) + blk, col, out)
    o_ref[0] = out


def _kmean_call(cache_kt, layer, page_table, n_full):
    n_seq = page_table.shape[0]
    _, _, n_h, hd, ps = cache_kt.shape
    page = lambda r: pl.BlockSpec((None, None, n_h, hd, ps),
                                  lambda b, g, pt: (layer, pt[b, g * KMEAN_PAGES + r], 0, 0, 0))
    blocks_per_step = KMEAN_PAGES // PAGES_PER_BLOCK
    return pl.pallas_call(
        _kmean_kernel,
        grid_spec=pltpu.PrefetchScalarGridSpec(
            num_scalar_prefetch=1,
            grid=(n_seq, n_full // blocks_per_step),
            in_specs=[page(r) for r in range(KMEAN_PAGES)],
            out_specs=pl.BlockSpec((1, n_h * hd, n_full), lambda b, g, pt: (b, 0, 0))),
        out_shape=jax.ShapeDtypeStruct((n_seq, n_h * hd, n_full), F32),
        compiler_params=_cparams("parallel", "arbitrary"),
        name="moba_kmean",
    )(page_table, *([cache_kt] * KMEAN_PAGES))


def _top3_kernel(q_ref, km_ref, o_ref):
    n_h = q_ref.shape[1]
    for h in range(n_h):
        g = _dot(q_ref[0, h], km_ref[0, h], precision=HI)
        lane = lax.broadcasted_iota(I32, g.shape, 1)
        out_lane = lax.broadcasted_iota(I32, (g.shape[0], LANES), 1)
        out = jnp.zeros((g.shape[0], LANES), I32)
        for s in range(MOBA_TOPK):
            mx = jnp.max(g, axis=1, keepdims=True)
            idx = jnp.min(jnp.where(g == mx, lane, g.shape[1]), axis=1, keepdims=True)
            out = jnp.where(out_lane == s, idx, out)
            g = jnp.where(lane == idx, -jnp.inf, g)
        o_ref[0, h] = out


def _top3_call(q, k_mean_t):
    n_seq, n_h, t, hd = q.shape
    n_full = k_mean_t.shape[3]
    return pl.pallas_call(
        _top3_kernel,
        grid=(n_seq,),
        in_specs=[pl.BlockSpec((1, n_h, t, hd), lambda b: (b, 0, 0, 0)),
                  pl.BlockSpec((1, n_h, hd, n_full), lambda b: (b, 0, 0, 0))],
        out_specs=pl.BlockSpec((1, n_h, t, LANES), lambda b: (b, 0, 0, 0)),
        out_shape=jax.ShapeDtypeStruct((n_seq, n_h, t, LANES), I32),
        compiler_params=_cparams("parallel"),
        name="moba_top3",
    )(q, k_mean_t)


def _moba_sample_kernel(pt_ref, sel_ref, rb_ref, qt_ref, knt_ref, vnt_ref, town_ref, tprev_ref, ck_hbm, cv_hbm,
                        o_ref, kbuf, vbuf, sems, *, layer, n_seq, n_h, n_t, n_full):
    n_pg = MOBA_TOPK * PAGES_PER_BLOCK
    n_copies = n_t * n_pg
    b = pl.program_id(0)
    h = pl.program_id(1)
    step = b * n_h + h
    slot = step % 2

    def page_copies(step_f, slot_f):
        b_f = step_f // n_h
        h_f = step_f % n_h
        copies = []
        for t in range(n_t):
            for s in range(MOBA_TOPK):
                blk_id = sel_ref[(step_f * n_t + t) * MOBA_TOPK + s]
                for r in range(PAGES_PER_BLOCK):
                    pg = pt_ref[b_f, blk_id * PAGES_PER_BLOCK + r]
                    j = (t * MOBA_TOPK + s) * PAGES_PER_BLOCK + r
                    copies.append(pltpu.make_async_copy(ck_hbm.at[layer, pg, h_f], kbuf.at[slot_f, j], sems.at[0, slot_f]))
                    copies.append(pltpu.make_async_copy(cv_hbm.at[layer, pg, h_f], vbuf.at[slot_f, j], sems.at[1, slot_f]))
        return copies

    @pl.when(step == 0)
    def _():
        for c in page_copies(step, slot):
            c.start()

    @pl.when(step + 1 < n_seq * n_h)
    def _():
        for c in page_copies(step + 1, 1 - slot):
            c.start()

    for j in range(n_copies):
        pltpu.make_async_copy(ck_hbm.at[layer, 0, 0], kbuf.at[slot, j], sems.at[0, slot]).wait()
        pltpu.make_async_copy(cv_hbm.at[layer, 0, 0], vbuf.at[slot, j], sems.at[1, slot]).wait()

    k_refs = [kbuf.at[slot, j] for j in range(n_copies)]
    v_refs = [vbuf.at[slot, j] for j in range(n_copies)]
    qt = qt_ref[0, 0] * ATT_SCALE
    knt = knt_ref[0, 0]
    vnt = vnt_ref[0, 0]
    b_far = rb_ref[N_BUCKETS - 1, h]
    out_lane = lax.broadcasted_iota(I32, (HEAD_DIM, n_t), 1)
    out = jnp.zeros((HEAD_DIM, n_t), F32)
    for t in range(n_t):
        q_col = qt[:, t:t + 1]
        base = ((b * n_h + h) * n_t + t) * MOBA_TOPK
        scores = [jnp.sum(knt * q_col, axis=0, keepdims=True) + town_ref[0, t:t + 1, :]]
        for s in range(MOBA_TOPK):
            last = sel_ref[base + s] == n_full - 1
            for r in range(PAGES_PER_BLOCK):
                sc = jnp.sum(k_refs[(t * MOBA_TOPK + s) * PAGES_PER_BLOCK + r][...] * q_col, axis=0, keepdims=True)
                bias = jnp.where(last, tprev_ref[0, t:t + 1, r * PAGE_SIZE:(r + 1) * PAGE_SIZE], b_far)
                scores.append(sc + bias)
        m = functools.reduce(jnp.maximum, [jnp.max(sc, axis=1, keepdims=True) for sc in scores])
        probs = [jnp.exp(sc - m) for sc in scores]
        denom = functools.reduce(lambda a, c: a + c, [jnp.sum(p, axis=1, keepdims=True) for p in probs])
        acc = functools.reduce(lambda a, c: a + c,
                               [probs[1 + j] * v_refs[t * n_pg + j][...] for j in range(n_pg)])
        col = (jnp.sum(acc, axis=1, keepdims=True) + jnp.sum(probs[0] * vnt, axis=1, keepdims=True)) / denom
        out = jnp.where(out_lane == t, col, out)
    o_ref[0, 0] = out


def _moba_sample_call(qt, knt, vnt, cache_kt, cache_vt, layer, page_table, sel_flat, rel_bias, ts_own, ts_prev, n_full):
    n_seq, n_h, hd, n_t = qt.shape
    ps = cache_kt.shape[4]
    tok = pl.BlockSpec((1, 1, hd, n_t), lambda b, h, pt, sel: (b, h, 0, 0))
    n_copies = n_t * MOBA_TOPK * PAGES_PER_BLOCK
    return pl.pallas_call(
        functools.partial(_moba_sample_kernel, layer=layer, n_seq=n_seq, n_h=n_h, n_t=n_t, n_full=n_full),
        grid_spec=pltpu.PrefetchScalarGridSpec(
            num_scalar_prefetch=2,
            grid=(n_seq, n_h),
            in_specs=[pl.BlockSpec(memory_space=pltpu.SMEM), tok, tok, tok,
                      pl.BlockSpec((1, n_t, n_t), lambda b, h, pt, sel: (h, 0, 0)),
                      pl.BlockSpec((1, n_t, MOBA_BLOCK), lambda b, h, pt, sel: (h, 0, 0)),
                      pl.BlockSpec(memory_space=pl.ANY), pl.BlockSpec(memory_space=pl.ANY)],
            out_specs=tok,
            scratch_shapes=[pltpu.VMEM((2, n_copies, hd, ps), F32), pltpu.VMEM((2, n_copies, hd, ps), F32),
                            pltpu.SemaphoreType.DMA((2, 2))]),
        out_shape=jax.ShapeDtypeStruct((n_seq, n_h, hd, n_t), F32),
        compiler_params=_cparams("arbitrary", "arbitrary"),
        name="moba_sample",
    )(page_table, sel_flat, rel_bias, qt, knt, vnt, ts_own, ts_prev, cache_kt, cache_vt)


def _moba_sample(p_att, cache_kt, cache_vt, layer, page_table, rel_bias, ts_own, ts_prev):
    n_seq, n_t, _ = p_att.shape
    n_full = page_table.shape[1] // PAGES_PER_BLOCK
    qkv = p_att.reshape(n_seq, n_t, 3, H_ATT, HEAD_DIM).transpose(2, 0, 3, 1, 4)
    q, k_new, v_new = qkv[0], qkv[1], qkv[2]
    k_mean_t = _kmean_call(cache_kt, layer, page_table, n_full).reshape(n_seq, H_ATT, HEAD_DIM, n_full)
    picks = _top3_call(q, k_mean_t)
    sel_flat = picks[..., :MOBA_TOPK].reshape(-1)
    tr = lambda z: jnp.swapaxes(z, 2, 3)
    out_t = _moba_sample_call(tr(q), tr(k_new), tr(v_new), cache_kt, cache_vt, layer, page_table, sel_flat, rel_bias,
                              ts_own, ts_prev, n_full)
    return out_t.transpose(0, 3, 1, 2).reshape(n_seq, n_t, D_ATT), k_new, v_new


PROMPT_TM = 512
PROMPT_KEY_GROUP = 4
PROMPT_FF_TM = 1024
PROMPT_MOE_TM = 1024
PROMPT_MOE_SUB = 288
PROMPT_MOE_FF_TILE = D_FF // 2
PROMPT_RW_TC = 256
PROMPT_RW_TL = 512
PROMPT_RW_CHUNK = 64
PROMPT_RW_UNROLL = 8
FF_TILE = 256


def _to_pages(z):
    b, s, _ = z.shape
    return z.reshape(b, s // PAGE_SIZE, PAGE_SIZE, H_ATT, HEAD_DIM).transpose(0, 1, 3, 2, 4)


def kernel(x_prompt, x_sample, cache_k, cache_v, state_wkv, state_shift, page_table, c_prompt, c_sample, rel_bias, w_ada, b_ada, w_in, rw_mu, rw_w0, rw_w2, rw_a0, rw_a2, rw_g2, rw_k_k, rw_k_a, rw_r_k, rw_lnx_g, rw_lnx_b, w_out, ln1_g, ln1_b, ln2_g, ln2_b, ffn_w_gate, ffn_w_up, ffn_w_down, moe_w_router, moe_w_gate, moe_w_up, moe_w_down):
    n_p, seq, d = x_prompt.shape
    n_s, t_s, _ = x_sample.shape
    depth = w_in.shape[0]
    assert page_table.shape[1] % PAGES_PER_BLOCK == 0 and seq % MOBA_BLOCK == 0
    rows_s = n_s * t_s

    n_c = n_p + n_s
    pad_c = -n_c % 8
    c_all = jnp.pad(jnp.concatenate([c_prompt, c_sample], axis=0), ((0, pad_c), (0, 0)))
    mods = _ada_call(c_all, w_ada, b_ada)

    t_own, t_prev, ts_own, ts_prev = _bias_call(
        rel_bias, [(MOBA_BLOCK, MOBA_BLOCK), (MOBA_BLOCK, MOBA_BLOCK), (t_s, t_s), (t_s, MOBA_BLOCK)],
        [0, MOBA_BLOCK, 0, MOBA_BLOCK], key_major=[True, True, False, False])

    cache_kt = jnp.swapaxes(cache_k, 3, 4)
    cache_vt = jnp.swapaxes(cache_v, 3, 4)

    xp = x_prompt
    xs = x_sample.reshape(1, rows_s, d)
    zero_shift = jnp.zeros((n_p, RW_IN), F32)
    zero_state = jnp.zeros((n_p, H_RWKV, HEAD_DIM, HEAD_DIM), F32)
    outs = {k: [] for k in ("kp", "vp", "ks", "vs", "wp", "ws", "hp", "hs")}
    for l in range(depth):
        mod_p = [m[:, None, :] for m in jnp.split(mods[l, :n_p], 6, axis=-1)]
        mod_s = [jnp.repeat(m, t_s, axis=0)[None] for m in jnp.split(mods[l, n_p:n_c], 6, axis=-1)]
        prm = dict(mu=rw_mu[l], w0=rw_w0[l], w2=rw_w2[l], a0=rw_a0[l], a2=rw_a2[l], g2=rw_g2[l], k_k=rw_k_k[l],
                   k_a=rw_k_a[l], r_k=rw_r_k[l], lnx_g=rw_lnx_g[l], lnx_b=rw_lnx_b[l])
        w_in_bf = w_in[l].astype(BF16)
        w_out_bf = w_out[l].astype(BF16)
        if l % 2 == 0:
            w_router = jnp.zeros((d, LANES), F32)
            wg, wu, wd = (w[l // 2][None].astype(BF16) for w in (ffn_w_gate, ffn_w_up, ffn_w_down))
        else:
            w_router = jnp.pad(moe_w_router[l // 2], ((0, 0), (0, LANES - N_EXPERTS)))
            wg, wu, wd = (w[l // 2].astype(BF16) for w in (moe_w_gate, moe_w_up, moe_w_down))

        pa, pr = _inproj_call(xp, mod_p[1], mod_p[0], w_in_bf, PROMPT_TM)
        y_att = _moba_prompt_call(pa, rel_bias, t_own, t_prev)
        y_rw, s_new, sh_new = _rwkv_mix(pr, zero_shift, zero_state, prm, PROMPT_RW_TC, PROMPT_RW_TL, PROMPT_RW_CHUNK,
                                        PROMPT_RW_UNROLL)
        xp = _outproj_call(y_att, y_rw, xp, mod_p[2], w_out_bf, ln1_g[l], ln1_b[l], PROMPT_TM)
        if l % 2 == 0:
            xp = _ffn_call(xp, mod_p[4], mod_p[3], mod_p[5], w_router, wg, wu, wd, ln2_g[l], ln2_b[l], PROMPT_FF_TM,
                           FF_TILE)
        else:
            xp = _moe_routed_call(xp, mod_p[4], mod_p[3], mod_p[5], w_router, wg, wu, wd, ln2_g[l], ln2_b[l],
                                  PROMPT_MOE_TM, PROMPT_MOE_FF_TILE, PROMPT_MOE_SUB)
        outs["kp"].append(_to_pages(pa[..., D_ATT:2 * D_ATT]))
        outs["vp"].append(_to_pages(pa[..., 2 * D_ATT:]))
        outs["wp"].append(s_new)
        outs["hp"].append(sh_new)

        pa, pr = _inproj_call(xs, mod_s[1], mod_s[0], w_in_bf, rows_s)
        y_att, k_new, v_new = _moba_sample(pa.reshape(n_s, t_s, ATT_IN), cache_kt, cache_vt, l, page_table, rel_bias,
                                           ts_own, ts_prev)
        y_rw, s_new, sh_new = _rwkv_mix(pr.reshape(n_s, t_s, RW_IN), state_shift[l], state_wkv[l], prm, t_s, t_s, t_s)
        xs = _outproj_call(y_att.reshape(1, rows_s, D_ATT), y_rw.reshape(1, rows_s, D_RWKV), xs, mod_s[2], w_out_bf,
                           ln1_g[l], ln1_b[l], rows_s)
        xs = _ffn_call(xs, mod_s[4], mod_s[3], mod_s[5], w_router, wg, wu, wd, ln2_g[l], ln2_b[l], rows_s, FF_TILE)
        outs["ks"].append(k_new)
        outs["vs"].append(v_new)
        outs["ws"].append(s_new)
        outs["hs"].append(sh_new)

    st = lambda k: jnp.stack(outs[k])
    return (xp, xs.reshape(n_s, t_s, d), st("kp"), st("vp"), st("ks"), st("vs"), st("wp"), st("ws"), st("hp"), st("hs"))
```

```python
import functools
import math

import jax
import jax.numpy as jnp
from jax import lax
from jax.experimental import pallas as pl
from jax.experimental.pallas import tpu as pltpu

F32 = jnp.float32
BF16 = jnp.bfloat16
I32 = jnp.int32
HI = lax.Precision.HIGHEST

D_MODEL = 1024
HEAD_DIM = 64
H_ATT = 8
H_RWKV = 8
D_ATT = H_ATT * HEAD_DIM
D_RWKV = H_RWKV * HEAD_DIM
ATT_IN = 3 * D_ATT
LORA_W = 64
LORA_A = 64
LORA_G = 128
RW_IN = 3 * D_RWKV + LORA_W + LORA_A + LORA_G
MOBA_BLOCK = 256
MOBA_TOPK = 3
PAGE_SIZE = 128
N_BUCKETS = 32
MAX_EXACT = N_BUCKETS // 2
MAX_DISTANCE = 128
D_FF = 2816
N_EXPERTS = 8
DEPTH = 2
ALPHA = (2 * DEPTH) ** 0.25
LN_EPS = 1e-5
LNX_EPS = 64e-5
NEG_INF = -1e30
ATT_SCALE = HEAD_DIM ** -0.5

LANES = 128
HEAD_PAIR = LANES // HEAD_DIM
BF16_ROWS = 16
VMEM_LIMIT = 56 * 1024 * 1024


def _cparams(*sem):
    return pltpu.CompilerParams(dimension_semantics=sem, vmem_limit_bytes=VMEM_LIMIT)


def _sigmoid(x):
    return 1.0 / (1.0 + jnp.exp(-x))


def _dot_nt(a, b, precision=None):
    return lax.dot_general(a, b, (((1,), (1,)), ((), ())), precision=precision, preferred_element_type=F32)


def _dot_tn(a, b, precision=None):
    return lax.dot_general(a, b, (((0,), (0,)), ((), ())), precision=precision, preferred_element_type=F32)


def _dot(a, b, precision=None):
    return jnp.dot(a, b, precision=precision, preferred_element_type=F32)


def _ada_kernel(c_ref, w_ref, b_ref, o_ref):
    c = c_ref[...]
    o_ref[0] = _dot(c * _sigmoid(c), w_ref[0], precision=HI) + b_ref[0]


def _ada_call(c_all, w_ada, b_ada):
    n = c_all.shape[0]
    depth, d, cols = w_ada.shape
    tn = 768
    return pl.pallas_call(
        _ada_kernel,
        grid=(depth, cols // tn),
        in_specs=[pl.BlockSpec((n, d), lambda l, j: (0, 0)),
                  pl.BlockSpec((1, d, tn), lambda l, j: (l, 0, j)),
                  pl.BlockSpec((1, 1, tn), lambda l, j: (l, 0, j))],
        out_specs=pl.BlockSpec((1, n, tn), lambda l, j: (l, 0, j)),
        out_shape=jax.ShapeDtypeStruct((depth, n, cols), F32),
        compiler_params=_cparams("parallel", "parallel"),
        name="ada_mod",
    )(c_all, w_ada, b_ada.reshape(depth, 1, cols))


def _bucket_of(dist):
    d = jnp.maximum(dist, 0)
    log_ratio = jnp.log(jnp.maximum(d, 1).astype(F32) / MAX_EXACT) / math.log(MAX_DISTANCE / MAX_EXACT)
    large = jnp.minimum(MAX_EXACT + (log_ratio * (N_BUCKETS - MAX_EXACT)).astype(I32), N_BUCKETS - 1)
    return jnp.where(d < MAX_EXACT, d, large)


def _bias_kernel(rb_ref, *o_refs, offsets, key_major):
    for o_ref, off, km in zip(o_refs, offsets, key_major):
        n_h, rows, cols = o_ref.shape
        dist = (lax.broadcasted_iota(I32, (rows, cols), 0) - lax.broadcasted_iota(I32, (rows, cols), 1))
        dist = (-dist if km else dist) + off
        bucket = _bucket_of(dist)
        for h in range(n_h):
            acc = jnp.zeros((rows, cols), F32)
            for b in range(N_BUCKETS):
                acc = jnp.where(bucket == b, rb_ref[b, h], acc)
            o_ref[h] = jnp.where(dist >= 0, acc, NEG_INF)


def _bias_call(rel_bias, shapes, offsets, key_major=None):
    n_h = rel_bias.shape[1]
    key_major = tuple(key_major) if key_major is not None else (False,) * len(shapes)
    return pl.pallas_call(
        functools.partial(_bias_kernel, offsets=tuple(offsets), key_major=key_major),
        in_specs=[pl.BlockSpec(memory_space=pltpu.SMEM)],
        out_specs=[pl.BlockSpec(memory_space=pltpu.VMEM) for _ in shapes],
        out_shape=[jax.ShapeDtypeStruct((n_h,) + s, F32) for s in shapes],
        name="rel_bias_tables",
    )(rel_bias)


def _inproj_kernel(x_ref, sc_ref, sh_ref, w_ref, oa_ref, or_ref, *, chunk):
    u = (x_ref[0] * (1.0 + sc_ref[0]) + sh_ref[0]).astype(BF16)
    att = oa_ref.shape[-1]
    for c0 in range(0, w_ref.shape[1], chunk):
        res = _dot(u, w_ref[:, c0:c0 + chunk])
        if c0 < att:
            oa_ref[0, :, c0:c0 + chunk] = res
        else:
            or_ref[0, :, c0 - att:c0 - att + chunk] = res


def _mod_spec(mod, tm):
    if mod.shape[1] == 1:
        return pl.BlockSpec((1, 1, mod.shape[2]), lambda n, i: (n, 0, 0))
    return pl.BlockSpec((1, tm, mod.shape[2]), lambda n, i: (n, i, 0))


def _inproj_call(x, sc, sh, w_bf, tm):
    n, l, d = x.shape
    cols = w_bf.shape[1]
    return pl.pallas_call(
        functools.partial(_inproj_kernel, chunk=256),
        grid=(n, l // tm),
        in_specs=[pl.BlockSpec((1, tm, d), lambda n, i: (n, i, 0)),
                  _mod_spec(sc, tm), _mod_spec(sh, tm),
                  pl.BlockSpec((d, cols), lambda n, i: (0, 0))],
        out_specs=[pl.BlockSpec((1, tm, ATT_IN), lambda n, i: (n, i, 0)),
                   pl.BlockSpec((1, tm, RW_IN), lambda n, i: (n, i, 0))],
        out_shape=[jax.ShapeDtypeStruct((n, l, ATT_IN), F32),
                   jax.ShapeDtypeStruct((n, l, RW_IN), F32)],
        compiler_params=_cparams("parallel", "parallel"),
        name="in_proj",
    )(x, sc, sh, w_bf)


def _moba_prompt_kernel(rb_ref, q_ref, k_ref, v_ref, town_ref, tprev_ref, o_ref, kb_ref, vb_ref, km_ref,
                        *, n_blk):
    hp = pl.program_id(1)
    i = pl.program_id(2)
    blk = MOBA_BLOCK

    @pl.when(i == 0)
    def _():
        k = k_ref[0]
        kb_ref[...] = k.astype(BF16)
        km_ref[...] = jnp.sum(k.reshape(n_blk, blk, LANES), axis=1) * (1.0 / blk)
        for c in range(n_blk):
            vt = v_ref[0, c * blk:(c + 1) * blk, :].T.astype(BF16)
            for h2 in range(HEAD_PAIR):
                vb_ref[h2, :HEAD_DIM, c * blk:(c + 1) * blk] = vt[h2 * HEAD_DIM:(h2 + 1) * HEAD_DIM]
                vb_ref[h2, HEAD_DIM:, c * blk:(c + 1) * blk] = jnp.ones((BF16_ROWS, blk), BF16)

    qt = q_ref[0].T
    dim_row = lax.broadcasted_iota(I32, qt.shape, 0)
    own0 = pl.multiple_of(i * blk, blk)
    heads = range(HEAD_PAIR)
    bidx = lax.broadcasted_iota(I32, (n_blk, blk), 0)
    past = bidx < i
    qt_h = [jnp.where((dim_row // HEAD_DIM) == h2, qt, 0.0) for h2 in heads]
    qb = [(z * ATT_SCALE).astype(BF16) for z in qt_h]
    sel = []
    for h2 in heads:
        g = jnp.where(past, _dot(km_ref[...], qt_h[h2], precision=HI), NEG_INF)
        rank = jnp.zeros(g.shape, I32)
        for jp in range(n_blk):
            gj = g[jp:jp + 1, :]
            rank = rank + ((gj > g) | ((gj == g) & (jp < bidx))).astype(I32)
        sel.append(jnp.where((rank < MOBA_TOPK) & past, 1.0, 0.0))
    b_far = [rb_ref[N_BUCKETS - 1, hp * HEAD_PAIR + h2] for h2 in heads]

    def attend(blocks, carry):
        s = []
        for r0, tables, rows in blocks:
            kb = kb_ref[pl.ds(r0, blk), :]
            s_b = [_dot(kb, qb[h2]) for h2 in heads]
            if tables is not None:
                s_b = [s_b[h2] + tables[h2] for h2 in heads]
            if rows is not None:
                s_b = [s_b[h2] + rows[h2] for h2 in heads]
            s.append(s_b)
        out = []
        for h2 in heads:
            m_new = functools.reduce(jnp.maximum, [jnp.max(s_b[h2], axis=0, keepdims=True) for s_b in s])
            if carry is not None:
                m, acc = carry[h2]
                m_new = jnp.maximum(m, m_new)
                acc = jnp.exp(m - m_new) * acc
            else:
                acc = 0.0
            for (r0, _, _), s_b in zip(blocks, s):
                p = jnp.exp(s_b[h2] - m_new).astype(BF16)
                acc = acc + _dot(vb_ref[h2, :, pl.ds(r0, blk)], p)
            out.append((m_new, acc))
        return tuple(out)

    def picked_row(j, h2, value):
        hit = jnp.sum(jnp.where(bidx == j, sel[h2], 0.0), axis=0, keepdims=True) > 0.5
        return jnp.where(hit, value, NEG_INF)

    def far_block(j):
        return pl.multiple_of(j * blk, blk), None, [picked_row(j, h2, b_far[h2]) for h2 in heads]

    def near_block(j):
        return pl.multiple_of(j * blk, blk), [tprev_ref[h2] for h2 in heads], [picked_row(j, h2, 0.0) for h2 in heads]

    carry = attend([(own0, [town_ref[h2] for h2 in heads], None)], None)
    n_far = jnp.maximum(i - 1, 0)
    grp = PROMPT_KEY_GROUP
    carry = lax.fori_loop(0, n_far // grp,
                          lambda j, c: attend([far_block(grp * j + u) for u in range(grp)], c), carry)
    carry = lax.fori_loop(grp * (n_far // grp), n_far, lambda j, c: attend([far_block(j)], c), carry)
    carry = lax.fori_loop(n_far, i, lambda j, c: attend([near_block(j)], c), carry)
    o_ref[0] = jnp.concatenate([acc[:HEAD_DIM] / acc[HEAD_DIM:HEAD_DIM + 1] for _, acc in carry], axis=0).T


def _moba_prompt_call(p_att, rel_bias, t_own, t_prev):
    n, l, _ = p_att.shape
    n_blk = l // MOBA_BLOCK
    n_hp = H_ATT // HEAD_PAIR
    tbl = pl.BlockSpec((HEAD_PAIR, MOBA_BLOCK, MOBA_BLOCK), lambda n, h, i: (h, 0, 0))
    return pl.pallas_call(
        functools.partial(_moba_prompt_kernel, n_blk=n_blk),
        grid=(n, n_hp, n_blk),
        in_specs=[pl.BlockSpec(memory_space=pltpu.SMEM),
                  pl.BlockSpec((1, MOBA_BLOCK, LANES), lambda n, h, i: (n, i, h)),
                  pl.BlockSpec((1, l, LANES), lambda n, h, i: (n, 0, n_hp + h)),
                  pl.BlockSpec((1, l, LANES), lambda n, h, i: (n, 0, 2 * n_hp + h)),
                  tbl, tbl],
        out_specs=pl.BlockSpec((1, MOBA_BLOCK, LANES), lambda n, h, i: (n, i, h)),
        out_shape=jax.ShapeDtypeStruct((n, l, D_ATT), F32),
        scratch_shapes=[pltpu.VMEM((l, LANES), BF16), pltpu.VMEM((HEAD_PAIR, HEAD_DIM + BF16_ROWS, l), BF16),
                        pltpu.VMEM((n_blk, LANES), F32)],
        compiler_params=_cparams("parallel", "parallel", "arbitrary"),
        name="moba_prompt",
    )(rel_bias, p_att, p_att, p_att, t_own, t_prev)


def _softplus(x):
    return jnp.maximum(x, 0.0) + jnp.log(1.0 + jnp.exp(-jnp.abs(x)))


def _rwkv_pre_kernel(p_ref, prev_ref, sh0_ref, mu_ref, w0_ref, w2_ref, a0_ref, a2_ref, g2_ref, kkw_ref, kaw_ref,
                     r_o, lw_o, k_o, v_o, kk_o, b_o, g_o):
    i = pl.program_id(1)
    p = p_ref[0]
    prev_row = jnp.where(i == 0, sh0_ref[0], prev_ref[0][7:8])
    row = lax.broadcasted_iota(I32, p.shape, 0)
    p_prev = jnp.where(row == 0, prev_row, pltpu.roll(p, 1, 0))
    xm = p + (p_prev - p) * mu_ref[...]
    o = 3 * D_RWKV
    r = xm[:, :D_RWKV]
    k = xm[:, D_RWKV:2 * D_RWKV]
    v = xm[:, 2 * D_RWKV:o]
    xw = xm[:, o:o + LORA_W]
    xa = xm[:, o + LORA_W:o + LORA_W + LORA_A]
    xg = xm[:, o + LORA_W + LORA_A:]
    w_log = -_softplus(-(w0_ref[...] + _dot(jnp.tanh(xw), w2_ref[...], precision=HI))) - 0.5
    a = _sigmoid(a0_ref[...] + _dot(xa, a2_ref[...], precision=HI))
    g = _dot(_sigmoid(xg), g2_ref[...], precision=HI)
    kk = k * kkw_ref[...]
    same_head = (lax.broadcasted_iota(I32, (D_RWKV, D_RWKV), 0) // HEAD_DIM
                 == lax.broadcasted_iota(I32, (D_RWKV, D_RWKV), 1) // HEAD_DIM)
    ss = _dot(kk * kk, jnp.where(same_head, 1.0, 0.0), precision=HI)
    kk = kk / jnp.maximum(jnp.sqrt(ss), 1e-12)
    r_o[0] = r
    lw_o[0] = -jnp.exp(w_log)
    k_o[0] = k * (1.0 + (a - 1.0) * kaw_ref[...])
    v_o[0] = v
    kk_o[0] = kk
    b_o[0] = kk * a
    g_o[0] = g


def _rwkv_pre_call(p_rw, shift0, mu, w0, w2, a0, a2, g2, k_k, k_a, tc):
    n, l, _ = p_rw.shape
    row = lambda z: z.reshape(1, -1)
    full = lambda z: pl.BlockSpec(z.shape, lambda n, i: (0,) * z.ndim)
    params = [row(mu), row(w0), w2, row(a0), a2, g2, row(k_k), row(k_a)]
    out_spec = pl.BlockSpec((1, tc, D_RWKV), lambda n, i: (n, i, 0))
    return pl.pallas_call(
        _rwkv_pre_kernel,
        grid=(n, l // tc),
        in_specs=[pl.BlockSpec((1, tc, RW_IN), lambda n, i: (n, i, 0)),
                  pl.BlockSpec((1, 8, RW_IN), lambda n, i: (n, jnp.maximum(i * (tc // 8) - 1, 0), 0)),
                  pl.BlockSpec((1, 1, RW_IN), lambda n, i: (n, 0, 0))] + [full(z) for z in params],
        out_specs=[out_spec] * 7,
        out_shape=[jax.ShapeDtypeStruct((n, l, D_RWKV), F32)] * 7,
        compiler_params=_cparams("parallel", "parallel"),
        name="rwkv_pre",
    )(p_rw, p_rw, shift0.reshape(n, 1, RW_IN), *params)


def _chunk_cumsum(x):
    rows = x.shape[0]
    row = lax.broadcasted_iota(I32, x.shape, 0)
    s = 1
    while s < rows:
        x = x + jnp.where(row >= s, pltpu.roll(x, s, 0), 0.0)
        s *= 2
    return x


def _rwkv_scan_kernel(r_ref, lw_ref, k_ref, v_ref, kk_ref, b_ref, g_ref, s0_ref, rk_ref, lg_ref, lb_ref,
                      y_ref, sf_ref, st_ref, yraw_ref, *, chunk, unroll, mxu):
    i = pl.program_id(2)
    C = chunk
    tl = r_ref.shape[1]
    n_dbl = max(C.bit_length() - 2, 0)

    @pl.when(i == 0)
    def _():
        st_ref[...] = s0_ref[0, 0]

    lane = lax.broadcasted_iota(I32, (1, LANES), 1)
    head0 = lane < HEAD_DIM
    ti = lax.broadcasted_iota(I32, (C, C), 0)
    si = lax.broadcasted_iota(I32, (C, C), 1)
    strict = si < ti
    incl = si <= ti
    eye_c = jnp.where(si == ti, 1.0, 0.0)
    rj = lax.broadcasted_iota(I32, (LANES, LANES), 0)
    cj = lax.broadcasted_iota(I32, (LANES, LANES), 1)
    same_head = (rj // HEAD_DIM) == (cj // HEAD_DIM)
    eye_l = rj == cj
    mm = lambda a, b: _dot(a.astype(mxu), b.astype(mxu))
    mm_nt = lambda a, b: _dot_nt(a.astype(mxu), b.astype(mxu))
    mm_tn = lambda a, b: _dot_tn(a.astype(mxu), b.astype(mxu))

    def chunk_operands(rows):
        lw = lw_ref[0, rows, :]
        r = r_ref[0, rows, :]
        k = k_ref[0, rows, :]
        v = v_ref[0, rows, :]
        kk = kk_ref[0, rows, :]
        b = b_ref[0, rows, :]
        cum = _chunk_cumsum(lw)
        c_end = cum[C - 1:C, :]
        e_neg = jnp.exp(-cum)
        e_end = jnp.exp(c_end - cum)
        al = -kk * jnp.exp(cum - lw)
        rt = r * jnp.exp(cum)
        return dict(v=v, al=al, rt=rt, bt=b * e_neg, kt=k * e_neg, bt2=b * e_end, kt2=k * e_end,
                    lhs=jnp.concatenate([al, rt], axis=0), c_end=c_end)

    def group_terms(rows_list):
        ops = [chunk_operands(rw) for rw in rows_list]
        ch = [(u, h) for u in range(len(ops)) for h in range(HEAD_PAIR)]
        mine = [head0, jnp.logical_not(head0)]
        lhs = [jnp.where(mine[h], ops[u]["lhs"], 0.0) for u, h in ch]
        ab = [mm_nt(lhs[n], ops[u]["bt"]) for n, (u, h) in enumerate(ch)]
        ak = [mm_nt(lhs[n], ops[u]["kt"]) for n, (u, h) in enumerate(ch)]
        m_ab = [jnp.where(strict, z[:C], 0.0) for z in ab]
        m_ak = [jnp.where(strict, z[:C], 0.0) for z in ak]
        b_br = [jnp.where(incl, z[C:], 0.0) for z in ab]
        b_kr = [jnp.where(incl, z[C:], 0.0) for z in ak]
        t_inv = [eye_c + z for z in m_ab]
        pk = m_ab
        for _ in range(n_dbl):
            pk = [mm(z, z) for z in pk]
            t_inv = [t + mm(z, t) for z, t in zip(pk, t_inv)]
        mv = [mm(m_ak[n], ops[u]["v"]) for n, (u, h) in enumerate(ch)]
        aw = [mm(t_inv[n], jnp.concatenate([ops[u]["al"], mv[n]], axis=1)) for n, (u, h) in enumerate(ch)]
        baw = [mm(b_br[n], aw[n]) for n in range(len(ch))]
        bkv = [mm(b_kr[n], ops[u]["v"]) for n, (u, h) in enumerate(ch)]
        terms = []
        for u, o in enumerate(ops):
            n0, n1 = u * HEAD_PAIR, u * HEAD_PAIR + 1
            pick = lambda f: jnp.where(head0, f(n0), f(n1))
            a2 = pick(lambda n: aw[n][:, :LANES])
            w = pick(lambda n: aw[n][:, LANES:])
            r2 = o["rt"] + pick(lambda n: baw[n][:, :LANES])
            y0 = pick(lambda n: baw[n][:, LANES:] + bkv[n])
            tc_t = jnp.where(same_head, mm_tn(o["bt2"], a2), 0.0)
            gc_t = jnp.where(same_head, mm_tn(jnp.concatenate([o["bt2"], o["kt2"]], axis=0),
                                              jnp.concatenate([w, o["v"]], axis=0)), 0.0)
            dec_col = jnp.sum(jnp.where(eye_l, jnp.exp(o["c_end"]), 0.0), axis=1, keepdims=True)
            terms.append((r2, y0, tc_t, gc_t, dec_col))
        return terms

    def group_body(c, carry):
        rows = [pl.ds(pl.multiple_of((c * unroll + u) * C, C), C) for u in range(unroll)]
        terms = group_terms(rows)
        st = st_ref[...]
        for rw, (r2, y0, tc_t, gc_t, dec_col) in zip(rows, terms):
            yraw_ref[rw, :] = mm(r2, st) + y0
            st = dec_col * st + mm(tc_t, st) + gc_t
        st_ref[...] = st
        return carry

    lax.fori_loop(0, tl // (C * unroll), group_body, 0)
    sf_ref[0, 0] = st_ref[...]

    def head_sum(z):
        s0 = jnp.sum(jnp.where(head0, z, 0.0), axis=1, keepdims=True)
        s1 = jnp.sum(jnp.where(head0, 0.0, z), axis=1, keepdims=True)
        return jnp.where(head0, s0, s1)

    y = yraw_ref[...]
    d = y - head_sum(y) * (1.0 / HEAD_DIM)
    var = head_sum(d * d) * (1.0 / HEAD_DIM)
    yn = d * lax.rsqrt(var + LNX_EPS) * lg_ref[...] + lb_ref[...]
    bonus = head_sum(r_ref[0] * k_ref[0] * rk_ref[...]) * v_ref[0]
    y_ref[0] = (yn + bonus) * g_ref[0]


def _rwkv_scan_call(ops, st0, r_k, lnx_g, lnx_b, tl, chunk, unroll):
    n, l, _ = ops[0].shape
    n_hp = H_RWKV // HEAD_PAIR
    mxu = BF16 if chunk >= 16 else F32
    tok = pl.BlockSpec((1, tl, LANES), lambda n, h, i: (n, i, h))
    par = pl.BlockSpec((1, LANES), lambda n, h, i: (0, h))
    st_spec = pl.BlockSpec((1, 1, LANES, LANES), lambda n, h, i: (n, h, 0, 0))
    return pl.pallas_call(
        functools.partial(_rwkv_scan_kernel, chunk=chunk, unroll=unroll, mxu=mxu),
        grid=(n, n_hp, l // tl),
        in_specs=[tok] * 7 + [st_spec, par, par, par],
        out_specs=[tok, st_spec],
        out_shape=[jax.ShapeDtypeStruct((n, l, D_RWKV), F32),
                   jax.ShapeDtypeStruct((n, n_hp, LANES, LANES), F32)],
        scratch_shapes=[pltpu.VMEM((LANES, LANES), F32), pltpu.VMEM((tl, LANES), F32)],
        compiler_params=_cparams("parallel", "parallel", "arbitrary"),
        name="rwkv_scan",
    )(*ops, st0, r_k.reshape(1, D_RWKV), lnx_g.reshape(1, D_RWKV), lnx_b.reshape(1, D_RWKV))


def _state_to_pairs(s):
    n, h = s.shape[:2]
    st = jnp.swapaxes(s, -1, -2).reshape(n, h // HEAD_PAIR, HEAD_PAIR, HEAD_DIM, HEAD_DIM)
    z = jnp.zeros_like(st[:, :, 0])
    top = jnp.concatenate([st[:, :, 0], z], axis=-1)
    bot = jnp.concatenate([z, st[:, :, 1]], axis=-1)
    return jnp.concatenate([top, bot], axis=-2)


def _pairs_to_state(sp):
    n, n_hp = sp.shape[:2]
    d0 = sp[:, :, :HEAD_DIM, :HEAD_DIM]
    d1 = sp[:, :, HEAD_DIM:, HEAD_DIM:]
    st = jnp.stack([d0, d1], axis=2).reshape(n, n_hp * HEAD_PAIR, HEAD_DIM, HEAD_DIM)
    return jnp.swapaxes(st, -1, -2)


def _rwkv_mix(p_rw, shift0, s0, prm, tc, tl, chunk, unroll=1):
    ops = _rwkv_pre_call(p_rw, shift0, prm["mu"], prm["w0"], prm["w2"], prm["a0"], prm["a2"], prm["g2"],
                         prm["k_k"], prm["k_a"], tc)
    y, sp = _rwkv_scan_call(ops, _state_to_pairs(s0), prm["r_k"], prm["lnx_g"], prm["lnx_b"], tl, chunk, unroll)
    return y, _pairs_to_state(sp), p_rw[:, -1]


def _layer_norm(z, g, b):
    mu = jnp.mean(z, axis=-1, keepdims=True)
    d = z - mu
    var = jnp.mean(d * d, axis=-1, keepdims=True)
    return d * lax.rsqrt(var + LN_EPS) * g + b


def _outproj_kernel(ya_ref, yr_ref, x_ref, gt_ref, w_ref, g_ref, b_ref, o_ref):
    o = (_dot(ya_ref[0].astype(BF16), w_ref[:D_ATT, :]) + _dot(yr_ref[0].astype(BF16), w_ref[D_ATT:, :]))
    o_ref[0] = _layer_norm(ALPHA * x_ref[0] + (1.0 + gt_ref[0]) * o, g_ref[...], b_ref[...])


def _outproj_call(y_att, y_rw, x, gt, w_bf, ln_g, ln_b, tm):
    n, l, d = x.shape
    row = pl.BlockSpec((1, d), lambda n, i: (0, 0))
    return pl.pallas_call(
        _outproj_kernel,
        grid=(n, l // tm),
        in_specs=[pl.BlockSpec((1, tm, D_ATT), lambda n, i: (n, i, 0)),
                  pl.BlockSpec((1, tm, D_RWKV), lambda n, i: (n, i, 0)),
                  pl.BlockSpec((1, tm, d), lambda n, i: (n, i, 0)),
                  _mod_spec(gt, tm),
                  pl.BlockSpec(w_bf.shape, lambda n, i: (0, 0)), row, row],
        out_specs=pl.BlockSpec((1, tm, d), lambda n, i: (n, i, 0)),
        out_shape=jax.ShapeDtypeStruct((n, l, d), F32),
        compiler_params=_cparams("parallel", "parallel"),
        name="out_proj_ln",
    )(y_att, y_rw, x, gt, w_bf, ln_g.reshape(1, d), ln_b.reshape(1, d))


def _ffn_kernel(x_ref, sc_ref, sh_ref, gt_ref, wr_ref, wg_ref, wu_ref, wd_ref, g_ref, b_ref, o_ref,
                u_ref, acc_ref, gate_ref, *, n_experts):
    e = pl.program_id(2)
    f = pl.program_id(3)
    last = (e == pl.num_programs(2) - 1) & (f == pl.num_programs(3) - 1)

    @pl.when((e == 0) & (f == 0))
    def _():
        u = x_ref[0] * (1.0 + sc_ref[0]) + sh_ref[0]
        u_ref[...] = u.astype(BF16)
        acc_ref[...] = jnp.zeros_like(acc_ref)
        if n_experts > 1:
            logits = _dot(u, wr_ref[...], precision=HI)
            lane = lax.broadcasted_iota(I32, logits.shape, 1)
            logits = jnp.where(lane < n_experts, logits, NEG_INF)
            v1 = jnp.max(logits, axis=1, keepdims=True)
            i1 = jnp.min(jnp.where(logits == v1, lane, LANES), axis=1, keepdims=True)
            rest = jnp.where(lane == i1, NEG_INF, logits)
            v2 = jnp.max(rest, axis=1, keepdims=True)
            i2 = jnp.min(jnp.where(rest == v2, lane, LANES), axis=1, keepdims=True)
            w2 = 1.0 / (1.0 + jnp.exp(v1 - v2))
            gate_ref[...] = jnp.where(lane == i1, 1.0 - w2, 0.0) + jnp.where(lane == i2, w2, 0.0)

    u = u_ref[...]
    hg = _dot(u, wg_ref[0])
    hu = _dot(u, wu_ref[0])
    h = hg * _sigmoid(hg) * hu
    if n_experts > 1:
        lane = lax.broadcasted_iota(I32, gate_ref.shape, 1)
        h = h * jnp.sum(jnp.where(lane == e, gate_ref[...], 0.0), axis=1, keepdims=True)
    acc_ref[...] += _dot(h.astype(BF16), wd_ref[0])

    @pl.when(last)
    def _():
        o_ref[0] = _layer_norm(ALPHA * x_ref[0] + (1.0 + gt_ref[0]) * acc_ref[...], g_ref[...], b_ref[...])


def _ffn_call(x, sc, sh, gt, w_router, wg_bf, wu_bf, wd_bf, ln_g, ln_b, tm, tf):
    n, l, d = x.shape
    n_e, _, d_ff = wg_bf.shape
    row = pl.BlockSpec((1, d), lambda n, i, e, f: (0, 0))
    mod = lambda m: (pl.BlockSpec((1, 1, d), lambda n, i, e, f: (n, 0, 0)) if m.shape[1] == 1
                     else pl.BlockSpec((1, tm, d), lambda n, i, e, f: (n, i, 0)))
    return pl.pallas_call(
        functools.partial(_ffn_kernel, n_experts=n_e),
        grid=(n, l // tm, n_e, d_ff // tf),
        in_specs=[pl.BlockSpec((1, tm, d), lambda n, i, e, f: (n, i, 0)),
                  mod(sc), mod(sh), mod(gt),
                  pl.BlockSpec(w_router.shape, lambda n, i, e, f: (0, 0)),
                  pl.BlockSpec((1, d, tf), lambda n, i, e, f: (e, 0, f)),
                  pl.BlockSpec((1, d, tf), lambda n, i, e, f: (e, 0, f)),
                  pl.BlockSpec((1, tf, d), lambda n, i, e, f: (e, f, 0)),
                  row, row],
        out_specs=pl.BlockSpec((1, tm, d), lambda n, i, e, f: (n, i, 0)),
        out_shape=jax.ShapeDtypeStruct((n, l, d), F32),
        scratch_shapes=[pltpu.VMEM((tm, d), BF16), pltpu.VMEM((tm, d), F32), pltpu.VMEM((tm, LANES), F32)],
        compiler_params=_cparams("parallel", "parallel", "arbitrary", "arbitrary"),
        name="ffn_ln" if n_e == 1 else "moe_ln",
    )(x, sc, sh, gt, w_router, wg_bf, wu_bf, wd_bf, ln_g.reshape(1, d), ln_b.reshape(1, d))


ROUTE_BLOCK = 256


def _router_kernel(x_ref, sc_ref, sh_ref, wr_ref, gate_o, pos_o, post_o, cnt_o, *, n_experts):
    u = x_ref[0] * (1.0 + sc_ref[0]) + sh_ref[0]
    tm = u.shape[0]
    logits = _dot(u, wr_ref[...], precision=HI)
    lane = lax.broadcasted_iota(I32, logits.shape, 1)
    logits = jnp.where(lane < n_experts, logits, NEG_INF)
    v1 = jnp.max(logits, axis=1, keepdims=True)
    i1 = jnp.min(jnp.where(logits == v1, lane, LANES), axis=1, keepdims=True)
    rest = jnp.where(lane == i1, NEG_INF, logits)
    v2 = jnp.max(rest, axis=1, keepdims=True)
    i2 = jnp.min(jnp.where(rest == v2, lane, LANES), axis=1, keepdims=True)
    w2 = 1.0 / (1.0 + jnp.exp(v1 - v2))
    gate_o[0] = jnp.where(lane == i1, 1.0 - w2, 0.0) + jnp.where(lane == i2, w2, 0.0)
    chosen = (lane == i1) | (lane == i2)
    sel = jnp.where(chosen, 1.0, 0.0)
    rb = ROUTE_BLOCK
    tri = jnp.where(lax.broadcasted_iota(I32, (rb, rb), 1) < lax.broadcasted_iota(I32, (rb, rb), 0), 1.0, 0.0).astype(BF16)
    carry = jnp.zeros((1, LANES), F32)
    for c in range(tm // rb):
        blk = sel[c * rb:(c + 1) * rb]
        pos = _dot(tri, blk.astype(BF16)) + carry
        pos = jnp.where(chosen[c * rb:(c + 1) * rb], pos, -1.0)
        pos_o[0, c * rb:(c + 1) * rb, :] = pos
        post_o[0, :, c * rb:(c + 1) * rb] = pos.T
        carry = carry + jnp.sum(blk, axis=0, keepdims=True)
    cnt_o[0, 0] = jnp.broadcast_to(carry, (8, LANES)).astype(I32)


def _router_call(x, sc, sh, w_router, n_experts, tm):
    n, l, d = x.shape
    mod = pl.BlockSpec((1, 1, d), lambda n, i: (n, 0, 0))
    tok = pl.BlockSpec((1, tm, LANES), lambda n, i: (n, i, 0))
    return pl.pallas_call(
        functools.partial(_router_kernel, n_experts=n_experts),
        grid=(n, l // tm),
        in_specs=[pl.BlockSpec((1, tm, d), lambda n, i: (n, i, 0)), mod, mod,
                  pl.BlockSpec(w_router.shape, lambda n, i: (0, 0))],
        out_specs=[tok, tok, pl.BlockSpec((1, LANES, tm), lambda n, i: (n, 0, i)),
                   pl.BlockSpec((1, 1, 8, LANES), lambda n, i: (n, i, 0, 0))],
        out_shape=[jax.ShapeDtypeStruct((n, l, LANES), F32), jax.ShapeDtypeStruct((n, l, LANES), F32),
                   jax.ShapeDtypeStruct((n, LANES, l), F32), jax.ShapeDtypeStruct((n, l // tm, 8, LANES), I32)],
        compiler_params=_cparams("parallel", "parallel"),
        name="moe_router",
    )(x, sc, sh, w_router)


def _moe_routed_kernel(cnt_ref, x_ref, sc_ref, sh_ref, gt_ref, gate_ref, pos_ref, post_ref, wg_ref, wu_ref, wd_ref,
                       g_ref, b_ref, o_ref, u_ref, xs_ref, oacc_ref, *, n_experts, sub):
    n = pl.program_id(0)
    i = pl.program_id(1)
    e = pl.program_id(2)
    f = pl.program_id(3)
    n_f = pl.num_programs(3)
    tm = x_ref.shape[1]
    count = cnt_ref[(n * pl.num_programs(1) + i) * n_experts + e]
    n_sub = (count + sub - 1) // sub

    @pl.when((e == 0) & (f == 0))
    def _():
        u_ref[...] = (x_ref[0] * (1.0 + sc_ref[0]) + sh_ref[0]).astype(BF16)
        o_ref[...] = jnp.zeros_like(o_ref)

    @pl.when(f == 0)
    def _():
        erow = lax.broadcasted_iota(I32, (8, tm), 0)
        pos_row = jnp.sum(jnp.where(erow == e, post_ref[0, 0:8, :], 0.0), axis=0, keepdims=True)

        def gather(s, c):
            rows = pl.ds(pl.multiple_of(s * sub, BF16_ROWS), sub)
            want = (lax.broadcasted_iota(I32, (sub, tm), 0) + s * sub).astype(F32)
            onehot = jnp.where(pos_row == want, 1.0, 0.0).astype(BF16)
            xs_ref[rows, :] = _dot(onehot, u_ref[...]).astype(BF16)
            oacc_ref[rows, :] = jnp.zeros((sub, oacc_ref.shape[1]), F32)
            return c

        lax.fori_loop(0, n_sub, gather, 0)

    def expert(s, c):
        rows = pl.ds(pl.multiple_of(s * sub, BF16_ROWS), sub)
        xs = xs_ref[rows, :]
        hg = _dot(xs, wg_ref[0])
        h = hg * _sigmoid(hg) * _dot(xs, wu_ref[0])
        oacc_ref[rows, :] += _dot(h.astype(BF16), wd_ref[0])
        return c

    lax.fori_loop(0, n_sub, expert, 0)

    @pl.when(f == n_f - 1)
    def _():
        lane = lax.broadcasted_iota(I32, (tm, LANES), 1)
        pos_col = jnp.sum(jnp.where(lane == e, pos_ref[0], 0.0), axis=1, keepdims=True)
        gate_col = jnp.sum(jnp.where(lane == e, gate_ref[0], 0.0), axis=1, keepdims=True)

        def scatter(s, c):
            rows = pl.ds(pl.multiple_of(s * sub, BF16_ROWS), sub)
            want = (lax.broadcasted_iota(I32, (tm, sub), 1) + s * sub).astype(F32)
            w = jnp.where(pos_col == want, gate_col, 0.0).astype(BF16)
            o_ref[0] += _dot(w, oacc_ref[rows, :].astype(BF16))
            return c

        lax.fori_loop(0, n_sub, scatter, 0)

    @pl.when((e == n_experts - 1) & (f == n_f - 1))
    def _():
        o_ref[0] = _layer_norm(ALPHA * x_ref[0] + (1.0 + gt_ref[0]) * o_ref[0], g_ref[...], b_ref[...])


def _moe_routed_call(x, sc, sh, gt, w_router, wg_bf, wu_bf, wd_bf, ln_g, ln_b, tm, tf, sub):
    n, l, d = x.shape
    n_e, _, d_ff = wg_bf.shape
    gate, pos, post, cnt = _router_call(x, sc, sh, w_router, n_e, tm)
    cnt_flat = cnt[:, :, 0, :n_e].reshape(-1)
    n_sub_max = -(-tm // sub)
    row = pl.BlockSpec((1, d), lambda n, i, e, f, c: (0, 0))
    mod = pl.BlockSpec((1, 1, d), lambda n, i, e, f, c: (n, 0, 0))
    tok = pl.BlockSpec((1, tm, LANES), lambda n, i, e, f, c: (n, i, 0))
    return pl.pallas_call(
        functools.partial(_moe_routed_kernel, n_experts=n_e, sub=sub),
        grid_spec=pltpu.PrefetchScalarGridSpec(
            num_scalar_prefetch=1,
            grid=(n, l // tm, n_e, d_ff // tf),
            in_specs=[pl.BlockSpec((1, tm, d), lambda n, i, e, f, c: (n, i, 0)), mod, mod, mod, tok, tok,
                      pl.BlockSpec((1, LANES, tm), lambda n, i, e, f, c: (n, 0, i)),
                      pl.BlockSpec((1, d, tf), lambda n, i, e, f, c: (e, 0, f)),
                      pl.BlockSpec((1, d, tf), lambda n, i, e, f, c: (e, 0, f)),
                      pl.BlockSpec((1, tf, d), lambda n, i, e, f, c: (e, f, 0)),
                      row, row],
            out_specs=pl.BlockSpec((1, tm, d), lambda n, i, e, f, c: (n, i, 0)),
            scratch_shapes=[pltpu.VMEM((tm, d), BF16), pltpu.VMEM((n_sub_max * sub, d), BF16),
                            pltpu.VMEM((n_sub_max * sub, d), F32)]),
        out_shape=jax.ShapeDtypeStruct((n, l, d), F32),
        compiler_params=_cparams("parallel", "parallel", "arbitrary", "arbitrary"),
        name="moe_routed_ln",
    )(cnt_flat, x, sc, sh, gt, gate, pos, post, wg_bf, wu_bf, wd_bf, ln_g.reshape(1, d), ln_b.reshape(1, d))


PAGES_PER_BLOCK = MOBA_BLOCK // PAGE_SIZE
KMEAN_PAGES = 32


def _kmean_kernel(pt_ref, *refs):
    o_ref = refs[-1]
    g = pl.program_id(1)

    @pl.when(g == 0)
    def _():
        o_ref[...] = jnp.zeros_like(o_ref)

    lane = lax.broadcasted_iota(I32, o_ref.shape[1:], 1)
    out = o_ref[0]
    for blk in range(KMEAN_PAGES // PAGES_PER_BLOCK):
        tot = functools.reduce(lambda a, c: a + c,
                               [refs[blk * PAGES_PER_BLOCK + r][...] for r in range(PAGES_PER_BLOCK)])
        col = jnp.sum(tot, axis=-1, keepdims=True).reshape(-1, 1) * (1.0 / MOBA_BLOCK)
        out = jnp.where(lane == g * (KMEAN_PAGES // PAGES_PER_BLOCK) + blk, col, out)
    o_ref[0] = out


def _kmean_call(cache_kt, layer, page_table, n_full):
    n_seq = page_table.shape[0]
    _, _, n_h, hd, ps = cache_kt.shape
    page = lambda r: pl.BlockSpec((None, None, n_h, hd, ps),
                                  lambda b, g, pt: (layer, pt[b, g * KMEAN_PAGES + r], 0, 0, 0))
    blocks_per_step = KMEAN_PAGES // PAGES_PER_BLOCK
    return pl.pallas_call(
        _kmean_kernel,
        grid_spec=pltpu.PrefetchScalarGridSpec(
            num_scalar_prefetch=1,
            grid=(n_seq, n_full // blocks_per_step),
            in_specs=[page(r) for r in range(KMEAN_PAGES)],
            out_specs=pl.BlockSpec((1, n_h * hd, n_full), lambda b, g, pt: (b, 0, 0))),
        out_shape=jax.ShapeDtypeStruct((n_seq, n_h * hd, n_full), F32),
        compiler_params=_cparams("parallel", "arbitrary"),
        name="moba_kmean",
    )(page_table, *([cache_kt] * KMEAN_PAGES))


def _top3_kernel(q_ref, km_ref, o_ref):
    n_h = q_ref.shape[1]
    for h in range(n_h):
        g = _dot(q_ref[0, h], km_ref[0, h], precision=HI)
        lane = lax.broadcasted_iota(I32, g.shape, 1)
        out_lane = lax.broadcasted_iota(I32, (g.shape[0], LANES), 1)
        out = jnp.zeros((g.shape[0], LANES), I32)
        for s in range(MOBA_TOPK):
            mx = jnp.max(g, axis=1, keepdims=True)
            idx = jnp.min(jnp.where(g == mx, lane, g.shape[1]), axis=1, keepdims=True)
            out = jnp.where(out_lane == s, idx, out)
            g = jnp.where(lane == idx, -jnp.inf, g)
        o_ref[0, h] = out


def _top3_call(q, k_mean_t):
    n_seq, n_h, t, hd = q.shape
    n_full = k_mean_t.shape[3]
    return pl.pallas_call(
        _top3_kernel,
        grid=(n_seq,),
        in_specs=[pl.BlockSpec((1, n_h, t, hd), lambda b: (b, 0, 0, 0)),
                  pl.BlockSpec((1, n_h, hd, n_full), lambda b: (b, 0, 0, 0))],
        out_specs=pl.BlockSpec((1, n_h, t, LANES), lambda b: (b, 0, 0, 0)),
        out_shape=jax.ShapeDtypeStruct((n_seq, n_h, t, LANES), I32),
        compiler_params=_cparams("parallel"),
        name="moba_top3",
    )(q, k_mean_t)


def _moba_sample_kernel(pt_ref, sel_ref, rb_ref, qt_ref, knt_ref, vnt_ref, town_ref, tprev_ref, ck_hbm, cv_hbm,
                        o_ref, kbuf, vbuf, sems, *, layer, n_seq, n_h, n_t, n_full):
    n_pg = MOBA_TOPK * PAGES_PER_BLOCK
    n_copies = n_t * n_pg
    b = pl.program_id(0)
    h = pl.program_id(1)
    step = b * n_h + h
    slot = step % 2

    def page_copies(step_f, slot_f):
        b_f = step_f // n_h
        h_f = step_f % n_h
        copies = []
        for t in range(n_t):
            for s in range(MOBA_TOPK):
                blk_id = sel_ref[(step_f * n_t + t) * MOBA_TOPK + s]
                for r in range(PAGES_PER_BLOCK):
                    pg = pt_ref[b_f, blk_id * PAGES_PER_BLOCK + r]
                    j = (t * MOBA_TOPK + s) * PAGES_PER_BLOCK + r
                    copies.append(pltpu.make_async_copy(ck_hbm.at[layer, pg, h_f], kbuf.at[slot_f, j], sems.at[0, slot_f]))
                    copies.append(pltpu.make_async_copy(cv_hbm.at[layer, pg, h_f], vbuf.at[slot_f, j], sems.at[1, slot_f]))
        return copies

    @pl.when(step == 0)
    def _():
        for c in page_copies(step, slot):
            c.start()

    @pl.when(step + 1 < n_seq * n_h)
    def _():
        for c in page_copies(step + 1, 1 - slot):
            c.start()

    for j in range(n_copies):
        pltpu.make_async_copy(ck_hbm.at[layer, 0, 0], kbuf.at[slot, j], sems.at[0, slot]).wait()
        pltpu.make_async_copy(cv_hbm.at[layer, 0, 0], vbuf.at[slot, j], sems.at[1, slot]).wait()

    k_refs = [kbuf.at[slot, j] for j in range(n_copies)]
    v_refs = [vbuf.at[slot, j] for j in range(n_copies)]
    qt = qt_ref[0, 0] * ATT_SCALE
    knt = knt_ref[0, 0]
    vnt = vnt_ref[0, 0]
    b_far = rb_ref[N_BUCKETS - 1, h]
    out_lane = lax.broadcasted_iota(I32, (HEAD_DIM, n_t), 1)
    out = jnp.zeros((HEAD_DIM, n_t), F32)
    for t in range(n_t):
        q_col = qt[:, t:t + 1]
        base = ((b * n_h + h) * n_t + t) * MOBA_TOPK
        scores = [jnp.sum(knt * q_col, axis=0, keepdims=True) + town_ref[0, t:t + 1, :]]
        for s in range(MOBA_TOPK):
            last = sel_ref[base + s] == n_full - 1
            for r in range(PAGES_PER_BLOCK):
                sc = jnp.sum(k_refs[(t * MOBA_TOPK + s) * PAGES_PER_BLOCK + r][...] * q_col, axis=0, keepdims=True)
                bias = jnp.where(last, tprev_ref[0, t:t + 1, r * PAGE_SIZE:(r + 1) * PAGE_SIZE], b_far)
                scores.append(sc + bias)
        m = functools.reduce(jnp.maximum, [jnp.max(sc, axis=1, keepdims=True) for sc in scores])
        probs = [jnp.exp(sc - m) for sc in scores]
        denom = functools.reduce(lambda a, c: a + c, [jnp.sum(p, axis=1, keepdims=True) for p in probs])
        acc = functools.reduce(lambda a, c: a + c,
                               [probs[1 + j] * v_refs[t * n_pg + j][...] for j in range(n_pg)])
        col = (jnp.sum(acc, axis=1, keepdims=True) + jnp.sum(probs[0] * vnt, axis=1, keepdims=True)) / denom
        out = jnp.where(out_lane == t, col, out)
    o_ref[0, 0] = out


def _moba_sample_call(qt, knt, vnt, cache_kt, cache_vt, layer, page_table, sel_flat, rel_bias, ts_own, ts_prev, n_full):
    n_seq, n_h, hd, n_t = qt.shape
    ps = cache_kt.shape[4]
    tok = pl.BlockSpec((1, 1, hd, n_t), lambda b, h, pt, sel: (b, h, 0, 0))
    n_copies = n_t * MOBA_TOPK * PAGES_PER_BLOCK
    return pl.pallas_call(
        functools.partial(_moba_sample_kernel, layer=layer, n_seq=n_seq, n_h=n_h, n_t=n_t, n_full=n_full),
        grid_spec=pltpu.PrefetchScalarGridSpec(
            num_scalar_prefetch=2,
            grid=(n_seq, n_h),
            in_specs=[pl.BlockSpec(memory_space=pltpu.SMEM), tok, tok, tok,
                      pl.BlockSpec((1, n_t, n_t), lambda b, h, pt, sel: (h, 0, 0)),
                      pl.BlockSpec((1, n_t, MOBA_BLOCK), lambda b, h, pt, sel: (h, 0, 0)),
                      pl.BlockSpec(memory_space=pl.ANY), pl.BlockSpec(memory_space=pl.ANY)],
            out_specs=tok,
            scratch_shapes=[pltpu.VMEM((2, n_copies, hd, ps), F32), pltpu.VMEM((2, n_copies, hd, ps), F32),
                            pltpu.SemaphoreType.DMA((2, 2))]),
        out_shape=jax.ShapeDtypeStruct((n_seq, n_h, hd, n_t), F32),
        compiler_params=_cparams("arbitrary", "arbitrary"),
        name="moba_sample",
    )(page_table, sel_flat, rel_bias, qt, knt, vnt, ts_own, ts_prev, cache_kt, cache_vt)


def _moba_sample(p_att, cache_kt, cache_vt, layer, page_table, rel_bias, ts_own, ts_prev):
    n_seq, n_t, _ = p_att.shape
    n_full = page_table.shape[1] // PAGES_PER_BLOCK
    qkv = p_att.reshape(n_seq, n_t, 3, H_ATT, HEAD_DIM).transpose(2, 0, 3, 1, 4)
    q, k_new, v_new = qkv[0], qkv[1], qkv[2]
    k_mean_t = _kmean_call(cache_kt, layer, page_table, n_full).reshape(n_seq, H_ATT, HEAD_DIM, n_full)
    picks = _top3_call(q, k_mean_t)
    sel_flat = picks[..., :MOBA_TOPK].reshape(-1)
    tr = lambda z: jnp.swapaxes(z, 2, 3)
    out_t = _moba_sample_call(tr(q), tr(k_new), tr(v_new), cache_kt, cache_vt, layer, page_table, sel_flat, rel_bias,
                              ts_own, ts_prev, n_full)
    return out_t.transpose(0, 3, 1, 2).reshape(n_seq, n_t, D_ATT), k_new, v_new


PROMPT_TM = 512
PROMPT_KEY_GROUP = 4
PROMPT_FF_TM = 1024
PROMPT_MOE_TM = 1024
PROMPT_MOE_SUB = 320
PROMPT_MOE_FF_TILE = D_FF // 2
PROMPT_RW_TC = 256
PROMPT_RW_TL = 1024
PROMPT_RW_CHUNK = 64
PROMPT_RW_UNROLL = 16
FF_TILE = 256


def _to_pages(z):
    b, s, _ = z.shape
    return z.reshape(b, s // PAGE_SIZE, PAGE_SIZE, H_ATT, HEAD_DIM).transpose(0, 1, 3, 2, 4)


def kernel(x_prompt, x_sample, cache_k, cache_v, state_wkv, state_shift, page_table, c_prompt, c_sample, rel_bias, w_ada, b_ada, w_in, rw_mu, rw_w0, rw_w2, rw_a0, rw_a2, rw_g2, rw_k_k, rw_k_a, rw_r_k, rw_lnx_g, rw_lnx_b, w_out, ln1_g, ln1_b, ln2_g, ln2_b, ffn_w_gate, ffn_w_up, ffn_w_down, moe_w_router, moe_w_gate, moe_w_up, moe_w_down):
    n_p, seq, d = x_prompt.shape
    n_s, t_s, _ = x_sample.shape
    depth = w_in.shape[0]
    assert page_table.shape[1] % PAGES_PER_BLOCK == 0 and seq % MOBA_BLOCK == 0
    rows_s = n_s * t_s

    n_c = n_p + n_s
    pad_c = -n_c % 8
    c_all = jnp.pad(jnp.concatenate([c_prompt, c_sample], axis=0), ((0, pad_c), (0, 0)))
    mods = _ada_call(c_all, w_ada, b_ada)

    t_own, t_prev, ts_own, ts_prev = _bias_call(
        rel_bias, [(MOBA_BLOCK, MOBA_BLOCK), (MOBA_BLOCK, MOBA_BLOCK), (t_s, t_s), (t_s, MOBA_BLOCK)],
        [0, MOBA_BLOCK, 0, MOBA_BLOCK], key_major=[True, True, False, False])

    cache_kt = jnp.swapaxes(cache_k, 3, 4)
    cache_vt = jnp.swapaxes(cache_v, 3, 4)

    xp = x_prompt
    xs = x_sample.reshape(1, rows_s, d)
    zero_shift = jnp.zeros((n_p, RW_IN), F32)
    zero_state = jnp.zeros((n_p, H_RWKV, HEAD_DIM, HEAD_DIM), F32)
    outs = {k: [] for k in ("kp", "vp", "ks", "vs", "wp", "ws", "hp", "hs")}
    for l in range(depth):
        mod_p = [m[:, None, :] for m in jnp.split(mods[l, :n_p], 6, axis=-1)]
        mod_s = [jnp.repeat(m, t_s, axis=0)[None] for m in jnp.split(mods[l, n_p:n_c], 6, axis=-1)]
        prm = dict(mu=rw_mu[l], w0=rw_w0[l], w2=rw_w2[l], a0=rw_a0[l], a2=rw_a2[l], g2=rw_g2[l], k_k=rw_k_k[l],
                   k_a=rw_k_a[l], r_k=rw_r_k[l], lnx_g=rw_lnx_g[l], lnx_b=rw_lnx_b[l])
        w_in_bf = w_in[l].astype(BF16)
        w_out_bf = w_out[l].astype(BF16)
        if l % 2 == 0:
            w_router = jnp.zeros((d, LANES), F32)
            wg, wu, wd = (w[l // 2][None].astype(BF16) for w in (ffn_w_gate, ffn_w_up, ffn_w_down))
        else:
            w_router = jnp.pad(moe_w_router[l // 2], ((0, 0), (0, LANES - N_EXPERTS)))
            wg, wu, wd = (w[l // 2].astype(BF16) for w in (moe_w_gate, moe_w_up, moe_w_down))

        pa, pr = _inproj_call(xp, mod_p[1], mod_p[0], w_in_bf, PROMPT_TM)
        y_att = _moba_prompt_call(pa, rel_bias, t_own, t_prev)
        y_rw, s_new, sh_new = _rwkv_mix(pr, zero_shift, zero_state, prm, PROMPT_RW_TC, PROMPT_RW_TL, PROMPT_RW_CHUNK,
                                        PROMPT_RW_UNROLL)
        xp = _outproj_call(y_att, y_rw, xp, mod_p[2], w_out_bf, ln1_g[l], ln1_b[l], PROMPT_TM)
        if l % 2 == 0:
            xp = _ffn_call(xp, mod_p[4], mod_p[3], mod_p[5], w_router, wg, wu, wd, ln2_g[l], ln2_b[l], PROMPT_FF_TM,
                           FF_TILE)
        else:
            xp = _moe_routed_call(xp, mod_p[4], mod_p[3], mod_p[5], w_router, wg, wu, wd, ln2_g[l], ln2_b[l],
                                  PROMPT_MOE_TM, PROMPT_MOE_FF_TILE, PROMPT_MOE_SUB)
        outs["kp"].append(_to_pages(pa[..., D_ATT:2 * D_ATT]))
        outs["vp"].append(_to_pages(pa[..., 2 * D_ATT:]))
        outs["wp"].append(s_new)
        outs["hp"].append(sh_new)

        pa, pr = _inproj_call(xs, mod_s[1], mod_s[0], w_in_bf, rows_s)
        y_att, k_new, v_new = _moba_sample(pa.reshape(n_s, t_s, ATT_IN), cache_kt, cache_vt, l, page_table, rel_bias,
                                           ts_own, ts_prev)
        y_rw, s_new, sh_new = _rwkv_mix(pr.reshape(n_s, t_s, RW_IN), state_shift[l], state_wkv[l], prm, t_s, t_s, t_s)
        xs = _outproj_call(y_att.reshape(1, rows_s, D_ATT), y_rw.reshape(1, rows_s, D_RWKV), xs, mod_s[2], w_out_bf,
                           ln1_g[l], ln1_b[l], rows_s)
        xs = _ffn_call(xs, mod_s[4], mod_s[3], mod_s[5], w_router, wg, wu, wd, ln2_g[l], ln2_b[l], rows_s, FF_TILE)
        outs["ks"].append(k_new)
        outs["vs"].append(v_new)
        outs["ws"].append(s_new)
        outs["hs"].append(sh_new)

    st = lambda k: jnp.stack(outs[k])
    return (xp, xs.reshape(n_s, t_s, d), st("kp"), st("vp"), st("ks"), st("vs"), st("wp"), st("ws"), st("hp"), st("hs"))
```
